```python
import math
import jax, jax.numpy as jnp
from jax import lax
import numpy as np

D_MODEL = 4096
BATCH = 1
SEQ = 8192
DEPTH = 4

MIXER_ORDER = ("ssd", "pool", "moba")
RMS_EPS = 1e-6

SSD_EXPAND = 2
SSD_D_INNER = SSD_EXPAND * D_MODEL
SSD_HEAD_DIM = 64
SSD_N_HEADS = SSD_D_INNER // SSD_HEAD_DIM
SSD_N_GROUPS = 8
SSD_HEADS_PER_GROUP = SSD_N_HEADS // SSD_N_GROUPS
SSD_D_STATE = 128
SSD_CONV_WIDTH = 4
SSD_CONV_DIM = SSD_D_INNER + 2 * SSD_N_GROUPS * SSD_D_STATE
SSD_IN_DIM = SSD_D_INNER + SSD_CONV_DIM + SSD_N_HEADS
SSD_CHUNK = 128

POOL_WIDTH = 2 * D_MODEL
POOL_WINDOWS = (2, 4, 8, 16)
POOL_GROUPS = len(POOL_WINDOWS)
POOL_GROUP_DIM = POOL_WIDTH // POOL_GROUPS

MOBA_HEAD_DIM = 128
MOBA_WIDTH = D_MODEL
MOBA_N_HEADS = MOBA_WIDTH // MOBA_HEAD_DIM
MOBA_BLOCK = 256
MOBA_TOPK = 3
MOBA_Q_CHUNK = 16

kernel_name = "hybrid_ssd_pool_moba_trunk"


def rms_norm(x, w):
    xf = x.astype(jnp.float32)
    y = xf * lax.rsqrt(jnp.mean(xf * xf, axis=-1, keepdims=True) + RMS_EPS)
    return (y * w.astype(jnp.float32)).astype(x.dtype)


def causal_dwconv(x, w, b):
    k_w = w.shape[0]
    s = x.shape[1]
    xp = jnp.pad(x, ((0, 0), (k_w - 1, 0), (0, 0)))
    y = b
    for k in range(k_w):
        y = y + xp[:, k:k + s] * w[k]
    return y


def ssd_scan(x, a, bm, cm):
    b, s, g, r, p = x.shape
    n = bm.shape[-1]
    l = SSD_CHUNK
    c = s // l
    x = x.reshape(b, c, l, g, r, p)
    bm = bm.reshape(b, c, l, g, n)
    cm = cm.reshape(b, c, l, g, n)
    a = a.reshape(b, c, l, g, r).transpose(0, 1, 3, 4, 2)
    a_cs = jnp.cumsum(a, axis=-1)
    seg = a_cs[..., :, None] - a_cs[..., None, :]
    tril = jnp.tril(jnp.ones((l, l), dtype=bool))
    decay_in = jnp.exp(jnp.where(tril, seg, -jnp.inf))
    cb = jnp.einsum("bclgn,bcsgn->bcgls", cm, bm)
    y_diag = jnp.einsum("bcgls,bcgrls,bcsgrp->bclgrp", cb, decay_in, x)
    decay_to_end = jnp.exp(a_cs[..., -1:] - a_cs)
    chunk_states = jnp.einsum("bclgn,bcgrl,bclgrp->bcgrpn", bm, decay_to_end, x)
    chunk_decay = jnp.exp(a_cs[..., -1])

    def step(h, inp):
        st, dec = inp
        return h * dec[..., None, None] + st, h

    h0 = jnp.zeros((b, g, r, p, n), dtype=chunk_states.dtype)
    _, prev = lax.scan(step, h0, (jnp.moveaxis(chunk_states, 1, 0),
                                  jnp.moveaxis(chunk_decay, 1, 0)))
    prev = jnp.moveaxis(prev, 0, 1)
    y_off = jnp.einsum("bclgn,bcgrpn,bcgrl->bclgrp", cm, prev, jnp.exp(a_cs))
    return (y_diag + y_off).reshape(b, s, g, r, p)


def ssd_mixer(u, w_in, conv_w, conv_b, dt_bias, a_log, d_skip, norm_w, w_out):
    b, s, _ = u.shape
    G, R, P, N = SSD_N_GROUPS, SSD_HEADS_PER_GROUP, SSD_HEAD_DIM, SSD_D_STATE
    zxbcdt = u @ w_in
    z = zxbcdt[..., :SSD_D_INNER]
    xbc = zxbcdt[..., SSD_D_INNER:SSD_D_INNER + SSD_CONV_DIM]
    dt = zxbcdt[..., SSD_D_INNER + SSD_CONV_DIM:]
    xbc = jax.nn.silu(causal_dwconv(xbc, conv_w, conv_b))
    xs = xbc[..., :SSD_D_INNER].reshape(b, s, G, R, P)
    bm = xbc[..., SSD_D_INNER:SSD_D_INNER + G * N].reshape(b, s, G, N)
    cm = xbc[..., SSD_D_INNER + G * N:].reshape(b, s, G, N)
    dt = jax.nn.softplus(dt.astype(jnp.float32) + dt_bias.astype(jnp.float32)).reshape(b, s, G, R)
    a = -jnp.exp(a_log.astype(jnp.float32)).reshape(G, R)
    y = ssd_scan(xs * dt[..., None], dt * a, bm, cm)
    y = y + xs * d_skip.reshape(G, R)[:, :, None]
    y = y.reshape(b, s, SSD_D_INNER)
    yz = (y * jax.nn.silu(z.astype(y.dtype))).reshape(b, s, G, SSD_D_INNER // G)
    yf = yz.astype(jnp.float32)
    yf = yf * lax.rsqrt(jnp.mean(yf * yf, axis=-1, keepdims=True) + RMS_EPS)
    y = (yf.reshape(b, s, SSD_D_INNER) * norm_w.astype(jnp.float32)).astype(u.dtype)
    return y @ w_out


def pool_mixer(u, w_in, w_grp, b_grp, scale, w_out):
    b, s, _ = u.shape
    proj = u @ w_in
    v, z = proj[..., :POOL_WIDTH], proj[..., POOL_WIDTH:]
    cs = jnp.cumsum(v.astype(jnp.float32), axis=1)
    pos = jnp.arange(s)
    groups = []
    for gi, w in enumerate(POOL_WINDOWS):
        sl = slice(gi * POOL_GROUP_DIM, (gi + 1) * POOL_GROUP_DIM)
        cs_g = cs[..., sl]
        prev = jnp.pad(cs_g, ((0, 0), (w, 0), (0, 0)))[:, :s]
        count = jnp.minimum(pos + 1, w).astype(jnp.float32)[None, :, None]
        groups.append((cs_g - prev) / count - v[..., sl].astype(jnp.float32))
    d = jnp.stack(groups, axis=2).astype(u.dtype)
    mixed = jnp.einsum("bsgc,gcd->bsgd", d, w_grp) + b_grp
    mixed = mixed.reshape(b, s, POOL_WIDTH) * scale
    return (mixed * jax.nn.silu(z)) @ w_out


def moba_mixer(u, w_in, q_norm_w, k_norm_w, w_out):
    b, s, _ = u.shape
    H, Dh, BLK, Q = MOBA_N_HEADS, MOBA_HEAD_DIM, MOBA_BLOCK, MOBA_Q_CHUNK
    proj = u @ w_in
    q = rms_norm(proj[..., 0 * MOBA_WIDTH:1 * MOBA_WIDTH].reshape(b, s, H, Dh), q_norm_w)
    k = rms_norm(proj[..., 1 * MOBA_WIDTH:2 * MOBA_WIDTH].reshape(b, s, H, Dh), k_norm_w)
    v = proj[..., 2 * MOBA_WIDTH:3 * MOBA_WIDTH].reshape(b, s, H, Dh)
    gate = proj[..., 3 * MOBA_WIDTH:]
    s_pad = -(-s // BLK) * BLK
    pad = ((0, 0), (0, s_pad - s), (0, 0), (0, 0))
    q = jnp.pad(q, pad).transpose(0, 2, 1, 3)
    k = jnp.pad(k, pad).transpose(0, 2, 1, 3)
    v = jnp.pad(v, pad).transpose(0, 2, 1, 3)
    nb = s_pad // BLK
    kb = k.reshape(b, H, nb, BLK, Dh)
    vb = v.reshape(b, H, nb, BLK, Dh)
    k_mean = jnp.mean(kb.astype(jnp.float32), axis=3).astype(k.dtype)
    gscore = jnp.einsum("bhsd,bhnd->bhsn", q, k_mean).astype(jnp.float32)
    q_blk = jnp.arange(s_pad) // BLK
    past = jnp.arange(nb)[None, :] < q_blk[:, None]
    gscore = jnp.where(past, gscore, -jnp.inf)
    k_sel = min(MOBA_TOPK, nb)
    _, sel = lax.top_k(gscore, k_sel)
    sel_valid = sel < q_blk[None, None, :, None]
    scale = 1.0 / math.sqrt(Dh)
    bi = jnp.arange(b)[:, None, None, None]
    hi = jnp.arange(H)[None, :, None, None]

    def chunk_fn(c):
        start = c * Q
        qc = lax.dynamic_slice_in_dim(q, start, Q, axis=2)
        sel_c = lax.dynamic_slice_in_dim(sel, start, Q, axis=2)
        valid_c = lax.dynamic_slice_in_dim(sel_valid, start, Q, axis=2)
        own = start // BLK
        k_own = lax.dynamic_index_in_dim(kb, own, axis=2, keepdims=False)
        v_own = lax.dynamic_index_in_dim(vb, own, axis=2, keepdims=False)
        k_g = kb[bi, hi, sel_c]
        v_g = vb[bi, hi, sel_c]
        s_sel = jnp.einsum("bhqd,bhqkjd->bhqkj", qc, k_g).astype(jnp.float32) * scale
        s_sel = jnp.where(valid_c[..., None], s_sel, -jnp.inf).reshape(b, H, Q, k_sel * BLK)
        s_own = jnp.einsum("bhqd,bhjd->bhqj", qc, k_own).astype(jnp.float32) * scale
        qpos = start + jnp.arange(Q)
        kpos = own * BLK + jnp.arange(BLK)
        s_own = jnp.where(kpos[None, :] <= qpos[:, None], s_own, -jnp.inf)
        p = jax.nn.softmax(jnp.concatenate([s_sel, s_own], axis=-1), axis=-1).astype(v.dtype)
        p_sel = p[..., :k_sel * BLK].reshape(b, H, Q, k_sel, BLK)
        p_own = p[..., k_sel * BLK:]
        return (jnp.einsum("bhqkj,bhqkjd->bhqd", p_sel, v_g)
                + jnp.einsum("bhqj,bhjd->bhqd", p_own, v_own))

    outs = lax.map(chunk_fn, jnp.arange(s_pad // Q))
    o = outs.transpose(1, 0, 3, 2, 4).reshape(b, s_pad, H * Dh)[:, :s]
    return (o * jax.nn.silu(gate)) @ w_out


MIXERS = {"ssd": ssd_mixer, "pool": pool_mixer, "moba": moba_mixer}


def setup_inputs(seed: int = 0) -> dict:
    key = jax.random.key(seed)
    keys = jax.random.split(key, 64)
    counter = iter(range(64))

    def nk():
        return keys[next(counter)]

    def nrm(shape, sc):
        return jax.random.normal(nk(), shape, jnp.float32) * sc

    p = {}
    p["x"] = nrm((BATCH, SEQ, D_MODEL), 1.0)

    def add_ssd(name):
        H = SSD_N_HEADS
        p[name + "_w_in"] = nrm((D_MODEL, SSD_IN_DIM), D_MODEL ** -0.5)
        p[name + "_conv_w"] = nrm((SSD_CONV_WIDTH, SSD_CONV_DIM), SSD_CONV_WIDTH ** -0.5)
        p[name + "_conv_b"] = nrm((SSD_CONV_DIM,), 0.02)
        dt = jnp.exp(jax.random.uniform(nk(), (H,), jnp.float32)
                     * (math.log(0.1) - math.log(0.001)) + math.log(0.001))
        p[name + "_dt_bias"] = dt + jnp.log(-jnp.expm1(-dt))
        p[name + "_a_log"] = jnp.log(jax.random.uniform(nk(), (H,), jnp.float32, minval=1.0, maxval=16.0))
        p[name + "_d"] = 1.0 + nrm((H,), 0.02)
        p[name + "_norm_w"] = 1.0 + nrm((SSD_D_INNER,), 0.02)
        p[name + "_w_out"] = nrm((SSD_D_INNER, D_MODEL), SSD_D_INNER ** -0.5)

    p["norm0"] = 1.0 + nrm((D_MODEL,), 0.02)
    add_ssd("ssd0")
    p["norm1"] = 1.0 + nrm((D_MODEL,), 0.02)
    p["pool1_w_in"] = nrm((D_MODEL, 2 * POOL_WIDTH), D_MODEL ** -0.5)
    p["pool1_w_grp"] = nrm((POOL_GROUPS, POOL_GROUP_DIM, POOL_GROUP_DIM), POOL_GROUP_DIM ** -0.5)
    p["pool1_b_grp"] = nrm((POOL_GROUPS, POOL_GROUP_DIM), 0.02)
    p["pool1_scale"] = 1.0 + nrm((POOL_WIDTH,), 0.02)
    p["pool1_w_out"] = nrm((POOL_WIDTH, D_MODEL), POOL_WIDTH ** -0.5)
    p["norm2"] = 1.0 + nrm((D_MODEL,), 0.02)
    p["moba2_w_in"] = nrm((D_MODEL, 4 * MOBA_WIDTH), D_MODEL ** -0.5)
    p["moba2_q_norm"] = 1.0 + nrm((MOBA_HEAD_DIM,), 0.02)
    p["moba2_k_norm"] = 1.0 + nrm((MOBA_HEAD_DIM,), 0.02)
    p["moba2_w_out"] = nrm((MOBA_WIDTH, D_MODEL), MOBA_WIDTH ** -0.5)
    p["norm3"] = 1.0 + nrm((D_MODEL,), 0.02)
    add_ssd("ssd3")
    return p


def reference(x,
              norm0, ssd0_w_in, ssd0_conv_w, ssd0_conv_b, ssd0_dt_bias, ssd0_a_log, ssd0_d, ssd0_norm_w, ssd0_w_out,
              norm1, pool1_w_in, pool1_w_grp, pool1_b_grp, pool1_scale, pool1_w_out,
              norm2, moba2_w_in, moba2_q_norm, moba2_k_norm, moba2_w_out,
              norm3, ssd3_w_in, ssd3_conv_w, ssd3_conv_b, ssd3_dt_bias, ssd3_a_log, ssd3_d, ssd3_norm_w, ssd3_w_out):
    layer_params = [
        (norm0, (ssd0_w_in, ssd0_conv_w, ssd0_conv_b, ssd0_dt_bias, ssd0_a_log, ssd0_d, ssd0_norm_w, ssd0_w_out)),
        (norm1, (pool1_w_in, pool1_w_grp, pool1_b_grp, pool1_scale, pool1_w_out)),
        (norm2, (moba2_w_in, moba2_q_norm, moba2_k_norm, moba2_w_out)),
        (norm3, (ssd3_w_in, ssd3_conv_w, ssd3_conv_b, ssd3_dt_bias, ssd3_a_log, ssd3_d, ssd3_norm_w, ssd3_w_out)),
    ]
    h = x
    for i in range(DEPTH):
        g, params = layer_params[i]
        mixer = MIXERS[MIXER_ORDER[i % len(MIXER_ORDER)]]
        h = h + mixer(rms_norm(h, g), *params)
    return h
```

```python
import functools
import math

import jax
import jax.numpy as jnp
from jax import lax
from jax.experimental import pallas as pl
from jax.experimental.pallas import tpu as pltpu

RMS_EPS = 1e-6

SSD_HEAD_DIM = 64
SSD_N_GROUPS = 8
SSD_D_STATE = 128
SSD_CONV_WIDTH = 4
SSD_CHUNK = 128

POOL_WINDOWS = (2, 4, 8, 16)
POOL_HALO = 16

MOBA_HEAD_DIM = 128
MOBA_BLOCK = 256
MOBA_TOPK = 3

LANES = 128
SUBLANES = 8
VMEM_LIMIT_BYTES = 56 * 1024 * 1024

BF16 = jnp.bfloat16
F32 = jnp.float32


def _params(*semantics):
    return pltpu.CompilerParams(dimension_semantics=semantics,
                                vmem_limit_bytes=VMEM_LIMIT_BYTES)


def _sigmoid(x):
    return 1.0 / (1.0 + jnp.exp(-x))


def _silu(x):
    return x * _sigmoid(x)


def _softplus(x):
    return jnp.maximum(x, 0.0) + jnp.log(1.0 + jnp.exp(-jnp.abs(x)))


def _split3(x):
    hi = x.astype(BF16)
    r1 = x - hi.astype(F32)
    mid = r1.astype(BF16)
    lo = (r1 - mid.astype(F32)).astype(BF16)
    return hi, mid, lo


def _rmsnorm_kernel(x_ref, w_ref, o_ref):
    x = x_ref[...]
    ms = jnp.mean(x * x, axis=-1, keepdims=True)
    o_ref[...] = (x * lax.rsqrt(ms + RMS_EPS) * w_ref[...]).astype(o_ref.dtype)


def rmsnorm(x, w, *, tm=256):
    m, d = x.shape
    tm = min(tm, m)
    return pl.pallas_call(
        _rmsnorm_kernel,
        grid=(m // tm,),
        in_specs=[pl.BlockSpec((tm, d), lambda i: (i, 0)),
                  pl.BlockSpec((1, d), lambda i: (0, 0))],
        out_specs=pl.BlockSpec((tm, d), lambda i: (i, 0)),
        out_shape=jax.ShapeDtypeStruct((m, d), BF16),
        compiler_params=_params("parallel"),
        name="rmsnorm",
    )(x, w.reshape(1, d))


def _matmul_kernel(*refs, has_residual):
    if has_residual:
        a_ref, w_ref, r_ref, o_ref = refs
    else:
        a_ref, w_ref, o_ref = refs
    acc = jnp.dot(a_ref[...], w_ref[...].astype(BF16), preferred_element_type=F32)
    if has_residual:
        acc = acc + r_ref[...]
    o_ref[...] = acc.astype(o_ref.dtype)


def matmul(a, w, *, k, n_out, tm, tn, a_kblk=0, w_kblk=0, w_col0=0, residual=None,
           out_dtype=F32, name="matmul"):
    m = a.shape[0]
    tm = min(tm, m)
    tn = min(tn, n_out)
    assert m % tm == 0 and n_out % tn == 0 and w_col0 % tn == 0
    col0_blk = w_col0 // tn
    in_specs = [pl.BlockSpec((tm, k), lambda i, j: (i, a_kblk), pipeline_mode=pl.Buffered(1)),
                pl.BlockSpec((k, tn), lambda i, j: (w_kblk, j + col0_blk))]
    args = [a, w]
    if residual is not None:
        in_specs.append(pl.BlockSpec((tm, tn), lambda i, j: (i, j)))
        args.append(residual)
    return pl.pallas_call(
        functools.partial(_matmul_kernel, has_residual=residual is not None),
        grid=(m // tm, n_out // tn),
        in_specs=in_specs,
        out_specs=pl.BlockSpec((tm, tn), lambda i, j: (i, j)),
        out_shape=jax.ShapeDtypeStruct((m, n_out), out_dtype),
        compiler_params=_params("parallel", "arbitrary"),
        name=name,
    )(*args)


def _conv_silu(raw, carry_ref, w_ref, b_ref):
    l = raw.shape[0]
    carry = carry_ref[...]
    w = w_ref[...]
    acc = b_ref[...] + raw * w[SSD_CONV_WIDTH - 1:SSD_CONV_WIDTH, :]
    row8 = lax.broadcasted_iota(jnp.int32, (SUBLANES, raw.shape[1]), 0)
    for j in range(1, SSD_CONV_WIDTH):
        rolled = pltpu.roll(raw, j, 0)
        head = jnp.where(row8 < j, pltpu.roll(carry, j, 0), rolled[:SUBLANES])
        shifted = jnp.concatenate([head, rolled[SUBLANES:]], axis=0)
        acc = acc + shifted * w[SSD_CONV_WIDTH - 1 - j:SSD_CONV_WIDTH - j, :]
    carry_ref[...] = raw[l - SUBLANES:, :]
    return _silu(acc)


def _ssd_kernel(z_ref, x_ref, b_ref, c_ref, dt_ref, dtt_ref,
                cwx_ref, cbx_ref, cwb_ref, cbb_ref, cwc_ref, cbc_ref,
                bias_ref, biast_ref, alog_ref, alogt_ref, dexp_ref, nw_ref, e_ref,
                o_ref, state_ref, carx_ref, carb_ref, carc_ref, *, n_heads, head_dim):
    chunk = pl.program_id(1)

    @pl.when(chunk == 0)
    def _():
        state_ref[...] = jnp.zeros_like(state_ref)
        carx_ref[...] = jnp.zeros_like(carx_ref)
        carb_ref[...] = jnp.zeros_like(carb_ref)
        carc_ref[...] = jnp.zeros_like(carc_ref)

    l = x_ref.shape[0]
    gw = x_ref.shape[1]
    xs = _conv_silu(x_ref[...], carx_ref, cwx_ref, cbx_ref)
    bm = _conv_silu(b_ref[...], carb_ref, cwb_ref, cbb_ref)
    cm = _conv_silu(c_ref[...], carc_ref, cwc_ref, cbc_ref)

    dt = _softplus(dt_ref[0] + bias_ref[0])
    a = dt * (-jnp.exp(alog_ref[0]))
    dtt = _softplus(dtt_ref[0] + biast_ref[0])
    at = dtt * (-jnp.exp(alogt_ref[0]))

    row = lax.broadcasted_iota(jnp.int32, (l, l), 0)
    col = lax.broadcasted_iota(jnp.int32, (l, l), 1)
    tril = row >= col
    tril_b = tril.astype(BF16)
    triu_b = (row <= col).astype(BF16)

    a_cs = sum(jnp.dot(tril_b, p, preferred_element_type=F32) for p in _split3(a))
    a_cst = sum(jnp.dot(p, triu_b, preferred_element_type=F32) for p in _split3(at))

    a_last = a_cs[l - 1:l, :]
    dte = jnp.exp(a_last - a_cs)
    e3 = jnp.exp(a_cs)

    stacked = jnp.concatenate([dt, dte, e3], axis=0)
    hi, mid, lo = _split3(stacked)
    grp = lax.broadcasted_iota(jnp.int32, stacked.shape, 1) // n_heads
    pieces = jnp.where(grp == 0, hi, jnp.where(grp == 1, mid, lo))
    expanded = jnp.dot(pieces, e_ref[...], preferred_element_type=F32)
    dt_exp = expanded[:l]
    dte_exp = expanded[l:2 * l]
    e3_exp = expanded[2 * l:]

    xt = xs * dt_exp
    xt_b = xt.astype(BF16)
    cb = lax.dot_general(cm.astype(BF16), bm.astype(BF16), (((1,), (1,)), ((), ())),
                         preferred_element_type=F32)

    heads_per_vreg = LANES // head_dim
    lane = lax.broadcasted_iota(jnp.int32, (l, LANES), 1)
    y_cols = []
    for v in range(gw // LANES):
        x_pair = xt_b[:, v * LANES:(v + 1) * LANES]
        acc = jnp.zeros((l, LANES), F32)
        for s in range(heads_per_vreg):
            h = v * heads_per_vreg + s
            seg = a_cs[:, h:h + 1] - a_cst[h:h + 1, :]
            decay = jnp.exp(jnp.where(tril, seg, -jnp.inf))
            m_h = (cb * decay).astype(BF16)
            in_head = (lane >= s * head_dim) & (lane < (s + 1) * head_dim)
            x_h = jnp.where(in_head, x_pair, jnp.zeros_like(x_pair))
            acc = acc + jnp.dot(m_h, x_h, preferred_element_type=F32)
        y_cols.append(acc)
    y_diag = jnp.concatenate(y_cols, axis=1) if len(y_cols) > 1 else y_cols[0]

    state = state_ref[...]
    y_off = jnp.dot(cm.astype(BF16), state.astype(BF16), preferred_element_type=F32) * e3_exp
    xw_b = (xt * dte_exp).astype(BF16)
    new_states = jnp.dot(bm.T.astype(BF16), xw_b, preferred_element_type=F32)
    state_ref[...] = state * e3_exp[l - 1:l, :] + new_states

    y = y_diag + y_off + xs * dexp_ref[...]
    yz = y * _silu(z_ref[...])
    ms = jnp.mean(yz * yz, axis=-1, keepdims=True)
    o_ref[...] = (yz * lax.rsqrt(ms + RMS_EPS) * nw_ref[...]).astype(o_ref.dtype)


def ssd_core(zxbc, dt_raw, conv_w, conv_b, dt_bias, a_log, d_skip, norm_w):
    s = zxbc.shape[0]
    g, n, p, l = SSD_N_GROUPS, SSD_D_STATE, SSD_HEAD_DIM, SSD_CHUNK
    h = dt_raw.shape[1]
    r = h // g
    gw = r * p
    di = h * p
    l = min(l, s)
    assert gw % LANES == 0 and n % LANES == 0 and di == g * gw
    nx = di // gw
    nb0 = 2 * di // n
    nc0 = nb0 + g

    dtg = dt_raw.reshape(s, g, r).transpose(1, 0, 2)
    dtg3 = jnp.concatenate([dtg, dtg, dtg], axis=-1)
    dtt = dt_raw.reshape(s, g, r).transpose(1, 2, 0)
    bias = dt_bias.reshape(g, 1, r)
    bias3 = jnp.concatenate([bias, bias, bias], axis=-1)
    alog = a_log.reshape(g, 1, r)
    alog3 = jnp.concatenate([alog, alog, alog], axis=-1)
    biast = dt_bias.reshape(g, r, 1)
    alogt = a_log.reshape(g, r, 1)
    dexp = jnp.repeat(d_skip, p).reshape(1, di)
    nw = norm_w.reshape(1, di)
    cw = conv_w
    cb = conv_b.reshape(1, -1)
    head_of_col = jnp.arange(gw) // p
    e1 = (head_of_col[None, :] == jnp.arange(r)[:, None]).astype(BF16)
    e3 = jnp.concatenate([e1, e1, e1], axis=0)

    kw = SSD_CONV_WIDTH
    in_specs = [
        pl.BlockSpec((l, gw), lambda gi, c: (c, gi)),
        pl.BlockSpec((l, gw), lambda gi, c: (c, nx + gi)),
        pl.BlockSpec((l, n), lambda gi, c: (c, nb0 + gi)),
        pl.BlockSpec((l, n), lambda gi, c: (c, nc0 + gi)),
        pl.BlockSpec((1, l, 3 * r), lambda gi, c: (gi, c, 0)),
        pl.BlockSpec((1, r, l), lambda gi, c: (gi, 0, c)),
        pl.BlockSpec((kw, gw), lambda gi, c: (0, gi)),
        pl.BlockSpec((1, gw), lambda gi, c: (0, gi)),
        pl.BlockSpec((kw, n), lambda gi, c: (0, nb0 - nx * (gw // n) + gi)),
        pl.BlockSpec((1, n), lambda gi, c: (0, nb0 - nx * (gw // n) + gi)),
        pl.BlockSpec((kw, n), lambda gi, c: (0, nc0 - nx * (gw // n) + gi)),
        pl.BlockSpec((1, n), lambda gi, c: (0, nc0 - nx * (gw // n) + gi)),
        pl.BlockSpec((1, 1, 3 * r), lambda gi, c: (gi, 0, 0)),
        pl.BlockSpec((1, r, 1), lambda gi, c: (gi, 0, 0)),
        pl.BlockSpec((1, 1, 3 * r), lambda gi, c: (gi, 0, 0)),
        pl.BlockSpec((1, r, 1), lambda gi, c: (gi, 0, 0)),
        pl.BlockSpec((1, gw), lambda gi, c: (0, gi)),
        pl.BlockSpec((1, gw), lambda gi, c: (0, gi)),
        pl.BlockSpec((3 * r, gw), lambda gi, c: (0, 0)),
    ]
    return pl.pallas_call(
        functools.partial(_ssd_kernel, n_heads=r, head_dim=p),
        grid=(g, s // l),
        in_specs=in_specs,
        out_specs=pl.BlockSpec((l, gw), lambda gi, c: (c, gi)),
        out_shape=jax.ShapeDtypeStruct((s, di), BF16),
        scratch_shapes=[pltpu.VMEM((n, gw), F32),
                        pltpu.VMEM((SUBLANES, gw), F32),
                        pltpu.VMEM((SUBLANES, n), F32),
                        pltpu.VMEM((SUBLANES, n), F32)],
        compiler_params=_params("parallel", "arbitrary"),
        name="ssd_core",
    )(zxbc, zxbc, zxbc, zxbc, dtg3, dtt, cw, cb, cw, cb, cw, cb,
      bias3, biast, alog3, alogt, dexp, nw, e3)


def ssd_mixer(h, norm_w_in, w_in, conv_w, conv_b, dt_bias, a_log, d_skip, norm_w, w_out):
    d = h.shape[1]
    n_heads = dt_bias.shape[0]
    di = n_heads * SSD_HEAD_DIM
    n_main = 2 * di + 2 * SSD_N_GROUPS * SSD_D_STATE
    u = rmsnorm(h, norm_w_in)
    zxbc = matmul(u, w_in, k=d, n_out=n_main, tm=2048, tn=512, name="ssd_in_proj")
    dt_raw = matmul(u, w_in, k=d, n_out=n_heads, tm=2048, tn=LANES, w_col0=n_main,
                    name="ssd_dt_proj")
    y = ssd_core(zxbc, dt_raw, conv_w, conv_b, dt_bias, a_log, d_skip, norm_w)
    for kb in range(di // d):
        h = matmul(y, w_out, k=d, n_out=d, tm=2048, tn=256, a_kblk=kb, w_kblk=kb,
                   residual=h, name="ssd_out_proj")
    return h


def _pool_kernel(v_ref, halo_ref, z_ref, w_ref, b_ref, sc_ref, o_ref, d_ref, *, chunk):
    gi = pl.program_id(0)
    i = pl.program_id(1)
    j = pl.program_id(2)
    tm, gd = v_ref.shape

    @pl.when(j == 0)
    def _():
        window = jnp.left_shift(2, gi)
        pos = i * tm + lax.broadcasted_iota(jnp.int32, (tm, chunk), 0)
        inv_count = 1.0 / jnp.minimum(pos + 1, window).astype(F32)
        for c0 in range(0, gd, chunk):
            v = v_ref[:, c0:c0 + chunk]
            halo = jnp.where(i > 0, halo_ref[:, c0:c0 + chunk], 0.0)
            vv = jnp.concatenate([halo, v], axis=0)
            s1 = vv + pltpu.roll(vv, 1, 0)
            s2 = s1 + pltpu.roll(s1, 2, 0)
            s3 = s2 + pltpu.roll(s2, 4, 0)
            s4 = s3 + pltpu.roll(s3, 8, 0)
            sw = jnp.where(gi == 0, s1, jnp.where(gi == 1, s2, jnp.where(gi == 2, s3, s4)))
            d = sw[POOL_HALO:] * inv_count - v
            d_ref[:, c0:c0 + chunk] = d.astype(d_ref.dtype)

    acc = jnp.dot(d_ref[...], w_ref[0].astype(BF16), preferred_element_type=F32)
    acc = (acc + b_ref[0]) * sc_ref[...]
    o_ref[...] = (acc * _silu(z_ref[...])).astype(o_ref.dtype)


def pool_core(proj, w_grp, b_grp, scale, *, tm=512, tn=512):
    s = proj.shape[0]
    ng, gd, _ = w_grp.shape
    assert ng == len(POOL_WINDOWS) and POOL_WINDOWS == tuple(2 << k for k in range(ng))
    width = ng * gd
    tm = min(tm, s)
    tn = min(tn, gd)
    chunk = min(512, gd)
    nj = gd // tn
    halo_blocks = tm // POOL_HALO
    in_specs = [
        pl.BlockSpec((tm, gd), lambda g, i, j: (i, g)),
        pl.BlockSpec((POOL_HALO, gd), lambda g, i, j: (jnp.maximum(i * halo_blocks - 1, 0), g)),
        pl.BlockSpec((tm, tn), lambda g, i, j: (i, (width + g * gd) // tn + j)),
        pl.BlockSpec((1, gd, tn), lambda g, i, j: (g, 0, j)),
        pl.BlockSpec((1, 1, tn), lambda g, i, j: (g, 0, j)),
        pl.BlockSpec((1, tn), lambda g, i, j: (0, g * nj + j)),
    ]
    return pl.pallas_call(
        functools.partial(_pool_kernel, chunk=chunk),
        grid=(ng, s // tm, nj),
        in_specs=in_specs,
        out_specs=pl.BlockSpec((tm, tn), lambda g, i, j: (i, g * nj + j)),
        out_shape=jax.ShapeDtypeStruct((s, width), BF16),
        scratch_shapes=[pltpu.VMEM((tm, gd), BF16)],
        compiler_params=_params("parallel", "parallel", "arbitrary"),
        name="pool_core",
    )(proj, proj, proj, w_grp, b_grp.reshape(ng, 1, gd), scale.reshape(1, width))


def pool_mixer(h, norm_w_in, w_in, w_grp, b_grp, scale, w_out):
    d = h.shape[1]
    width = scale.shape[0]
    u = rmsnorm(h, norm_w_in)
    proj = matmul(u, w_in, k=d, n_out=2 * width, tm=2048, tn=512, name="pool_in_proj")
    y = pool_core(proj, w_grp, b_grp, scale)
    for kb in range(width // d):
        h = matmul(y, w_out, k=d, n_out=d, tm=2048, tn=256, a_kblk=kb, w_kblk=kb,
                   residual=h, name="pool_out_proj")
    return h


def _moba_prep_kernel(q_ref, k_ref, v_ref, qw_ref, kw_ref, qn_ref, kn_ref, vb_ref, km_ref):
    i = pl.program_id(0)

    @pl.when(i == 0)
    def _():
        km_ref[...] = jnp.zeros_like(km_ref)

    dh = MOBA_HEAD_DIM
    n_heads = q_ref.shape[1] // dh
    qw = qw_ref[...]
    kw = kw_ref[...]
    means = []
    for h in range(n_heads):
        sl = slice(h * dh, (h + 1) * dh)
        q = q_ref[:, sl]
        qn = q * lax.rsqrt(jnp.mean(q * q, axis=-1, keepdims=True) + RMS_EPS) * qw
        qn_ref[:, sl] = qn.astype(qn_ref.dtype)
        k = k_ref[:, sl]
        kn = k * lax.rsqrt(jnp.mean(k * k, axis=-1, keepdims=True) + RMS_EPS) * kw
        kn_ref[:, sl] = kn.astype(kn_ref.dtype)
        means.append(jnp.mean(kn, axis=0, keepdims=True))
    vb_ref[...] = v_ref[...].astype(vb_ref.dtype)
    mean_row = jnp.concatenate(means, axis=1)
    blk = lax.broadcasted_iota(jnp.int32, km_ref.shape, 0)
    km_ref[...] = jnp.where(blk == i, mean_row, km_ref[...])


def moba_prep(proj, q_norm_w, k_norm_w):
    s = proj.shape[0]
    width = proj.shape[1] // 4
    blk = MOBA_BLOCK
    nb = s // blk
    row_spec = lambda c: pl.BlockSpec((blk, width), lambda i: (i, c))
    out_row = pl.BlockSpec((blk, width), lambda i: (i, 0))
    return pl.pallas_call(
        _moba_prep_kernel,
        grid=(nb,),
        in_specs=[row_spec(0), row_spec(1), row_spec(2),
                  pl.BlockSpec((1, MOBA_HEAD_DIM), lambda i: (0, 0)),
                  pl.BlockSpec((1, MOBA_HEAD_DIM), lambda i: (0, 0))],
        out_specs=[out_row, out_row, out_row,
                   pl.BlockSpec((nb, width), lambda i: (0, 0))],
        out_shape=[jax.ShapeDtypeStruct((s, width), BF16),
                   jax.ShapeDtypeStruct((s, width), BF16),
                   jax.ShapeDtypeStruct((s, width), BF16),
                   jax.ShapeDtypeStruct((nb, width), F32)],
        compiler_params=_params("arbitrary"),
        name="moba_prep",
    )(proj, proj, proj, q_norm_w.reshape(1, -1), k_norm_w.reshape(1, -1))


def _moba_attn_kernel(q_ref, k_ref, v_ref, km_ref, gate_ref, o_ref, m_ref, l_ref, acc_ref):
    qi = pl.program_id(1)
    blk = MOBA_BLOCK
    tq = q_ref.shape[0]
    nb = km_ref.shape[0]
    scale = 1.0 / math.sqrt(MOBA_HEAD_DIM)
    nt = (((1,), (1,)), ((), ()))
    q = q_ref[...]

    gscore = lax.dot_general(q, km_ref[...].astype(BF16), nt, preferred_element_type=F32)
    bcol = lax.broadcasted_iota(jnp.int32, (tq, nb), 1)
    gwork = jnp.where(bcol < qi, gscore, -jnp.inf)
    sel = jnp.zeros((tq, nb), F32)
    for _ in range(min(MOBA_TOPK, nb)):
        top = jnp.max(gwork, axis=1, keepdims=True)
        is_top = (gwork == top) & (top > -jnp.inf)
        first = jnp.min(jnp.where(is_top, bcol, nb), axis=1, keepdims=True)
        pick = bcol == first
        sel = jnp.where(pick, 1.0, sel)
        gwork = jnp.where(pick, -jnp.inf, gwork)

    own = pl.multiple_of(qi * blk, blk)
    s_own = lax.dot_general(q, k_ref[pl.ds(own, blk), :], nt, preferred_element_type=F32) * scale
    r = lax.broadcasted_iota(jnp.int32, (tq, blk), 0)
    c = lax.broadcasted_iota(jnp.int32, (tq, blk), 1)
    s_own = jnp.where(c <= r, s_own, -jnp.inf)
    m0 = jnp.max(s_own, axis=1, keepdims=True)
    p0 = jnp.exp(s_own - m0)
    m_ref[...] = m0
    l_ref[...] = jnp.sum(p0, axis=1, keepdims=True)
    acc_ref[...] = jnp.dot(p0.astype(BF16), v_ref[pl.ds(own, blk), :], preferred_element_type=F32)

    def past_block(kj, carry):
        start = pl.multiple_of(kj * blk, blk)
        chosen = jnp.sum(jnp.where(bcol == kj, sel, 0.0), axis=1, keepdims=True)
        s = lax.dot_general(q, k_ref[pl.ds(start, blk), :], nt, preferred_element_type=F32) * scale
        s = jnp.where(chosen > 0.5, s, -jnp.inf)
        m_prev = m_ref[...]
        m_new = jnp.maximum(m_prev, jnp.max(s, axis=1, keepdims=True))
        alpha = jnp.exp(m_prev - m_new)
        p = jnp.exp(s - m_new)
        l_ref[...] = alpha * l_ref[...] + jnp.sum(p, axis=1, keepdims=True)
        acc_ref[...] = alpha * acc_ref[...] + jnp.dot(
            p.astype(BF16), v_ref[pl.ds(start, blk), :], preferred_element_type=F32)
        m_ref[...] = m_new
        return carry

    lax.fori_loop(0, qi, past_block, 0)

    out = acc_ref[...] / l_ref[...]
    o_ref[...] = (out * _silu(gate_ref[...])).astype(o_ref.dtype)


def moba_attn(qn, kn, vb, kmean, proj):
    s, width = qn.shape
    dh = MOBA_HEAD_DIM
    n_heads = width // dh
    tq = MOBA_BLOCK
    nb = kmean.shape[0]
    gate0 = 3 * n_heads
    return pl.pallas_call(
        _moba_attn_kernel,
        grid=(n_heads, s // tq),
        in_specs=[pl.BlockSpec((tq, dh), lambda h, i: (i, h)),
                  pl.BlockSpec((s, dh), lambda h, i: (0, h)),
                  pl.BlockSpec((s, dh), lambda h, i: (0, h)),
                  pl.BlockSpec((nb, dh), lambda h, i: (0, h)),
                  pl.BlockSpec((tq, dh), lambda h, i: (i, gate0 + h))],
        out_specs=pl.BlockSpec((tq, dh), lambda h, i: (i, h)),
        out_shape=jax.ShapeDtypeStruct((s, width), BF16),
        scratch_shapes=[pltpu.VMEM((tq, 1), F32),
                        pltpu.VMEM((tq, 1), F32),
                        pltpu.VMEM((tq, dh), F32)],
        compiler_params=_params("parallel", "arbitrary"),
        name="moba_attn",
    )(qn, kn, vb, kmean, proj)


def moba_mixer(h, norm_w_in, w_in, q_norm_w, k_norm_w, w_out):
    s, d = h.shape
    assert s % MOBA_BLOCK == 0
    width = w_out.shape[0]
    u = rmsnorm(h, norm_w_in)
    proj = matmul(u, w_in, k=d, n_out=4 * width, tm=2048, tn=512, name="moba_in_proj")
    qn, kn, vb, kmean = moba_prep(proj, q_norm_w, k_norm_w)
    y = moba_attn(qn, kn, vb, kmean, proj)
    for kb in range(width // d):
        h = matmul(y, w_out, k=d, n_out=d, tm=2048, tn=256, a_kblk=kb, w_kblk=kb,
                   residual=h, name="moba_out_proj")
    return h


def kernel(x, norm0, ssd0_w_in, ssd0_conv_w, ssd0_conv_b, ssd0_dt_bias, ssd0_a_log, ssd0_d, ssd0_norm_w, ssd0_w_out, norm1, pool1_w_in, pool1_w_grp, pool1_b_grp, pool1_scale, pool1_w_out, norm2, moba2_w_in, moba2_q_norm, moba2_k_norm, moba2_w_out, norm3, ssd3_w_in, ssd3_conv_w, ssd3_conv_b, ssd3_dt_bias, ssd3_a_log, ssd3_d, ssd3_norm_w, ssd3_w_out):
    b, s, d = x.shape
    outs = []
    for bi in range(b):
        h = x[bi]
        h = ssd_mixer(h, norm0, ssd0_w_in, ssd0_conv_w, ssd0_conv_b, ssd0_dt_bias,
                      ssd0_a_log, ssd0_d, ssd0_norm_w, ssd0_w_out)
        h = pool_mixer(h, norm1, pool1_w_in, pool1_w_grp, pool1_b_grp, pool1_scale, pool1_w_out)
        h = moba_mixer(h, norm2, moba2_w_in, moba2_q_norm, moba2_k_norm, moba2_w_out)
        h = ssd_mixer(h, norm3, ssd3_w_in, ssd3_conv_w, ssd3_conv_b, ssd3_dt_bias,
                      ssd3_a_log, ssd3_d, ssd3_norm_w, ssd3_w_out)
        outs.append(h)
    return jnp.stack(outs, axis=0)
```

```python
import functools
import math

import jax
import jax.numpy as jnp
from jax import lax
from jax.experimental import pallas as pl
from jax.experimental.pallas import tpu as pltpu

RMS_EPS = 1e-6

SSD_HEAD_DIM = 64
SSD_N_GROUPS = 8
SSD_D_STATE = 128
SSD_CONV_WIDTH = 4
SSD_CHUNK = 128

POOL_WINDOWS = (2, 4, 8, 16)
POOL_HALO = 16

MOBA_HEAD_DIM = 128
MOBA_BLOCK = 256
MOBA_TOPK = 3
MOBA_GROUP = 8
MOBA_MASKED = -1e30

LANES = 128
SUBLANES = 8
VMEM_LIMIT_BYTES = 56 * 1024 * 1024

BF16 = jnp.bfloat16
F32 = jnp.float32


def _params(*semantics):
    return pltpu.CompilerParams(dimension_semantics=semantics,
                                vmem_limit_bytes=VMEM_LIMIT_BYTES)


def _silu(x):
    half = 0.5 * x
    return half + half * jnp.tanh(half)


def _softplus(x):
    return jnp.maximum(x, 0.0) + jnp.log(1.0 + jnp.exp(-jnp.abs(x)))


def _split3(x):
    hi = x.astype(BF16)
    r1 = x - hi.astype(F32)
    mid = r1.astype(BF16)
    lo = (r1 - mid.astype(F32)).astype(BF16)
    return hi, mid, lo


def _rmsnorm_kernel(x_ref, w_ref, o_ref):
    x = x_ref[...]
    ms = jnp.mean(x * x, axis=-1, keepdims=True)
    o_ref[...] = (x * lax.rsqrt(ms + RMS_EPS) * w_ref[...]).astype(o_ref.dtype)


def rmsnorm(x, w, *, tm=256):
    m, d = x.shape
    tm = min(tm, m)
    return pl.pallas_call(
        _rmsnorm_kernel,
        grid=(m // tm,),
        in_specs=[pl.BlockSpec((tm, d), lambda i: (i, 0)),
                  pl.BlockSpec((1, d), lambda i: (0, 0))],
        out_specs=pl.BlockSpec((tm, d), lambda i: (i, 0)),
        out_shape=jax.ShapeDtypeStruct((m, d), BF16),
        compiler_params=_params("parallel"),
        name="rmsnorm",
    )(x, w.reshape(1, d))


def _matmul_kernel(*refs, has_residual):
    if has_residual:
        a_ref, w_ref, r_ref, o_ref = refs
    else:
        a_ref, w_ref, o_ref = refs
    acc = jnp.dot(a_ref[...], w_ref[...].astype(BF16), preferred_element_type=F32)
    if has_residual:
        acc = acc + r_ref[...]
    o_ref[...] = acc.astype(o_ref.dtype)


def matmul(a, w, *, k, n_out, tm, tn, a_kblk=0, w_kblk=0, w_col0=0, residual=None,
           out_dtype=F32, name="matmul"):
    m = a.shape[0]
    tm = min(tm, m)
    tn = min(tn, n_out)
    assert m % tm == 0 and n_out % tn == 0 and w_col0 % tn == 0
    col0_blk = w_col0 // tn
    in_specs = [pl.BlockSpec((tm, k), lambda i, j: (i, a_kblk), pipeline_mode=pl.Buffered(1)),
                pl.BlockSpec((k, tn), lambda i, j: (w_kblk, j + col0_blk))]
    args = [a, w]
    if residual is not None:
        in_specs.append(pl.BlockSpec((tm, tn), lambda i, j: (i, j)))
        args.append(residual)
    return pl.pallas_call(
        functools.partial(_matmul_kernel, has_residual=residual is not None),
        grid=(m // tm, n_out // tn),
        in_specs=in_specs,
        out_specs=pl.BlockSpec((tm, tn), lambda i, j: (i, j)),
        out_shape=jax.ShapeDtypeStruct((m, n_out), out_dtype),
        compiler_params=_params("parallel", "arbitrary"),
        name=name,
    )(*args)


def _conv_silu(raw, carry_ref, w_ref, b_ref):
    l = raw.shape[0]
    carry = carry_ref[...]
    w = w_ref[...]
    acc = b_ref[...] + raw * w[SSD_CONV_WIDTH - 1:SSD_CONV_WIDTH, :]
    row8 = lax.broadcasted_iota(jnp.int32, (SUBLANES, raw.shape[1]), 0)
    for j in range(1, SSD_CONV_WIDTH):
        rolled = pltpu.roll(raw, j, 0)
        head = jnp.where(row8 < j, pltpu.roll(carry, j, 0), rolled[:SUBLANES])
        shifted = jnp.concatenate([head, rolled[SUBLANES:]], axis=0)
        acc = acc + shifted * w[SSD_CONV_WIDTH - 1 - j:SSD_CONV_WIDTH - j, :]
    carry_ref[...] = raw[l - SUBLANES:, :]
    return _silu(acc)


def _ssd_kernel(z_ref, x_ref, b_ref, c_ref, dt_ref, dtt_ref,
                cwx_ref, cbx_ref, cwb_ref, cbb_ref, cwc_ref, cbc_ref,
                bias_ref, biast_ref, alog_ref, alogt_ref, dexp_ref, nw_ref, e_ref,
                o_ref, state_ref, carx_ref, carb_ref, carc_ref, *, n_heads, head_dim):
    chunk = pl.program_id(1)

    @pl.when(chunk == 0)
    def _():
        state_ref[...] = jnp.zeros_like(state_ref)
        carx_ref[...] = jnp.zeros_like(carx_ref)
        carb_ref[...] = jnp.zeros_like(carb_ref)
        carc_ref[...] = jnp.zeros_like(carc_ref)

    l = x_ref.shape[0]
    gw = x_ref.shape[1]
    xs = _conv_silu(x_ref[...], carx_ref, cwx_ref, cbx_ref)
    bm = _conv_silu(b_ref[...], carb_ref, cwb_ref, cbb_ref)
    cm = _conv_silu(c_ref[...], carc_ref, cwc_ref, cbc_ref)

    dt = _softplus(dt_ref[0] + bias_ref[0])
    a = dt * (-jnp.exp(alog_ref[0]))
    dtt = _softplus(dtt_ref[0] + biast_ref[0])
    at = dtt * (-jnp.exp(alogt_ref[0]))

    row = lax.broadcasted_iota(jnp.int32, (l, l), 0)
    col = lax.broadcasted_iota(jnp.int32, (l, l), 1)
    tril = row >= col
    tril_b = tril.astype(BF16)
    triu_b = (row <= col).astype(BF16)

    a_cs = sum(jnp.dot(tril_b, p, preferred_element_type=F32) for p in _split3(a))
    a_cst = sum(jnp.dot(p, triu_b, preferred_element_type=F32) for p in _split3(at))

    a_last = a_cs[l - 1:l, :]
    dte = jnp.exp(a_last - a_cs)
    e3 = jnp.exp(a_cs)

    stacked = jnp.concatenate([dt, dte, e3], axis=0)
    hi, mid, lo = _split3(stacked)
    grp = lax.broadcasted_iota(jnp.int32, stacked.shape, 1) // n_heads
    pieces = jnp.where(grp == 0, hi, jnp.where(grp == 1, mid, lo))
    expanded = jnp.dot(pieces, e_ref[...], preferred_element_type=F32)
    dt_exp = expanded[:l]
    dte_exp = expanded[l:2 * l]
    e3_exp = expanded[2 * l:]

    xt = xs * dt_exp
    xt_b = xt.astype(BF16)
    cb = lax.dot_general(cm.astype(BF16), bm.astype(BF16), (((1,), (1,)), ((), ())),
                         preferred_element_type=F32)

    heads_per_vreg = LANES // head_dim
    lane = lax.broadcasted_iota(jnp.int32, (l, LANES), 1)
    y_cols = []
    for v in range(gw // LANES):
        x_pair = xt_b[:, v * LANES:(v + 1) * LANES]
        acc = jnp.zeros((l, LANES), F32)
        for s in range(heads_per_vreg):
            h = v * heads_per_vreg + s
            seg = a_cs[:, h:h + 1] - a_cst[h:h + 1, :]
            decay = jnp.exp(jnp.where(tril, seg, -jnp.inf))
            m_h = (cb * decay).astype(BF16)
            in_head = (lane >= s * head_dim) & (lane < (s + 1) * head_dim)
            x_h = jnp.where(in_head, x_pair, jnp.zeros_like(x_pair))
            acc = acc + jnp.dot(m_h, x_h, preferred_element_type=F32)
        y_cols.append(acc)
    y_diag = jnp.concatenate(y_cols, axis=1) if len(y_cols) > 1 else y_cols[0]

    state = state_ref[...]
    y_off = jnp.dot(cm.astype(BF16), state.astype(BF16), preferred_element_type=F32) * e3_exp
    xw_b = (xt * dte_exp).astype(BF16)
    new_states = jnp.dot(bm.T.astype(BF16), xw_b, preferred_element_type=F32)
    state_ref[...] = state * e3_exp[l - 1:l, :] + new_states

    y = y_diag + y_off + xs * dexp_ref[...]
    yz = y * _silu(z_ref[...])
    ms = jnp.mean(yz * yz, axis=-1, keepdims=True)
    o_ref[...] = (yz * lax.rsqrt(ms + RMS_EPS) * nw_ref[...]).astype(o_ref.dtype)


def ssd_core(zxbc, dt_raw, conv_w, conv_b, dt_bias, a_log, d_skip, norm_w):
    s = zxbc.shape[0]
    g, n, p, l = SSD_N_GROUPS, SSD_D_STATE, SSD_HEAD_DIM, SSD_CHUNK
    h = dt_raw.shape[1]
    r = h // g
    gw = r * p
    di = h * p
    l = min(l, s)
    assert gw % LANES == 0 and n % LANES == 0 and di == g * gw
    nx = di // gw
    nb0 = 2 * di // n
    nc0 = nb0 + g

    dtg = dt_raw.reshape(s, g, r).transpose(1, 0, 2)
    dtg3 = jnp.concatenate([dtg, dtg, dtg], axis=-1)
    dtt = dt_raw.reshape(s, g, r).transpose(1, 2, 0)
    bias = dt_bias.reshape(g, 1, r)
    bias3 = jnp.concatenate([bias, bias, bias], axis=-1)
    alog = a_log.reshape(g, 1, r)
    alog3 = jnp.concatenate([alog, alog, alog], axis=-1)
    biast = dt_bias.reshape(g, r, 1)
    alogt = a_log.reshape(g, r, 1)
    dexp = jnp.repeat(d_skip, p).reshape(1, di)
    nw = norm_w.reshape(1, di)
    cw = conv_w
    cb = conv_b.reshape(1, -1)
    head_of_col = jnp.arange(gw) // p
    e1 = (head_of_col[None, :] == jnp.arange(r)[:, None]).astype(BF16)
    e3 = jnp.concatenate([e1, e1, e1], axis=0)

    kw = SSD_CONV_WIDTH
    in_specs = [
        pl.BlockSpec((l, gw), lambda gi, c: (c, gi)),
        pl.BlockSpec((l, gw), lambda gi, c: (c, nx + gi)),
        pl.BlockSpec((l, n), lambda gi, c: (c, nb0 + gi)),
        pl.BlockSpec((l, n), lambda gi, c: (c, nc0 + gi)),
        pl.BlockSpec((1, l, 3 * r), lambda gi, c: (gi, c, 0)),
        pl.BlockSpec((1, r, l), lambda gi, c: (gi, 0, c)),
        pl.BlockSpec((kw, gw), lambda gi, c: (0, gi)),
        pl.BlockSpec((1, gw), lambda gi, c: (0, gi)),
        pl.BlockSpec((kw, n), lambda gi, c: (0, nb0 - nx * (gw // n) + gi)),
        pl.BlockSpec((1, n), lambda gi, c: (0, nb0 - nx * (gw // n) + gi)),
        pl.BlockSpec((kw, n), lambda gi, c: (0, nc0 - nx * (gw // n) + gi)),
        pl.BlockSpec((1, n), lambda gi, c: (0, nc0 - nx * (gw // n) + gi)),
        pl.BlockSpec((1, 1, 3 * r), lambda gi, c: (gi, 0, 0)),
        pl.BlockSpec((1, r, 1), lambda gi, c: (gi, 0, 0)),
        pl.BlockSpec((1, 1, 3 * r), lambda gi, c: (gi, 0, 0)),
        pl.BlockSpec((1, r, 1), lambda gi, c: (gi, 0, 0)),
        pl.BlockSpec((1, gw), lambda gi, c: (0, gi)),
        pl.BlockSpec((1, gw), lambda gi, c: (0, gi)),
        pl.BlockSpec((3 * r, gw), lambda gi, c: (0, 0)),
    ]
    return pl.pallas_call(
        functools.partial(_ssd_kernel, n_heads=r, head_dim=p),
        grid=(g, s // l),
        in_specs=in_specs,
        out_specs=pl.BlockSpec((l, gw), lambda gi, c: (c, gi)),
        out_shape=jax.ShapeDtypeStruct((s, di), BF16),
        scratch_shapes=[pltpu.VMEM((n, gw), F32),
                        pltpu.VMEM((SUBLANES, gw), F32),
                        pltpu.VMEM((SUBLANES, n), F32),
                        pltpu.VMEM((SUBLANES, n), F32)],
        compiler_params=_params("parallel", "arbitrary"),
        name="ssd_core",
    )(zxbc, zxbc, zxbc, zxbc, dtg3, dtt, cw, cb, cw, cb, cw, cb,
      bias3, biast, alog3, alogt, dexp, nw, e3)


def ssd_mixer(h, norm_w_in, w_in, conv_w, conv_b, dt_bias, a_log, d_skip, norm_w, w_out):
    d = h.shape[1]
    n_heads = dt_bias.shape[0]
    di = n_heads * SSD_HEAD_DIM
    n_main = 2 * di + 2 * SSD_N_GROUPS * SSD_D_STATE
    u = rmsnorm(h, norm_w_in)
    zxbc = matmul(u, w_in, k=d, n_out=n_main, tm=2048, tn=512, name="ssd_in_proj")
    dt_raw = matmul(u, w_in, k=d, n_out=n_heads, tm=2048, tn=LANES, w_col0=n_main,
                    name="ssd_dt_proj")
    y = ssd_core(zxbc, dt_raw, conv_w, conv_b, dt_bias, a_log, d_skip, norm_w)
    for kb in range(di // d):
        h = matmul(y, w_out, k=d, n_out=d, tm=2048, tn=256, a_kblk=kb, w_kblk=kb,
                   residual=h, name="ssd_out_proj")
    return h


def _pool_diff_kernel(v_ref, halo_ref, o_ref, *, group_dim, chunk):
    i = pl.program_id(0)
    tm = v_ref.shape[0]
    pos = i * tm + lax.broadcasted_iota(jnp.int32, (tm, chunk), 0)
    for g, window in enumerate(POOL_WINDOWS):
        inv_count = 1.0 / jnp.minimum(pos + 1, window).astype(F32)
        for c0 in range(g * group_dim, (g + 1) * group_dim, chunk):
            v = v_ref[:, c0:c0 + chunk]
            halo = jnp.where(i > 0, halo_ref[:, c0:c0 + chunk], 0.0)
            sw = jnp.concatenate([halo, v], axis=0)
            shift = 1
            while shift < window:
                sw = sw + pltpu.roll(sw, shift, 0)
                shift *= 2
            d = sw[POOL_HALO:] * inv_count - v
            o_ref[:, c0:c0 + chunk] = d.astype(o_ref.dtype)


def pool_diff(proj, width, group_dim, *, tm=256):
    s = proj.shape[0]
    tm = min(tm, s)
    chunk = min(512, group_dim)
    halo_blocks = tm // POOL_HALO
    return pl.pallas_call(
        functools.partial(_pool_diff_kernel, group_dim=group_dim, chunk=chunk),
        grid=(s // tm,),
        in_specs=[pl.BlockSpec((tm, width), lambda i: (i, 0)),
                  pl.BlockSpec((POOL_HALO, width),
                               lambda i: (jnp.maximum(i * halo_blocks - 1, 0), 0))],
        out_specs=pl.BlockSpec((tm, width), lambda i: (i, 0)),
        out_shape=jax.ShapeDtypeStruct((s, width), BF16),
        compiler_params=_params("parallel"),
        name="pool_diff",
    )(proj, proj)


def _pool_group_kernel(d_ref, w_ref, b_ref, sc_ref, z_ref, o_ref):
    acc = jnp.dot(d_ref[...], w_ref[0].astype(BF16), preferred_element_type=F32)
    acc = (acc + b_ref[0]) * sc_ref[...]
    o_ref[...] = (acc * _silu(z_ref[...])).astype(o_ref.dtype)


def pool_core(proj, w_grp, b_grp, scale, *, tm=2048, tn=512):
    s = proj.shape[0]
    ng, gd, _ = w_grp.shape
    assert ng == len(POOL_WINDOWS) and max(POOL_WINDOWS) <= POOL_HALO
    width = ng * gd
    d = pool_diff(proj, width, gd)
    tm = min(tm, s)
    tn = min(tn, gd)
    nj = gd // tn
    in_specs = [
        pl.BlockSpec((tm, gd), lambda g, i, j: (i, g), pipeline_mode=pl.Buffered(1)),
        pl.BlockSpec((1, gd, tn), lambda g, i, j: (g, 0, j)),
        pl.BlockSpec((1, 1, tn), lambda g, i, j: (g, 0, j)),
        pl.BlockSpec((1, tn), lambda g, i, j: (0, g * nj + j)),
        pl.BlockSpec((tm, tn), lambda g, i, j: (i, (width + g * gd) // tn + j)),
    ]
    return pl.pallas_call(
        _pool_group_kernel,
        grid=(ng, s // tm, nj),
        in_specs=in_specs,
        out_specs=pl.BlockSpec((tm, tn), lambda g, i, j: (i, g * nj + j)),
        out_shape=jax.ShapeDtypeStruct((s, width), BF16),
        compiler_params=_params("parallel", "parallel", "arbitrary"),
        name="pool_group",
    )(d, w_grp, b_grp.reshape(ng, 1, gd), scale.reshape(1, width), proj)


def pool_mixer(h, norm_w_in, w_in, w_grp, b_grp, scale, w_out):
    d = h.shape[1]
    width = scale.shape[0]
    u = rmsnorm(h, norm_w_in)
    proj = matmul(u, w_in, k=d, n_out=2 * width, tm=2048, tn=512, name="pool_in_proj")
    y = pool_core(proj, w_grp, b_grp, scale)
    for kb in range(width // d):
        h = matmul(y, w_out, k=d, n_out=d, tm=2048, tn=256, a_kblk=kb, w_kblk=kb,
                   residual=h, name="pool_out_proj")
    return h


def _moba_prep_kernel(q_ref, k_ref, v_ref, qw_ref, kw_ref, qs_ref, ka_ref, va_ref, km_ref):
    i = pl.program_id(0)

    @pl.when(i == 0)
    def _():
        km_ref[...] = jnp.zeros_like(km_ref)

    dh = MOBA_HEAD_DIM
    rows = q_ref.shape[0]
    n_heads = q_ref.shape[1] // dh
    scale = 1.0 / math.sqrt(dh)
    qw = qw_ref[...] * scale
    kw = kw_ref[...]
    lane = lax.broadcasted_iota(jnp.int32, (rows, dh), 1)
    onehot = jnp.where(lane == i, 1.0, 0.0).astype(ka_ref.dtype)
    ones = jnp.ones((rows, dh), va_ref.dtype)
    means = []
    for h in range(n_heads):
        sl = slice(h * dh, (h + 1) * dh)
        lo = slice(2 * h * dh, (2 * h + 1) * dh)
        hi = slice((2 * h + 1) * dh, (2 * h + 2) * dh)
        q = q_ref[:, sl]
        qs = q * lax.rsqrt(jnp.mean(q * q, axis=-1, keepdims=True) + RMS_EPS) * qw
        qs_ref[:, sl] = qs.astype(qs_ref.dtype)
        k = k_ref[:, sl]
        kn = k * lax.rsqrt(jnp.mean(k * k, axis=-1, keepdims=True) + RMS_EPS) * kw
        ka_ref[:, lo] = kn.astype(ka_ref.dtype)
        ka_ref[:, hi] = onehot
        va_ref[:, lo] = v_ref[:, sl].astype(va_ref.dtype)
        va_ref[:, hi] = ones
        means.append(jnp.mean(kn, axis=0, keepdims=True))
    mean_row = jnp.concatenate(means, axis=1)
    blk = lax.broadcasted_iota(jnp.int32, km_ref.shape, 0)
    km_ref[...] = jnp.where(blk == i, mean_row, km_ref[...])


def moba_prep(proj, q_norm_w, k_norm_w):
    s = proj.shape[0]
    width = proj.shape[1] // 4
    blk = MOBA_BLOCK
    nb = s // blk
    assert nb <= MOBA_HEAD_DIM
    row_spec = lambda c: pl.BlockSpec((blk, width), lambda i: (i, c))
    return pl.pallas_call(
        _moba_prep_kernel,
        grid=(nb,),
        in_specs=[row_spec(0), row_spec(1), row_spec(2),
                  pl.BlockSpec((1, MOBA_HEAD_DIM), lambda i: (0, 0)),
                  pl.BlockSpec((1, MOBA_HEAD_DIM), lambda i: (0, 0))],
        out_specs=[pl.BlockSpec((blk, width), lambda i: (i, 0)),
                   pl.BlockSpec((blk, 2 * width), lambda i: (i, 0)),
                   pl.BlockSpec((blk, 2 * width), lambda i: (i, 0)),
                   pl.BlockSpec((nb, width), lambda i: (0, 0))],
        out_shape=[jax.ShapeDtypeStruct((s, width), BF16),
                   jax.ShapeDtypeStruct((s, 2 * width), BF16),
                   jax.ShapeDtypeStruct((s, 2 * width), BF16),
                   jax.ShapeDtypeStruct((nb, width), F32)],
        compiler_params=_params("arbitrary"),
        name="moba_prep",
    )(proj, proj, proj, q_norm_w.reshape(1, -1), k_norm_w.reshape(1, -1))


def _moba_attn_kernel(q_ref, k_ref, v_ref, km_ref, gate_ref, o_ref,
                      s_ref, sown_ref, mrun_ref, acc_ref, *, group):
    qi = pl.program_id(1)
    blk = MOBA_BLOCK
    dh = MOBA_HEAD_DIM
    tq = q_ref.shape[0]
    nb = km_ref.shape[0]
    nt = (((1,), (1,)), ((), ()))
    q = q_ref[...]

    gscore = lax.dot_general(km_ref[...].astype(BF16), q, nt, preferred_element_type=F32)
    brow = lax.broadcasted_iota(jnp.int32, (nb, tq), 0)
    browf = brow.astype(F32)
    gwork = jnp.where(brow < qi, gscore, -jnp.inf)
    sel = jnp.zeros((nb, tq), F32)
    for _ in range(min(MOBA_TOPK, nb)):
        top = jnp.max(gwork, axis=0, keepdims=True)
        is_top = (gwork == top) & (top > -jnp.inf)
        first = jnp.min(jnp.where(is_top, browf, float(nb)), axis=0, keepdims=True)
        pick = browf == first
        sel = jnp.where(pick, 1.0, sel)
        gwork = jnp.where(pick, -jnp.inf, gwork)
    bias_t = jnp.where(sel > 0.5, 0.0, MOBA_MASKED)
    if nb < dh:
        bias_t = jnp.concatenate([bias_t, jnp.zeros((dh - nb, tq), F32)], axis=0)
    qa = jnp.concatenate([q, bias_t.T.astype(BF16)], axis=1)

    own = pl.multiple_of(qi * blk, blk)
    s_own = lax.dot_general(q, k_ref[pl.ds(own, blk), pl.ds(0, dh)], nt,
                            preferred_element_type=F32)
    r = lax.broadcasted_iota(jnp.int32, (tq, blk), 0)
    c = lax.broadcasted_iota(jnp.int32, (tq, blk), 1)
    s_own = jnp.where(c <= r, s_own, MOBA_MASKED)
    sown_ref[...] = s_own
    m_lanes = s_own[:, :LANES]
    for c0 in range(LANES, blk, LANES):
        m_lanes = jnp.maximum(m_lanes, s_own[:, c0:c0 + LANES])
    mrun_ref[...] = m_lanes

    n_groups = (qi + group - 1) // group

    def score_group(g, carry):
        m_lanes = mrun_ref[...]
        for u in range(group):
            kj = g * group + u
            start = pl.multiple_of(kj * blk, blk)
            s = lax.dot_general(qa, k_ref[pl.ds(start, blk), :], nt, preferred_element_type=F32)
            s_ref[kj] = s
            for c0 in range(0, blk, LANES):
                m_lanes = jnp.maximum(m_lanes, s[:, c0:c0 + LANES])
        mrun_ref[...] = m_lanes
        return carry

    lax.fori_loop(0, n_groups, score_group, 0)

    m_row = jnp.max(mrun_ref[...], axis=1, keepdims=True)
    mrun_ref[...] = jnp.broadcast_to(m_row, mrun_ref.shape)

    def probs(s):
        m_b = mrun_ref[...]
        return jnp.concatenate([jnp.exp(s[:, c0:c0 + LANES] - m_b) for c0 in range(0, blk, LANES)],
                               axis=1).astype(BF16)

    acc_ref[...] = jnp.dot(probs(sown_ref[...]), v_ref[pl.ds(own, blk), :],
                           preferred_element_type=F32)

    def value_group(g, carry):
        tot = jnp.zeros(acc_ref.shape, F32)
        for u in range(group):
            kj = g * group + u
            start = pl.multiple_of(kj * blk, blk)
            tot = tot + jnp.dot(probs(s_ref[kj]), v_ref[pl.ds(start, blk), :],
                                preferred_element_type=F32)
        acc_ref[...] += tot
        return carry

    lax.fori_loop(0, n_groups, value_group, 0)

    acc = acc_ref[...]
    out = acc[:, :dh] / acc[:, dh:]
    o_ref[...] = (out * _silu(gate_ref[...])).astype(o_ref.dtype)


def moba_attn(qs, ka, va, kmean, proj):
    s, width = qs.shape
    dh = MOBA_HEAD_DIM
    n_heads = width // dh
    blk = MOBA_BLOCK
    tq = blk
    nb = kmean.shape[0]
    group = math.gcd(nb, MOBA_GROUP)
    gate0 = 3 * n_heads
    return pl.pallas_call(
        functools.partial(_moba_attn_kernel, group=group),
        grid=(n_heads, s // tq),
        in_specs=[pl.BlockSpec((tq, dh), lambda h, i: (i, h)),
                  pl.BlockSpec((s, 2 * dh), lambda h, i: (0, h)),
                  pl.BlockSpec((s, 2 * dh), lambda h, i: (0, h)),
                  pl.BlockSpec((nb, dh), lambda h, i: (0, h)),
                  pl.BlockSpec((tq, dh), lambda h, i: (i, gate0 + h))],
        out_specs=pl.BlockSpec((tq, dh), lambda h, i: (i, h)),
        out_shape=jax.ShapeDtypeStruct((s, width), BF16),
        scratch_shapes=[pltpu.VMEM((nb, tq, blk), F32),
                        pltpu.VMEM((tq, blk), F32),
                        pltpu.VMEM((tq, LANES), F32),
                        pltpu.VMEM((tq, 2 * dh), F32)],
        compiler_params=_params("parallel", "arbitrary"),
        name="moba_attn",
    )(qs, ka, va, kmean, proj)


def moba_mixer(h, norm_w_in, w_in, q_norm_w, k_norm_w, w_out):
    s, d = h.shape
    assert s % MOBA_BLOCK == 0
    width = w_out.shape[0]
    u = rmsnorm(h, norm_w_in)
    proj = matmul(u, w_in, k=d, n_out=4 * width, tm=2048, tn=512, name="moba_in_proj")
    qs, ka, va, kmean = moba_prep(proj, q_norm_w, k_norm_w)
    y = moba_attn(qs, ka, va, kmean, proj)
    for kb in range(width // d):
        h = matmul(y, w_out, k=d, n_out=d, tm=2048, tn=256, a_kblk=kb, w_kblk=kb,
                   residual=h, name="moba_out_proj")
    return h


def kernel(x, norm0, ssd0_w_in, ssd0_conv_w, ssd0_conv_b, ssd0_dt_bias, ssd0_a_log, ssd0_d, ssd0_norm_w, ssd0_w_out, norm1, pool1_w_in, pool1_w_grp, pool1_b_grp, pool1_scale, pool1_w_out, norm2, moba2_w_in, moba2_q_norm, moba2_k_norm, moba2_w_out, norm3, ssd3_w_in, ssd3_conv_w, ssd3_conv_b, ssd3_dt_bias, ssd3_a_log, ssd3_d, ssd3_norm_w, ssd3_w_out):
    b, s, d = x.shape
    outs = []
    for bi in range(b):
        h = x[bi]
        h = ssd_mixer(h, norm0, ssd0_w_in, ssd0_conv_w, ssd0_conv_b, ssd0_dt_bias,
                      ssd0_a_log, ssd0_d, ssd0_norm_w, ssd0_w_out)
        h = pool_mixer(h, norm1, pool1_w_in, pool1_w_grp, pool1_b_grp, pool1_scale, pool1_w_out)
        h = moba_mixer(h, norm2, moba2_w_in, moba2_q_norm, moba2_k_norm, moba2_w_out)
        h = ssd_mixer(h, norm3, ssd3_w_in, ssd3_conv_w, ssd3_conv_b, ssd3_dt_bias,
                      ssd3_a_log, ssd3_d, ssd3_norm_w, ssd3_w_out)
        outs.append(h)
    return jnp.stack(outs, axis=0)
```

```python
import functools
import math

import jax
import jax.numpy as jnp
from jax import lax
from jax.experimental import pallas as pl
from jax.experimental.pallas import tpu as pltpu

RMS_EPS = 1e-6

SSD_HEAD_DIM = 64
SSD_N_GROUPS = 8
SSD_D_STATE = 128
SSD_CONV_WIDTH = 4
SSD_CHUNK = 128
SSD_STRIPE = 512

POOL_WINDOWS = (2, 4, 8, 16)
POOL_HALO = 16

MOBA_HEAD_DIM = 128
MOBA_BLOCK = 256
MOBA_TOPK = 3
MOBA_MASKED = -1e30

LANES = 128
SUBLANES = 8
VMEM_LIMIT_BYTES = 56 * 1024 * 1024

BF16 = jnp.bfloat16
F32 = jnp.float32


def _params(*semantics):
    return pltpu.CompilerParams(dimension_semantics=semantics,
                                vmem_limit_bytes=VMEM_LIMIT_BYTES)


def _silu(x):
    half = 0.5 * x
    return half + half * jnp.tanh(half)


def _softplus(x):
    return jnp.maximum(x, 0.0) + jnp.log(1.0 + jnp.exp(-jnp.abs(x)))


def _split3(x):
    hi = x.astype(BF16)
    r1 = x - hi.astype(F32)
    mid = r1.astype(BF16)
    lo = (r1 - mid.astype(F32)).astype(BF16)
    return hi, mid, lo


def _rmsnorm_kernel(x_ref, w_ref, o_ref):
    x = x_ref[...]
    ms = jnp.mean(x * x, axis=-1, keepdims=True)
    o_ref[...] = (x * lax.rsqrt(ms + RMS_EPS) * w_ref[...]).astype(o_ref.dtype)


def rmsnorm(x, w, *, tm=256):
    m, d = x.shape
    tm = min(tm, m)
    return pl.pallas_call(
        _rmsnorm_kernel,
        grid=(m // tm,),
        in_specs=[pl.BlockSpec((tm, d), lambda i: (i, 0)),
                  pl.BlockSpec((1, d), lambda i: (0, 0))],
        out_specs=pl.BlockSpec((tm, d), lambda i: (i, 0)),
        out_shape=jax.ShapeDtypeStruct((m, d), BF16),
        compiler_params=_params("parallel"),
        name="rmsnorm",
    )(x, w.reshape(1, d))


def _matmul_kernel(*refs, has_residual):
    if has_residual:
        a_ref, w_ref, r_ref, o_ref = refs
    else:
        a_ref, w_ref, o_ref = refs
    acc = jnp.dot(a_ref[...], w_ref[...].astype(BF16), preferred_element_type=F32)
    if has_residual:
        acc = acc + r_ref[...]
    o_ref[...] = acc.astype(o_ref.dtype)


def matmul(a, w, *, k, n_out, tm, tn, a_kblk=0, w_kblk=0, w_col0=0, residual=None,
           out_dtype=F32, name="matmul"):
    m = a.shape[0]
    tm = min(tm, m)
    tn = min(tn, n_out)
    assert m % tm == 0 and n_out % tn == 0 and w_col0 % tn == 0
    col0_blk = w_col0 // tn
    in_specs = [pl.BlockSpec((tm, k), lambda i, j: (i, a_kblk), pipeline_mode=pl.Buffered(1)),
                pl.BlockSpec((k, tn), lambda i, j: (w_kblk, j + col0_blk))]
    args = [a, w]
    if residual is not None:
        in_specs.append(pl.BlockSpec((tm, tn), lambda i, j: (i, j)))
        args.append(residual)
    return pl.pallas_call(
        functools.partial(_matmul_kernel, has_residual=residual is not None),
        grid=(m // tm, n_out // tn),
        in_specs=in_specs,
        out_specs=pl.BlockSpec((tm, tn), lambda i, j: (i, j)),
        out_shape=jax.ShapeDtypeStruct((m, n_out), out_dtype),
        compiler_params=_params("parallel", "arbitrary"),
        name=name,
    )(*args)


def _conv_silu(raw_ref, hist_ref, w_ref, b_ref, cols=slice(None)):
    l = raw_ref.shape[0]
    raw = raw_ref[:, cols]
    hist_ref[SUBLANES:, cols] = raw
    w = w_ref[:, cols]
    acc = b_ref[:, cols] + raw * w[SSD_CONV_WIDTH - 1:SSD_CONV_WIDTH, :]
    for j in range(1, SSD_CONV_WIDTH):
        acc = acc + hist_ref[pl.ds(SUBLANES - j, l), cols] * w[SSD_CONV_WIDTH - 1 - j:SSD_CONV_WIDTH - j, :]
    hist_ref[:SUBLANES, cols] = raw[l - SUBLANES:, :]
    return _silu(acc)


def _ssd_kernel(z_ref, x_ref, b_ref, c_ref, dt_ref, dtt_ref,
                cwx_ref, cbx_ref, cwb_ref, cbb_ref, cwc_ref, cbc_ref,
                bias_ref, biast_ref, alog_ref, alogt_ref, dexp_ref, nw_ref, e_ref,
                o_ref, state_ref, carx_ref, carb_ref, carc_ref, yz_ref, *, n_heads, head_dim):
    chunk = pl.program_id(1)

    @pl.when(chunk == 0)
    def _():
        state_ref[...] = jnp.zeros_like(state_ref)
        for hist_ref in (carx_ref, carb_ref, carc_ref):
            hist_ref[:SUBLANES, :] = jnp.zeros((SUBLANES, hist_ref.shape[1]), F32)

    l = x_ref.shape[0]
    gw = x_ref.shape[1]
    bm = _conv_silu(b_ref, carb_ref, cwb_ref, cbb_ref)
    cm = _conv_silu(c_ref, carc_ref, cwc_ref, cbc_ref)

    log2e = math.log2(math.e)
    dt = _softplus(dt_ref[0] + bias_ref[0])
    a = dt * (-log2e * jnp.exp(alog_ref[0]))
    dtt = _softplus(dtt_ref[0] + biast_ref[0])
    at = dtt * (-log2e * jnp.exp(alogt_ref[0]))

    row = lax.broadcasted_iota(jnp.int32, (l, l), 0)
    col = lax.broadcasted_iota(jnp.int32, (l, l), 1)
    tril = row >= col
    tril_b = tril.astype(BF16)
    triu_b = (row <= col).astype(BF16)

    a_cs = sum(jnp.dot(tril_b, p, preferred_element_type=F32) for p in _split3(a))
    a_cst = sum(jnp.dot(p, triu_b, preferred_element_type=F32) for p in _split3(at))

    a_last = a_cs[l - 1:l, :]
    dte = jnp.exp2(a_last - a_cs)
    e3 = jnp.exp2(a_cs)

    stacked = jnp.concatenate([dt, dte, e3], axis=0)
    hi, mid, lo = _split3(stacked)
    grp = lax.broadcasted_iota(jnp.int32, stacked.shape, 1) // n_heads
    pieces = jnp.where(grp == 0, hi, jnp.where(grp == 1, mid, lo))

    cm_b = cm.astype(BF16)
    bmt_b = bm.T.astype(BF16)
    cb = lax.dot_general(cm_b, bm.astype(BF16), (((1,), (1,)), ((), ())),
                         preferred_element_type=F32)
    cb = jnp.where(tril, cb, 0.0)

    heads_per_vreg = LANES // head_dim
    lane = lax.broadcasted_iota(jnp.int32, (l, LANES), 1)
    stripe = min(SSD_STRIPE, gw)
    ssq = jnp.zeros((l, 1), F32)
    for c0 in range(0, gw, stripe):
        cols = slice(c0, c0 + stripe)
        xs = _conv_silu(x_ref, carx_ref, cwx_ref, cbx_ref, cols)
        expanded = jnp.dot(pieces, e_ref[:, cols], preferred_element_type=F32)
        dt_exp = expanded[:l]
        dte_exp = expanded[l:2 * l]
        e3_exp = expanded[2 * l:]
        xt = xs * dt_exp
        xt_b = xt.astype(BF16)

        y_cols = []
        for v in range(stripe // LANES):
            x_pair = xt_b[:, v * LANES:(v + 1) * LANES]
            acc = jnp.zeros((l, LANES), F32)
            for s in range(heads_per_vreg):
                h = (c0 // LANES + v) * heads_per_vreg + s
                seg = jnp.minimum(a_cs[:, h:h + 1] - a_cst[h:h + 1, :], 0.0)
                m_h = (cb * jnp.exp2(seg)).astype(BF16)
                in_head = (lane >= s * head_dim) & (lane < (s + 1) * head_dim)
                x_h = jnp.where(in_head, x_pair, jnp.zeros_like(x_pair))
                acc = acc + jnp.dot(m_h, x_h, preferred_element_type=F32)
            y_cols.append(acc)
        y_diag = jnp.concatenate(y_cols, axis=1) if len(y_cols) > 1 else y_cols[0]

        state = state_ref[:, cols]
        y_off = jnp.dot(cm_b, state.astype(BF16), preferred_element_type=F32) * e3_exp
        xw_b = (xt * dte_exp).astype(BF16)
        new_states = jnp.dot(bmt_b, xw_b, preferred_element_type=F32)
        state_ref[:, cols] = state * e3_exp[l - 1:l, :] + new_states

        y = y_diag + y_off + xs * dexp_ref[:, cols]
        yz = y * _silu(z_ref[:, cols])
        ssq = ssq + jnp.sum(yz * yz, axis=-1, keepdims=True)
        yz_ref[:, cols] = yz

    inv_rms = lax.rsqrt(ssq * (1.0 / gw) + RMS_EPS)
    o_ref[...] = (yz_ref[...] * inv_rms * nw_ref[...]).astype(o_ref.dtype)


def ssd_core(zxbc, dt_raw, conv_w, conv_b, dt_bias, a_log, d_skip, norm_w):
    s = zxbc.shape[0]
    g, n, p, l = SSD_N_GROUPS, SSD_D_STATE, SSD_HEAD_DIM, SSD_CHUNK
    h = dt_raw.shape[1]
    r = h // g
    gw = r * p
    di = h * p
    l = min(l, s)
    assert gw % LANES == 0 and n % LANES == 0 and di == g * gw
    nx = di // gw
    nb0 = 2 * di // n
    nc0 = nb0 + g

    dtg = dt_raw.reshape(s, g, r).transpose(1, 0, 2)
    dtg3 = jnp.concatenate([dtg, dtg, dtg], axis=-1)
    dtt = dt_raw.reshape(s, g, r).transpose(1, 2, 0)
    bias = dt_bias.reshape(g, 1, r)
    bias3 = jnp.concatenate([bias, bias, bias], axis=-1)
    alog = a_log.reshape(g, 1, r)
    alog3 = jnp.concatenate([alog, alog, alog], axis=-1)
    biast = dt_bias.reshape(g, r, 1)
    alogt = a_log.reshape(g, r, 1)
    dexp = jnp.repeat(d_skip, p).reshape(1, di)
    nw = norm_w.reshape(1, di)
    cw = conv_w
    cb = conv_b.reshape(1, -1)
    head_of_col = jnp.arange(gw) // p
    e1 = (head_of_col[None, :] == jnp.arange(r)[:, None]).astype(BF16)
    e3 = jnp.concatenate([e1, e1, e1], axis=0)

    kw = SSD_CONV_WIDTH
    in_specs = [
        pl.BlockSpec((l, gw), lambda gi, c: (c, gi)),
        pl.BlockSpec((l, gw), lambda gi, c: (c, nx + gi)),
        pl.BlockSpec((l, n), lambda gi, c: (c, nb0 + gi)),
        pl.BlockSpec((l, n), lambda gi, c: (c, nc0 + gi)),
        pl.BlockSpec((1, l, 3 * r), lambda gi, c: (gi, c, 0)),
        pl.BlockSpec((1, r, l), lambda gi, c: (gi, 0, c)),
        pl.BlockSpec((kw, gw), lambda gi, c: (0, gi)),
        pl.BlockSpec((1, gw), lambda gi, c: (0, gi)),
        pl.BlockSpec((kw, n), lambda gi, c: (0, nb0 - nx * (gw // n) + gi)),
        pl.BlockSpec((1, n), lambda gi, c: (0, nb0 - nx * (gw // n) + gi)),
        pl.BlockSpec((kw, n), lambda gi, c: (0, nc0 - nx * (gw // n) + gi)),
        pl.BlockSpec((1, n), lambda gi, c: (0, nc0 - nx * (gw // n) + gi)),
        pl.BlockSpec((1, 1, 3 * r), lambda gi, c: (gi, 0, 0)),
        pl.BlockSpec((1, r, 1), lambda gi, c: (gi, 0, 0)),
        pl.BlockSpec((1, 1, 3 * r), lambda gi, c: (gi, 0, 0)),
        pl.BlockSpec((1, r, 1), lambda gi, c: (gi, 0, 0)),
        pl.BlockSpec((1, gw), lambda gi, c: (0, gi)),
        pl.BlockSpec((1, gw), lambda gi, c: (0, gi)),
        pl.BlockSpec((3 * r, gw), lambda gi, c: (0, 0)),
    ]
    return pl.pallas_call(
        functools.partial(_ssd_kernel, n_heads=r, head_dim=p),
        grid=(g, s // l),
        in_specs=in_specs,
        out_specs=pl.BlockSpec((l, gw), lambda gi, c: (c, gi)),
        out_shape=jax.ShapeDtypeStruct((s, di), BF16),
        scratch_shapes=[pltpu.VMEM((n, gw), F32),
                        pltpu.VMEM((SUBLANES + l, gw), F32),
                        pltpu.VMEM((SUBLANES + l, n), F32),
                        pltpu.VMEM((SUBLANES + l, n), F32),
                        pltpu.VMEM((l, gw), F32)],
        compiler_params=_params("parallel", "arbitrary"),
        name="ssd_core",
    )(zxbc, zxbc, zxbc, zxbc, dtg3, dtt, cw, cb, cw, cb, cw, cb,
      bias3, biast, alog3, alogt, dexp, nw, e3)


def ssd_mixer(h, norm_w_in, w_in, conv_w, conv_b, dt_bias, a_log, d_skip, norm_w, w_out):
    d = h.shape[1]
    n_heads = dt_bias.shape[0]
    di = n_heads * SSD_HEAD_DIM
    n_main = 2 * di + 2 * SSD_N_GROUPS * SSD_D_STATE
    u = rmsnorm(h, norm_w_in)
    zxbc = matmul(u, w_in, k=d, n_out=n_main, tm=2048, tn=512, name="ssd_in_proj")
    dt_raw = matmul(u, w_in, k=d, n_out=n_heads, tm=2048, tn=LANES, w_col0=n_main,
                    name="ssd_dt_proj")
    y = ssd_core(zxbc, dt_raw, conv_w, conv_b, dt_bias, a_log, d_skip, norm_w)
    for kb in range(di // d):
        h = matmul(y, w_out, k=d, n_out=d, tm=2048, tn=256, a_kblk=kb, w_kblk=kb,
                   residual=h, name="ssd_out_proj")
    return h


def _pool_diff_kernel(v_ref, halo_ref, o_ref, *, group_dim, chunk):
    i = pl.program_id(0)
    tm = v_ref.shape[0]
    pos = i * tm + lax.broadcasted_iota(jnp.int32, (tm, chunk), 0)
    for g, window in enumerate(POOL_WINDOWS):
        inv_count = 1.0 / jnp.minimum(pos + 1, window).astype(F32)
        for c0 in range(g * group_dim, (g + 1) * group_dim, chunk):
            v = v_ref[:, c0:c0 + chunk]
            halo = jnp.where(i > 0, halo_ref[:, c0:c0 + chunk], 0.0)
            sw = jnp.concatenate([halo, v], axis=0)
            shift = 1
            while shift < window:
                sw = sw + pltpu.roll(sw, shift, 0)
                shift *= 2
            d = sw[POOL_HALO:] * inv_count - v
            o_ref[:, c0:c0 + chunk] = d.astype(o_ref.dtype)


def pool_diff(proj, width, group_dim, *, tm=256):
    s = proj.shape[0]
    tm = min(tm, s)
    chunk = min(512, group_dim)
    halo_blocks = tm // POOL_HALO
    return pl.pallas_call(
        functools.partial(_pool_diff_kernel, group_dim=group_dim, chunk=chunk),
        grid=(s // tm,),
        in_specs=[pl.BlockSpec((tm, width), lambda i: (i, 0)),
                  pl.BlockSpec((POOL_HALO, width),
                               lambda i: (jnp.maximum(i * halo_blocks - 1, 0), 0))],
        out_specs=pl.BlockSpec((tm, width), lambda i: (i, 0)),
        out_shape=jax.ShapeDtypeStruct((s, width), BF16),
        compiler_params=_params("parallel"),
        name="pool_diff",
    )(proj, proj)


def _pool_group_kernel(d_ref, w_ref, b_ref, sc_ref, z_ref, o_ref):
    acc = jnp.dot(d_ref[...], w_ref[0].astype(BF16), preferred_element_type=F32)
    acc = (acc + b_ref[0]) * sc_ref[...]
    o_ref[...] = (acc * _silu(z_ref[...])).astype(o_ref.dtype)


def pool_core(proj, w_grp, b_grp, scale, *, tm=2048, tn=512):
    s = proj.shape[0]
    ng, gd, _ = w_grp.shape
    assert ng == len(POOL_WINDOWS) and max(POOL_WINDOWS) <= POOL_HALO
    width = ng * gd
    d = pool_diff(proj, width, gd)
    tm = min(tm, s)
    tn = min(tn, gd)
    nj = gd // tn
    in_specs = [
        pl.BlockSpec((tm, gd), lambda g, i, j: (i, g), pipeline_mode=pl.Buffered(1)),
        pl.BlockSpec((1, gd, tn), lambda g, i, j: (g, 0, j)),
        pl.BlockSpec((1, 1, tn), lambda g, i, j: (g, 0, j)),
        pl.BlockSpec((1, tn), lambda g, i, j: (0, g * nj + j)),
        pl.BlockSpec((tm, tn), lambda g, i, j: (i, (width + g * gd) // tn + j)),
    ]
    return pl.pallas_call(
        _pool_group_kernel,
        grid=(ng, s // tm, nj),
        in_specs=in_specs,
        out_specs=pl.BlockSpec((tm, tn), lambda g, i, j: (i, g * nj + j)),
        out_shape=jax.ShapeDtypeStruct((s, width), BF16),
        compiler_params=_params("parallel", "parallel", "arbitrary"),
        name="pool_group",
    )(d, w_grp, b_grp.reshape(ng, 1, gd), scale.reshape(1, width), proj)


def pool_mixer(h, norm_w_in, w_in, w_grp, b_grp, scale, w_out):
    d = h.shape[1]
    width = scale.shape[0]
    u = rmsnorm(h, norm_w_in)
    proj = matmul(u, w_in, k=d, n_out=2 * width, tm=2048, tn=512, name="pool_in_proj")
    y = pool_core(proj, w_grp, b_grp, scale)
    for kb in range(width // d):
        h = matmul(y, w_out, k=d, n_out=d, tm=2048, tn=256, a_kblk=kb, w_kblk=kb,
                   residual=h, name="pool_out_proj")
    return h


def _moba_prep_kernel(q_ref, k_ref, v_ref, qw_ref, kw_ref, qs_ref, ka_ref, va_ref, km_ref):
    i = pl.program_id(0)

    @pl.when(i == 0)
    def _():
        km_ref[...] = jnp.zeros_like(km_ref)

    dh = MOBA_HEAD_DIM
    rows = q_ref.shape[0]
    n_heads = q_ref.shape[1] // dh
    scale = math.log2(math.e) / math.sqrt(dh)
    qw = qw_ref[...] * scale
    kw = kw_ref[...]
    lane = lax.broadcasted_iota(jnp.int32, (rows, dh), 1)
    onehot = jnp.where(lane == i, 1.0, 0.0).astype(ka_ref.dtype)
    ones = jnp.ones((rows, dh), va_ref.dtype)
    means = []
    for h in range(n_heads):
        sl = slice(h * dh, (h + 1) * dh)
        lo = slice(2 * h * dh, (2 * h + 1) * dh)
        hi = slice((2 * h + 1) * dh, (2 * h + 2) * dh)
        q = q_ref[:, sl]
        qs = q * lax.rsqrt(jnp.mean(q * q, axis=-1, keepdims=True) + RMS_EPS) * qw
        qs_ref[:, sl] = qs.astype(qs_ref.dtype)
        k = k_ref[:, sl]
        kn = k * lax.rsqrt(jnp.mean(k * k, axis=-1, keepdims=True) + RMS_EPS) * kw
        ka_ref[:, lo] = kn.astype(ka_ref.dtype)
        ka_ref[:, hi] = onehot
        va_ref[:, lo] = v_ref[:, sl].astype(va_ref.dtype)
        va_ref[:, hi] = ones
        means.append(jnp.mean(kn, axis=0, keepdims=True))
    mean_row = jnp.concatenate(means, axis=1)
    blk = lax.broadcasted_iota(jnp.int32, km_ref.shape, 0)
    km_ref[...] = jnp.where(blk == i, mean_row, km_ref[...])


def moba_prep(proj, q_norm_w, k_norm_w):
    s = proj.shape[0]
    width = proj.shape[1] // 4
    blk = MOBA_BLOCK
    nb = s // blk
    assert nb <= MOBA_HEAD_DIM
    row_spec = lambda c: pl.BlockSpec((blk, width), lambda i: (i, c))
    return pl.pallas_call(
        _moba_prep_kernel,
        grid=(nb,),
        in_specs=[row_spec(0), row_spec(1), row_spec(2),
                  pl.BlockSpec((1, MOBA_HEAD_DIM), lambda i: (0, 0)),
                  pl.BlockSpec((1, MOBA_HEAD_DIM), lambda i: (0, 0))],
        out_specs=[pl.BlockSpec((blk, width), lambda i: (i, 0)),
                   pl.BlockSpec((blk, 2 * width), lambda i: (i, 0)),
                   pl.BlockSpec((blk, 2 * width), lambda i: (i, 0)),
                   pl.BlockSpec((nb, width), lambda i: (0, 0))],
        out_shape=[jax.ShapeDtypeStruct((s, width), BF16),
                   jax.ShapeDtypeStruct((s, 2 * width), BF16),
                   jax.ShapeDtypeStruct((s, 2 * width), BF16),
                   jax.ShapeDtypeStruct((nb, width), F32)],
        compiler_params=_params("arbitrary"),
        name="moba_prep",
    )(proj, proj, proj, q_norm_w.reshape(1, -1), k_norm_w.reshape(1, -1))


def _moba_attn_kernel(qa_in_ref, qb_in_ref, k_ref, v_ref, km_ref, ga_ref, gb_ref, oa_ref, ob_ref,
                      s_ref, sstat_ref, sown_ref, qa_ref, mrun_ref):
    p = pl.program_id(1)
    blk = MOBA_BLOCK
    dh = MOBA_HEAD_DIM
    tq = qa_in_ref.shape[0]
    nb = km_ref.shape[0]
    n_dynamic = nb // 2 - 1
    nt = (((1,), (1,)), ((), ()))
    q_blocks = (p, nb - 1 - p)
    q2 = jnp.concatenate([qa_in_ref[...], qb_in_ref[...]], axis=0)

    gscore = lax.dot_general(km_ref[...].astype(BF16), q2, nt, preferred_element_type=F32)
    brow = lax.broadcasted_iota(jnp.int32, (nb, 2 * tq), 0)
    qcol = lax.broadcasted_iota(jnp.int32, (nb, 2 * tq), 1)
    browf = brow.astype(F32)
    own_blk = jnp.where(qcol < tq, q_blocks[0], q_blocks[1])
    gwork = jnp.where(brow < own_blk, gscore, -jnp.inf)
    sel = jnp.zeros((nb, 2 * tq), F32)
    for _ in range(min(MOBA_TOPK, nb)):
        top = jnp.max(gwork, axis=0, keepdims=True)
        is_top = (gwork == top) & (top > -jnp.inf)
        first = jnp.min(jnp.where(is_top, browf, float(nb)), axis=0, keepdims=True)
        pick = browf == first
        sel = jnp.where(pick, 1.0, sel)
        gwork = jnp.where(pick, -jnp.inf, gwork)
    bias_t = jnp.where(sel > 0.5, 0.0, MOBA_MASKED)
    if nb < dh:
        bias_t = jnp.concatenate([bias_t, jnp.zeros((dh - nb, 2 * tq), F32)], axis=0)
    q2a = jnp.concatenate([q2, bias_t.T.astype(BF16)], axis=1)

    def lane_max(s):
        m = s[:, :LANES]
        for c0 in range(LANES, blk, LANES):
            m = jnp.maximum(m, s[:, c0:c0 + LANES])
        return m

    r = lax.broadcasted_iota(jnp.int32, (tq, blk), 0)
    c = lax.broadcasted_iota(jnp.int32, (tq, blk), 1)
    own_start = []
    for t in range(2):
        qa_ref[t] = q2a[t * tq:(t + 1) * tq]
        start = pl.multiple_of(q_blocks[t] * blk, blk)
        own_start.append(start)
        s_own = lax.dot_general(q2[t * tq:(t + 1) * tq], k_ref[pl.ds(start, blk), pl.ds(0, dh)], nt,
                                preferred_element_type=F32)
        s_own = jnp.where(c <= r, s_own, MOBA_MASKED)
        sown_ref[t] = s_own
        mrun_ref[t] = lane_max(s_own)

    n_static = nb // 2
    sc_static = lax.dot_general(qa_ref[1], k_ref[pl.ds(0, n_static * blk), :], nt,
                                preferred_element_type=F32)
    sstat_ref[...] = sc_static
    m_a = mrun_ref[0]
    m_b = mrun_ref[1]
    for c0 in range(0, n_static * blk, LANES):
        m_b = jnp.maximum(m_b, sc_static[:, c0:c0 + LANES])

    def slot(u):
        is_a = u < p
        tile = jnp.where(is_a, 0, 1)
        kj = jnp.where(is_a, u, n_static + u - p)
        return is_a, tile, pl.multiple_of(kj * blk, blk)

    for u in range(n_dynamic):
        is_a, tile, start = slot(u)
        sc = lax.dot_general(qa_ref[tile], k_ref[pl.ds(start, blk), :], nt,
                             preferred_element_type=F32)
        s_ref[u] = sc
        sm = lane_max(sc)
        m_a = jnp.maximum(m_a, jnp.where(is_a, sm, MOBA_MASKED))
        m_b = jnp.maximum(m_b, jnp.where(is_a, MOBA_MASKED, sm))

    def probs(sc, m_lanes):
        return jnp.concatenate([jnp.exp2(sc[:, c0:c0 + LANES] - m_lanes)
                                for c0 in range(0, sc.shape[1], LANES)], axis=1).astype(BF16)

    m_a = jnp.broadcast_to(jnp.max(m_a, axis=1, keepdims=True), (tq, LANES))
    m_b = jnp.broadcast_to(jnp.max(m_b, axis=1, keepdims=True), (tq, LANES))
    mrun_ref[0] = m_a
    mrun_ref[1] = m_b

    acc = [jnp.dot(probs(sown_ref[t], mrun_ref[t]), v_ref[pl.ds(own_start[t], blk), :],
                   preferred_element_type=F32) for t in range(2)]
    acc[1] = acc[1] + jnp.dot(probs(sstat_ref[...], mrun_ref[1]), v_ref[pl.ds(0, n_static * blk), :],
                              preferred_element_type=F32)
    if n_dynamic > 0:
        p_cols, v_rows = [], []
        for u in range(n_dynamic):
            is_a, tile, start = slot(u)
            pr = probs(s_ref[u], mrun_ref[tile])
            zero = jnp.zeros_like(pr)
            p_cols.append(jnp.concatenate([jnp.where(is_a, pr, zero), jnp.where(is_a, zero, pr)],
                                          axis=0))
            v_rows.append(v_ref[pl.ds(start, blk), :])
        both = jnp.dot(jnp.concatenate(p_cols, axis=1), jnp.concatenate(v_rows, axis=0),
                       preferred_element_type=F32)
        acc = [acc[0] + both[:tq], acc[1] + both[tq:]]

    for t, (g_ref, o_ref) in enumerate(((ga_ref, oa_ref), (gb_ref, ob_ref))):
        out = acc[t][:, :dh] / acc[t][:, dh:]
        o_ref[...] = (out * _silu(g_ref[...])).astype(o_ref.dtype)


def moba_attn(qs, ka, va, kmean, proj):
    s, width = qs.shape
    dh = MOBA_HEAD_DIM
    n_heads = width // dh
    blk = MOBA_BLOCK
    tq = blk
    nb = kmean.shape[0]
    assert nb % 2 == 0
    half = nb // 2
    gate0 = 3 * n_heads
    out_a, out_b = pl.pallas_call(
        _moba_attn_kernel,
        grid=(n_heads, half),
        in_specs=[pl.BlockSpec((tq, dh), lambda h, p: (p, h)),
                  pl.BlockSpec((tq, dh), lambda h, p: (nb - 1 - p, h)),
                  pl.BlockSpec((s, 2 * dh), lambda h, p: (0, h)),
                  pl.BlockSpec((s, 2 * dh), lambda h, p: (0, h)),
                  pl.BlockSpec((nb, dh), lambda h, p: (0, h)),
                  pl.BlockSpec((tq, dh), lambda h, p: (p, gate0 + h)),
                  pl.BlockSpec((tq, dh), lambda h, p: (nb - 1 - p, gate0 + h))],
        out_specs=[pl.BlockSpec((tq, dh), lambda h, p: (p, h)),
                   pl.BlockSpec((tq, dh), lambda h, p: (half - 1 - p, h))],
        out_shape=[jax.ShapeDtypeStruct((s // 2, width), BF16),
                   jax.ShapeDtypeStruct((s // 2, width), BF16)],
        scratch_shapes=[pltpu.VMEM((max(half - 1, 1), tq, blk), F32),
                        pltpu.VMEM((tq, half * blk), F32),
                        pltpu.VMEM((2, tq, blk), F32),
                        pltpu.VMEM((2, tq, 2 * dh), BF16),
                        pltpu.VMEM((2, tq, LANES), F32)],
        compiler_params=_params("parallel", "arbitrary"),
        name="moba_attn",
    )(qs, qs, ka, va, kmean, proj, proj)
    return jnp.concatenate([out_a, out_b], axis=0)


def moba_mixer(h, norm_w_in, w_in, q_norm_w, k_norm_w, w_out):
    s, d = h.shape
    assert s % MOBA_BLOCK == 0
    width = w_out.shape[0]
    u = rmsnorm(h, norm_w_in)
    proj = matmul(u, w_in, k=d, n_out=4 * width, tm=2048, tn=512, name="moba_in_proj")
    qs, ka, va, kmean = moba_prep(proj, q_norm_w, k_norm_w)
    y = moba_attn(qs, ka, va, kmean, proj)
    for kb in range(width // d):
        h = matmul(y, w_out, k=d, n_out=d, tm=2048, tn=256, a_kblk=kb, w_kblk=kb,
                   residual=h, name="moba_out_proj")
    return h


def kernel(x, norm0, ssd0_w_in, ssd0_conv_w, ssd0_conv_b, ssd0_dt_bias, ssd0_a_log, ssd0_d, ssd0_norm_w, ssd0_w_out, norm1, pool1_w_in, pool1_w_grp, pool1_b_grp, pool1_scale, pool1_w_out, norm2, moba2_w_in, moba2_q_norm, moba2_k_norm, moba2_w_out, norm3, ssd3_w_in, ssd3_conv_w, ssd3_conv_b, ssd3_dt_bias, ssd3_a_log, ssd3_d, ssd3_norm_w, ssd3_w_out):
    b, s, d = x.shape
    outs = []
    for bi in range(b):
        h = x[bi]
        h = ssd_mixer(h, norm0, ssd0_w_in, ssd0_conv_w, ssd0_conv_b, ssd0_dt_bias,
                      ssd0_a_log, ssd0_d, ssd0_norm_w, ssd0_w_out)
        h = pool_mixer(h, norm1, pool1_w_in, pool1_w_grp, pool1_b_grp, pool1_scale, pool1_w_out)
        h = moba_mixer(h, norm2, moba2_w_in, moba2_q_norm, moba2_k_norm, moba2_w_out)
        h = ssd_mixer(h, norm3, ssd3_w_in, ssd3_conv_w, ssd3_conv_b, ssd3_dt_bias,
                      ssd3_a_log, ssd3_d, ssd3_norm_w, ssd3_w_out)
        outs.append(h)
    return jnp.stack(outs, axis=0)
```

```python
import functools
import math

import jax
import jax.numpy as jnp
from jax import lax
from jax.experimental import pallas as pl
from jax.experimental.pallas import tpu as pltpu

RMS_EPS = 1e-6

SSD_HEAD_DIM = 64
SSD_N_GROUPS = 8
SSD_D_STATE = 128
SSD_CONV_WIDTH = 4
SSD_CHUNK = 128
SSD_STRIPE = 512

POOL_WINDOWS = (2, 4, 8, 16)
POOL_HALO = 16

MOBA_HEAD_DIM = 128
MOBA_BLOCK = 256
MOBA_TOPK = 3
MOBA_MASKED = -1e30

LANES = 128
SUBLANES = 8
VMEM_LIMIT_BYTES = 56 * 1024 * 1024

BF16 = jnp.bfloat16
F32 = jnp.float32


def _params(*semantics):
    return pltpu.CompilerParams(dimension_semantics=semantics,
                                vmem_limit_bytes=VMEM_LIMIT_BYTES)


def _silu(x):
    half = 0.5 * x
    return half + half * jnp.tanh(half)


def _softplus(x):
    return jnp.maximum(x, 0.0) + jnp.log(1.0 + jnp.exp(-jnp.abs(x)))


def _split3(x):
    hi = x.astype(BF16)
    r1 = x - hi.astype(F32)
    mid = r1.astype(BF16)
    lo = (r1 - mid.astype(F32)).astype(BF16)
    return hi, mid, lo


def _rmsnorm_kernel(x_ref, w_ref, o_ref):
    x = x_ref[...]
    ms = jnp.mean(x * x, axis=-1, keepdims=True)
    o_ref[...] = (x * lax.rsqrt(ms + RMS_EPS) * w_ref[...]).astype(o_ref.dtype)


def rmsnorm(x, w, *, tm=256):
    m, d = x.shape
    tm = min(tm, m)
    return pl.pallas_call(
        _rmsnorm_kernel,
        grid=(m // tm,),
        in_specs=[pl.BlockSpec((tm, d), lambda i: (i, 0)),
                  pl.BlockSpec((1, d), lambda i: (0, 0))],
        out_specs=pl.BlockSpec((tm, d), lambda i: (i, 0)),
        out_shape=jax.ShapeDtypeStruct((m, d), BF16),
        compiler_params=_params("parallel"),
        name="rmsnorm",
    )(x, w.reshape(1, d))


def _matmul_kernel(*refs, has_residual):
    if has_residual:
        a_ref, w_ref, r_ref, o_ref = refs
    else:
        a_ref, w_ref, o_ref = refs
    acc = jnp.dot(a_ref[...], w_ref[...].astype(BF16), preferred_element_type=F32)
    if has_residual:
        acc = acc + r_ref[...]
    o_ref[...] = acc.astype(o_ref.dtype)


def matmul(a, w, *, k, n_out, tm, tn, a_kblk=0, w_kblk=0, w_col0=0, residual=None,
           out_dtype=F32, name="matmul"):
    m = a.shape[0]
    tm = min(tm, m)
    tn = min(tn, n_out)
    assert m % tm == 0 and n_out % tn == 0 and w_col0 % tn == 0
    col0_blk = w_col0 // tn
    in_specs = [pl.BlockSpec((tm, k), lambda i, j: (i, a_kblk), pipeline_mode=pl.Buffered(1)),
                pl.BlockSpec((k, tn), lambda i, j: (w_kblk, j + col0_blk))]
    args = [a, w]
    if residual is not None:
        in_specs.append(pl.BlockSpec((tm, tn), lambda i, j: (i, j)))
        args.append(residual)
    return pl.pallas_call(
        functools.partial(_matmul_kernel, has_residual=residual is not None),
        grid=(m // tm, n_out // tn),
        in_specs=in_specs,
        out_specs=pl.BlockSpec((tm, tn), lambda i, j: (i, j)),
        out_shape=jax.ShapeDtypeStruct((m, n_out), out_dtype),
        compiler_params=_params("parallel", "arbitrary"),
        name=name,
    )(*args)


def _xbc_proj_kernel(a_ref, w_ref, cw_ref, cb_ref, o_ref, hist_ref, *, row_block):
    i = pl.program_id(0)
    j = pl.program_id(1)

    @pl.when(i == 0)
    def _():
        hist_ref[j] = jnp.zeros(hist_ref.shape[1:], F32)

    w_b = w_ref[...].astype(BF16)
    cw = cw_ref[...]
    cb = cb_ref[...]
    prev = hist_ref[j]
    row8 = lax.broadcasted_iota(jnp.int32, prev.shape, 0)
    taps = SSD_CONV_WIDTH
    for r0 in range(0, a_ref.shape[0], row_block):
        raw = jnp.dot(a_ref[r0:r0 + row_block, :], w_b, preferred_element_type=F32)
        acc = cb + raw * cw[taps - 1:taps, :]
        for t in range(1, taps):
            rolled = pltpu.roll(raw, t, 0)
            head = jnp.where(row8 < t, pltpu.roll(prev, t, 0), rolled[:SUBLANES])
            shifted = jnp.concatenate([head, rolled[SUBLANES:]], axis=0)
            acc = acc + shifted * cw[taps - 1 - t:taps - t, :]
        o_ref[r0:r0 + row_block, :] = _silu(acc)
        prev = raw[row_block - SUBLANES:, :]
    hist_ref[j] = prev


def ssd_xbc_proj(a, w, conv_w, conv_b, *, k, w_col0, tm=2048, tn=512, row_block=128):
    m = a.shape[0]
    n_out = conv_w.shape[1]
    tm = min(tm, m)
    tn = min(tn, n_out)
    row_block = min(row_block, tm)
    assert m % tm == 0 and n_out % tn == 0 and w_col0 % tn == 0 and tm % row_block == 0
    col0_blk = w_col0 // tn
    return pl.pallas_call(
        functools.partial(_xbc_proj_kernel, row_block=row_block),
        grid=(m // tm, n_out // tn),
        in_specs=[pl.BlockSpec((tm, k), lambda i, j: (i, 0), pipeline_mode=pl.Buffered(1)),
                  pl.BlockSpec((k, tn), lambda i, j: (0, j + col0_blk)),
                  pl.BlockSpec((SSD_CONV_WIDTH, tn), lambda i, j: (0, j)),
                  pl.BlockSpec((1, tn), lambda i, j: (0, j))],
        out_specs=pl.BlockSpec((tm, tn), lambda i, j: (i, j)),
        out_shape=jax.ShapeDtypeStruct((m, n_out), F32),
        scratch_shapes=[pltpu.VMEM((n_out // tn, SUBLANES, tn), F32)],
        compiler_params=_params("arbitrary", "arbitrary"),
        name="ssd_xbc_proj",
    )(a, w, conv_w, conv_b.reshape(1, n_out))


def _ssd_kernel(z_ref, x_ref, b_ref, c_ref, dt_ref, dtt_ref,
                bias_ref, biast_ref, alog_ref, alogt_ref, dexp_ref, nw_ref, e_ref,
                o_ref, state_ref, yz_ref, *, n_heads, head_dim):
    chunk = pl.program_id(1)

    @pl.when(chunk == 0)
    def _():
        state_ref[...] = jnp.zeros_like(state_ref)

    l = x_ref.shape[0]
    gw = x_ref.shape[1]
    bm = b_ref[...]
    cm = c_ref[...]

    log2e = math.log2(math.e)
    dt = _softplus(dt_ref[0] + bias_ref[0])
    a = dt * (-log2e * jnp.exp(alog_ref[0]))
    dtt = _softplus(dtt_ref[0] + biast_ref[0])
    at = dtt * (-log2e * jnp.exp(alogt_ref[0]))

    row = lax.broadcasted_iota(jnp.int32, (l, l), 0)
    col = lax.broadcasted_iota(jnp.int32, (l, l), 1)
    tril = row >= col
    tril_b = tril.astype(BF16)
    triu_b = (row <= col).astype(BF16)

    a_cs = sum(jnp.dot(tril_b, p, preferred_element_type=F32) for p in _split3(a))
    a_cst = sum(jnp.dot(p, triu_b, preferred_element_type=F32) for p in _split3(at))

    a_last = a_cs[l - 1:l, :]
    dte = jnp.exp2(a_last - a_cs)
    e3 = jnp.exp2(a_cs)

    stacked = jnp.concatenate([dt, dte, e3], axis=0)
    hi, mid, lo = _split3(stacked)
    grp = lax.broadcasted_iota(jnp.int32, stacked.shape, 1) // n_heads
    pieces = jnp.where(grp == 0, hi, jnp.where(grp == 1, mid, lo))

    cm_b = cm.astype(BF16)
    bmt_b = bm.T.astype(BF16)
    cb = lax.dot_general(cm_b, bm.astype(BF16), (((1,), (1,)), ((), ())),
                         preferred_element_type=F32)
    cb = jnp.where(tril, cb, 0.0)

    heads_per_vreg = LANES // head_dim
    lane = lax.broadcasted_iota(jnp.int32, (l, LANES), 1)
    stripe = min(SSD_STRIPE, gw)
    ssq = jnp.zeros((l, 1), F32)
    for c0 in range(0, gw, stripe):
        cols = slice(c0, c0 + stripe)
        xs = x_ref[:, cols]
        expanded = jnp.dot(pieces, e_ref[:, cols], preferred_element_type=F32)
        dt_exp = expanded[:l]
        dte_exp = expanded[l:2 * l]
        e3_exp = expanded[2 * l:]
        xt = xs * dt_exp
        xt_b = xt.astype(BF16)

        y_cols = []
        for v in range(stripe // LANES):
            x_pair = xt_b[:, v * LANES:(v + 1) * LANES]
            acc = jnp.zeros((l, LANES), F32)
            for s in range(heads_per_vreg):
                h = (c0 // LANES + v) * heads_per_vreg + s
                seg = jnp.minimum(a_cs[:, h:h + 1] - a_cst[h:h + 1, :], 0.0)
                m_h = (cb * jnp.exp2(seg)).astype(BF16)
                in_head = (lane >= s * head_dim) & (lane < (s + 1) * head_dim)
                x_h = jnp.where(in_head, x_pair, jnp.zeros_like(x_pair))
                acc = acc + jnp.dot(m_h, x_h, preferred_element_type=F32)
            y_cols.append(acc)
        y_diag = jnp.concatenate(y_cols, axis=1) if len(y_cols) > 1 else y_cols[0]

        state = state_ref[:, cols]
        y_off = jnp.dot(cm_b, state.astype(BF16), preferred_element_type=F32) * e3_exp
        xw_b = (xt * dte_exp).astype(BF16)
        new_states = jnp.dot(bmt_b, xw_b, preferred_element_type=F32)
        state_ref[:, cols] = state * e3_exp[l - 1:l, :] + new_states

        y = y_diag + y_off + xs * dexp_ref[:, cols]
        yz = y * _silu(z_ref[:, cols])
        ssq = ssq + jnp.sum(yz * yz, axis=-1, keepdims=True)
        yz_ref[:, cols] = yz

    inv_rms = lax.rsqrt(ssq * (1.0 / gw) + RMS_EPS)
    o_ref[...] = (yz_ref[...] * inv_rms * nw_ref[...]).astype(o_ref.dtype)


def ssd_core(z, xbc, dt_raw, dt_bias, a_log, d_skip, norm_w):
    s = z.shape[0]
    g, n, p, l = SSD_N_GROUPS, SSD_D_STATE, SSD_HEAD_DIM, SSD_CHUNK
    h = dt_raw.shape[1]
    r = h // g
    gw = r * p
    di = h * p
    l = min(l, s)
    assert gw % LANES == 0 and n % LANES == 0 and di == g * gw
    nb0 = di // n
    nc0 = nb0 + g

    dtg = dt_raw.reshape(s, g, r).transpose(1, 0, 2)
    dtg3 = jnp.concatenate([dtg, dtg, dtg], axis=-1)
    dtt = dt_raw.reshape(s, g, r).transpose(1, 2, 0)
    bias = dt_bias.reshape(g, 1, r)
    bias3 = jnp.concatenate([bias, bias, bias], axis=-1)
    alog = a_log.reshape(g, 1, r)
    alog3 = jnp.concatenate([alog, alog, alog], axis=-1)
    biast = dt_bias.reshape(g, r, 1)
    alogt = a_log.reshape(g, r, 1)
    dexp = jnp.repeat(d_skip, p).reshape(1, di)
    nw = norm_w.reshape(1, di)
    head_of_col = jnp.arange(gw) // p
    e1 = (head_of_col[None, :] == jnp.arange(r)[:, None]).astype(BF16)
    e3 = jnp.concatenate([e1, e1, e1], axis=0)

    in_specs = [
        pl.BlockSpec((l, gw), lambda gi, c: (c, gi)),
        pl.BlockSpec((l, gw), lambda gi, c: (c, gi)),
        pl.BlockSpec((l, n), lambda gi, c: (c, nb0 + gi)),
        pl.BlockSpec((l, n), lambda gi, c: (c, nc0 + gi)),
        pl.BlockSpec((1, l, 3 * r), lambda gi, c: (gi, c, 0)),
        pl.BlockSpec((1, r, l), lambda gi, c: (gi, 0, c)),
        pl.BlockSpec((1, 1, 3 * r), lambda gi, c: (gi, 0, 0)),
        pl.BlockSpec((1, r, 1), lambda gi, c: (gi, 0, 0)),
        pl.BlockSpec((1, 1, 3 * r), lambda gi, c: (gi, 0, 0)),
        pl.BlockSpec((1, r, 1), lambda gi, c: (gi, 0, 0)),
        pl.BlockSpec((1, gw), lambda gi, c: (0, gi)),
        pl.BlockSpec((1, gw), lambda gi, c: (0, gi)),
        pl.BlockSpec((3 * r, gw), lambda gi, c: (0, 0)),
    ]
    return pl.pallas_call(
        functools.partial(_ssd_kernel, n_heads=r, head_dim=p),
        grid=(g, s // l),
        in_specs=in_specs,
        out_specs=pl.BlockSpec((l, gw), lambda gi, c: (c, gi)),
        out_shape=jax.ShapeDtypeStruct((s, di), BF16),
        scratch_shapes=[pltpu.VMEM((n, gw), F32),
                        pltpu.VMEM((l, gw), F32)],
        compiler_params=_params("parallel", "arbitrary"),
        name="ssd_core",
    )(z, xbc, xbc, xbc, dtg3, dtt, bias3, biast, alog3, alogt, dexp, nw, e3)


def ssd_mixer(h, norm_w_in, w_in, conv_w, conv_b, dt_bias, a_log, d_skip, norm_w, w_out):
    d = h.shape[1]
    n_heads = dt_bias.shape[0]
    di = n_heads * SSD_HEAD_DIM
    n_main = 2 * di + 2 * SSD_N_GROUPS * SSD_D_STATE
    u = rmsnorm(h, norm_w_in)
    z = matmul(u, w_in, k=d, n_out=di, tm=2048, tn=512, name="ssd_z_proj")
    xbc = ssd_xbc_proj(u, w_in, conv_w, conv_b, k=d, w_col0=di)
    dt_raw = matmul(u, w_in, k=d, n_out=n_heads, tm=2048, tn=LANES, w_col0=n_main,
                    name="ssd_dt_proj")
    y = ssd_core(z, xbc, dt_raw, dt_bias, a_log, d_skip, norm_w)
    for kb in range(di // d):
        h = matmul(y, w_out, k=d, n_out=d, tm=2048, tn=256, a_kblk=kb, w_kblk=kb,
                   residual=h, name="ssd_out_proj")
    return h


def _pool_diff_kernel(v_ref, halo_ref, o_ref, *, group_dim, chunk):
    i = pl.program_id(0)
    tm = v_ref.shape[0]
    pos = i * tm + lax.broadcasted_iota(jnp.int32, (tm, chunk), 0)
    for g, window in enumerate(POOL_WINDOWS):
        inv_count = 1.0 / jnp.minimum(pos + 1, window).astype(F32)
        for c0 in range(g * group_dim, (g + 1) * group_dim, chunk):
            v = v_ref[:, c0:c0 + chunk]
            halo = jnp.where(i > 0, halo_ref[:, c0:c0 + chunk], 0.0)
            sw = jnp.concatenate([halo, v], axis=0)
            shift = 1
            while shift < window:
                sw = sw + pltpu.roll(sw, shift, 0)
                shift *= 2
            d = sw[POOL_HALO:] * inv_count - v
            o_ref[:, c0:c0 + chunk] = d.astype(o_ref.dtype)


def pool_diff(proj, width, group_dim, *, tm=256):
    s = proj.shape[0]
    tm = min(tm, s)
    chunk = min(512, group_dim)
    halo_blocks = tm // POOL_HALO
    return pl.pallas_call(
        functools.partial(_pool_diff_kernel, group_dim=group_dim, chunk=chunk),
        grid=(s // tm,),
        in_specs=[pl.BlockSpec((tm, width), lambda i: (i, 0)),
                  pl.BlockSpec((POOL_HALO, width),
                               lambda i: (jnp.maximum(i * halo_blocks - 1, 0), 0))],
        out_specs=pl.BlockSpec((tm, width), lambda i: (i, 0)),
        out_shape=jax.ShapeDtypeStruct((s, width), BF16),
        compiler_params=_params("parallel"),
        name="pool_diff",
    )(proj, proj)


def _pool_group_kernel(d_ref, w_ref, b_ref, sc_ref, z_ref, o_ref):
    acc = jnp.dot(d_ref[...], w_ref[0].astype(BF16), preferred_element_type=F32)
    acc = (acc + b_ref[0]) * sc_ref[...]
    o_ref[...] = (acc * _silu(z_ref[...])).astype(o_ref.dtype)


def pool_core(proj, w_grp, b_grp, scale, *, tm=2048, tn=512):
    s = proj.shape[0]
    ng, gd, _ = w_grp.shape
    assert ng == len(POOL_WINDOWS) and max(POOL_WINDOWS) <= POOL_HALO
    width = ng * gd
    d = pool_diff(proj, width, gd)
    tm = min(tm, s)
    tn = min(tn, gd)
    nj = gd // tn
    in_specs = [
        pl.BlockSpec((tm, gd), lambda g, i, j: (i, g), pipeline_mode=pl.Buffered(1)),
        pl.BlockSpec((1, gd, tn), lambda g, i, j: (g, 0, j)),
        pl.BlockSpec((1, 1, tn), lambda g, i, j: (g, 0, j)),
        pl.BlockSpec((1, tn), lambda g, i, j: (0, g * nj + j)),
        pl.BlockSpec((tm, tn), lambda g, i, j: (i, (width + g * gd) // tn + j)),
    ]
    return pl.pallas_call(
        _pool_group_kernel,
        grid=(ng, s // tm, nj),
        in_specs=in_specs,
        out_specs=pl.BlockSpec((tm, tn), lambda g, i, j: (i, g * nj + j)),
        out_shape=jax.ShapeDtypeStruct((s, width), BF16),
        compiler_params=_params("parallel", "parallel", "arbitrary"),
        name="pool_group",
    )(d, w_grp, b_grp.reshape(ng, 1, gd), scale.reshape(1, width), proj)


def pool_mixer(h, norm_w_in, w_in, w_grp, b_grp, scale, w_out):
    d = h.shape[1]
    width = scale.shape[0]
    u = rmsnorm(h, norm_w_in)
    proj = matmul(u, w_in, k=d, n_out=2 * width, tm=2048, tn=512, name="pool_in_proj")
    y = pool_core(proj, w_grp, b_grp, scale)
    for kb in range(width // d):
        h = matmul(y, w_out, k=d, n_out=d, tm=2048, tn=256, a_kblk=kb, w_kblk=kb,
                   residual=h, name="pool_out_proj")
    return h


def _moba_prep_kernel(q_ref, k_ref, v_ref, qw_ref, kw_ref, qs_ref, ka_ref, va_ref, km_ref):
    i = pl.program_id(0)

    @pl.when(i == 0)
    def _():
        km_ref[...] = jnp.zeros_like(km_ref)

    dh = MOBA_HEAD_DIM
    rows = q_ref.shape[0]
    n_heads = q_ref.shape[1] // dh
    scale = math.log2(math.e) / math.sqrt(dh)
    qw = qw_ref[...] * scale
    kw = kw_ref[...]
    lane = lax.broadcasted_iota(jnp.int32, (rows, dh), 1)
    onehot = jnp.where(lane == i, 1.0, 0.0).astype(ka_ref.dtype)
    ones = jnp.ones((rows, dh), va_ref.dtype)
    means = []
    for h in range(n_heads):
        sl = slice(h * dh, (h + 1) * dh)
        lo = slice(2 * h * dh, (2 * h + 1) * dh)
        hi = slice((2 * h + 1) * dh, (2 * h + 2) * dh)
        q = q_ref[:, sl]
        qs = q * lax.rsqrt(jnp.mean(q * q, axis=-1, keepdims=True) + RMS_EPS) * qw
        qs_ref[:, sl] = qs.astype(qs_ref.dtype)
        k = k_ref[:, sl]
        kn = k * lax.rsqrt(jnp.mean(k * k, axis=-1, keepdims=True) + RMS_EPS) * kw
        ka_ref[:, lo] = kn.astype(ka_ref.dtype)
        ka_ref[:, hi] = onehot
        va_ref[:, lo] = v_ref[:, sl].astype(va_ref.dtype)
        va_ref[:, hi] = ones
        means.append(jnp.mean(kn, axis=0, keepdims=True))
    mean_row = jnp.concatenate(means, axis=1)
    blk = lax.broadcasted_iota(jnp.int32, km_ref.shape, 0)
    km_ref[...] = jnp.where(blk == i, mean_row, km_ref[...])


def moba_prep(proj, q_norm_w, k_norm_w):
    s = proj.shape[0]
    width = proj.shape[1] // 4
    blk = MOBA_BLOCK
    nb = s // blk
    assert nb <= MOBA_HEAD_DIM
    row_spec = lambda c: pl.BlockSpec((blk, width), lambda i: (i, c))
    return pl.pallas_call(
        _moba_prep_kernel,
        grid=(nb,),
        in_specs=[row_spec(0), row_spec(1), row_spec(2),
                  pl.BlockSpec((1, MOBA_HEAD_DIM), lambda i: (0, 0)),
                  pl.BlockSpec((1, MOBA_HEAD_DIM), lambda i: (0, 0))],
        out_specs=[pl.BlockSpec((blk, width), lambda i: (i, 0)),
                   pl.BlockSpec((blk, 2 * width), lambda i: (i, 0)),
                   pl.BlockSpec((blk, 2 * width), lambda i: (i, 0)),
                   pl.BlockSpec((nb, width), lambda i: (0, 0))],
        out_shape=[jax.ShapeDtypeStruct((s, width), BF16),
                   jax.ShapeDtypeStruct((s, 2 * width), BF16),
                   jax.ShapeDtypeStruct((s, 2 * width), BF16),
                   jax.ShapeDtypeStruct((nb, width), F32)],
        compiler_params=_params("arbitrary"),
        name="moba_prep",
    )(proj, proj, proj, q_norm_w.reshape(1, -1), k_norm_w.reshape(1, -1))


def _moba_attn_kernel(qa_in_ref, qb_in_ref, k_ref, v_ref, km_ref, ga_ref, gb_ref, oa_ref, ob_ref,
                      s_ref, sstat_ref, sown_ref, qa_ref, mrun_ref):
    p = pl.program_id(1)
    blk = MOBA_BLOCK
    dh = MOBA_HEAD_DIM
    tq = qa_in_ref.shape[0]
    nb = km_ref.shape[0]
    n_dynamic = nb // 2 - 1
    nt = (((1,), (1,)), ((), ()))
    q_blocks = (p, nb - 1 - p)
    q2 = jnp.concatenate([qa_in_ref[...], qb_in_ref[...]], axis=0)

    gscore = lax.dot_general(km_ref[...].astype(BF16), q2, nt, preferred_element_type=F32)
    brow = lax.broadcasted_iota(jnp.int32, (nb, 2 * tq), 0)
    qcol = lax.broadcasted_iota(jnp.int32, (nb, 2 * tq), 1)
    browf = brow.astype(F32)
    own_blk = jnp.where(qcol < tq, q_blocks[0], q_blocks[1])
    gwork = jnp.where(brow < own_blk, gscore, -jnp.inf)
    sel = jnp.zeros((nb, 2 * tq), F32)
    for _ in range(min(MOBA_TOPK, nb)):
        top = jnp.max(gwork, axis=0, keepdims=True)
        is_top = (gwork == top) & (top > -jnp.inf)
        first = jnp.min(jnp.where(is_top, browf, float(nb)), axis=0, keepdims=True)
        pick = browf == first
        sel = jnp.where(pick, 1.0, sel)
        gwork = jnp.where(pick, -jnp.inf, gwork)
    bias_t = jnp.where(sel > 0.5, 0.0, MOBA_MASKED)
    if nb < dh:
        bias_t = jnp.concatenate([bias_t, jnp.zeros((dh - nb, 2 * tq), F32)], axis=0)
    q2a = jnp.concatenate([q2, bias_t.T.astype(BF16)], axis=1)

    def lane_max(s):
        m = s[:, :LANES]
        for c0 in range(LANES, blk, LANES):
            m = jnp.maximum(m, s[:, c0:c0 + LANES])
        return m

    r = lax.broadcasted_iota(jnp.int32, (tq, blk), 0)
    c = lax.broadcasted_iota(jnp.int32, (tq, blk), 1)
    own_start = []
    for t in range(2):
        qa_ref[t] = q2a[t * tq:(t + 1) * tq]
        start = pl.multiple_of(q_blocks[t] * blk, blk)
        own_start.append(start)
        s_own = lax.dot_general(q2[t * tq:(t + 1) * tq], k_ref[pl.ds(start, blk), pl.ds(0, dh)], nt,
                                preferred_element_type=F32)
        s_own = jnp.where(c <= r, s_own, MOBA_MASKED)
        sown_ref[t] = s_own
        mrun_ref[t] = lane_max(s_own)

    n_static = nb // 2
    sc_static = lax.dot_general(qa_ref[1], k_ref[pl.ds(0, n_static * blk), :], nt,
                                preferred_element_type=F32)
    sstat_ref[...] = sc_static
    m_a = mrun_ref[0]
    m_b = mrun_ref[1]
    for c0 in range(0, n_static * blk, LANES):
        m_b = jnp.maximum(m_b, sc_static[:, c0:c0 + LANES])

    def slot(u):
        is_a = u < p
        tile = jnp.where(is_a, 0, 1)
        kj = jnp.where(is_a, u, n_static + u - p)
        return is_a, tile, pl.multiple_of(kj * blk, blk)

    for u in range(n_dynamic):
        is_a, tile, start = slot(u)
        sc = lax.dot_general(qa_ref[tile], k_ref[pl.ds(start, blk), :], nt,
                             preferred_element_type=F32)
        s_ref[u] = sc
        sm = lane_max(sc)
        m_a = jnp.maximum(m_a, jnp.where(is_a, sm, MOBA_MASKED))
        m_b = jnp.maximum(m_b, jnp.where(is_a, MOBA_MASKED, sm))

    def probs(sc, m_lanes):
        return jnp.concatenate([jnp.exp2(sc[:, c0:c0 + LANES] - m_lanes)
                                for c0 in range(0, sc.shape[1], LANES)], axis=1).astype(BF16)

    m_a = jnp.broadcast_to(jnp.max(m_a, axis=1, keepdims=True), (tq, LANES))
    m_b = jnp.broadcast_to(jnp.max(m_b, axis=1, keepdims=True), (tq, LANES))
    mrun_ref[0] = m_a
    mrun_ref[1] = m_b

    acc = [jnp.dot(probs(sown_ref[t], mrun_ref[t]), v_ref[pl.ds(own_start[t], blk), :],
                   preferred_element_type=F32) for t in range(2)]
    acc[1] = acc[1] + jnp.dot(probs(sstat_ref[...], mrun_ref[1]), v_ref[pl.ds(0, n_static * blk), :],
                              preferred_element_type=F32)
    for u in range(n_dynamic):
        is_a, tile, start = slot(u)
        part = jnp.dot(probs(s_ref[u], mrun_ref[tile]), v_ref[pl.ds(start, blk), :],
                       preferred_element_type=F32)
        zero = jnp.zeros_like(part)
        acc = [acc[0] + jnp.where(is_a, part, zero), acc[1] + jnp.where(is_a, zero, part)]

    for t, (g_ref, o_ref) in enumerate(((ga_ref, oa_ref), (gb_ref, ob_ref))):
        out = acc[t][:, :dh] / acc[t][:, dh:]
        o_ref[...] = (out * _silu(g_ref[...])).astype(o_ref.dtype)


def moba_attn(qs, ka, va, kmean, proj):
    s, width = qs.shape
    dh = MOBA_HEAD_DIM
    n_heads = width // dh
    blk = MOBA_BLOCK
    tq = blk
    nb = kmean.shape[0]
    assert nb % 2 == 0
    half = nb // 2
    gate0 = 3 * n_heads
    out_a, out_b = pl.pallas_call(
        _moba_attn_kernel,
        grid=(n_heads, half),
        in_specs=[pl.BlockSpec((tq, dh), lambda h, p: (p, h)),
                  pl.BlockSpec((tq, dh), lambda h, p: (nb - 1 - p, h)),
                  pl.BlockSpec((s, 2 * dh), lambda h, p: (0, h)),
                  pl.BlockSpec((s, 2 * dh), lambda h, p: (0, h)),
                  pl.BlockSpec((nb, dh), lambda h, p: (0, h)),
                  pl.BlockSpec((tq, dh), lambda h, p: (p, gate0 + h)),
                  pl.BlockSpec((tq, dh), lambda h, p: (nb - 1 - p, gate0 + h))],
        out_specs=[pl.BlockSpec((tq, dh), lambda h, p: (p, h)),
                   pl.BlockSpec((tq, dh), lambda h, p: (half - 1 - p, h))],
        out_shape=[jax.ShapeDtypeStruct((s // 2, width), BF16),
                   jax.ShapeDtypeStruct((s // 2, width), BF16)],
        scratch_shapes=[pltpu.VMEM((max(half - 1, 1), tq, blk), F32),
                        pltpu.VMEM((tq, half * blk), F32),
                        pltpu.VMEM((2, tq, blk), F32),
                        pltpu.VMEM((2, tq, 2 * dh), BF16),
                        pltpu.VMEM((2, tq, LANES), F32)],
        compiler_params=_params("parallel", "arbitrary"),
        name="moba_attn",
    )(qs, qs, ka, va, kmean, proj, proj)
    return jnp.concatenate([out_a, out_b], axis=0)


def moba_mixer(h, norm_w_in, w_in, q_norm_w, k_norm_w, w_out):
    s, d = h.shape
    assert s % MOBA_BLOCK == 0
    width = w_out.shape[0]
    u = rmsnorm(h, norm_w_in)
    proj = matmul(u, w_in, k=d, n_out=4 * width, tm=2048, tn=512, name="moba_in_proj")
    qs, ka, va, kmean = moba_prep(proj, q_norm_w, k_norm_w)
    y = moba_attn(qs, ka, va, kmean, proj)
    for kb in range(width // d):
        h = matmul(y, w_out, k=d, n_out=d, tm=2048, tn=256, a_kblk=kb, w_kblk=kb,
                   residual=h, name="moba_out_proj")
    return h


def kernel(x, norm0, ssd0_w_in, ssd0_conv_w, ssd0_conv_b, ssd0_dt_bias, ssd0_a_log, ssd0_d, ssd0_norm_w, ssd0_w_out, norm1, pool1_w_in, pool1_w_grp, pool1_b_grp, pool1_scale, pool1_w_out, norm2, moba2_w_in, moba2_q_norm, moba2_k_norm, moba2_w_out, norm3, ssd3_w_in, ssd3_conv_w, ssd3_conv_b, ssd3_dt_bias, ssd3_a_log, ssd3_d, ssd3_norm_w, ssd3_w_out):
    b, s, d = x.shape
    outs = []
    for bi in range(b):
        h = x[bi]
        h = ssd_mixer(h, norm0, ssd0_w_in, ssd0_conv_w, ssd0_conv_b, ssd0_dt_bias,
                      ssd0_a_log, ssd0_d, ssd0_norm_w, ssd0_w_out)
        h = pool_mixer(h, norm1, pool1_w_in, pool1_w_grp, pool1_b_grp, pool1_scale, pool1_w_out)
        h = moba_mixer(h, norm2, moba2_w_in, moba2_q_norm, moba2_k_norm, moba2_w_out)
        h = ssd_mixer(h, norm3, ssd3_w_in, ssd3_conv_w, ssd3_conv_b, ssd3_dt_bias,
                      ssd3_a_log, ssd3_d, ssd3_norm_w, ssd3_w_out)
        outs.append(h)
    return jnp.stack(outs, axis=0)
```

```python
import functools
import math

import jax
import jax.numpy as jnp
from jax import lax
from jax.experimental import pallas as pl
from jax.experimental.pallas import tpu as pltpu

RMS_EPS = 1e-6

SSD_HEAD_DIM = 64
SSD_N_GROUPS = 8
SSD_D_STATE = 128
SSD_CONV_WIDTH = 4
SSD_CHUNK = 128
SSD_STRIPE = 512

POOL_WINDOWS = (2, 4, 8, 16)
POOL_HALO = 16

MOBA_HEAD_DIM = 128
MOBA_BLOCK = 256
MOBA_TOPK = 3
MOBA_MASKED = -1e30
MOBA_PAIRS = 2
MOBA_STAGE_SLOTS = 3
MOBA_STAGE_LAG = 1

LANES = 128
SUBLANES = 8
VMEM_LIMIT_BYTES = 56 * 1024 * 1024

BF16 = jnp.bfloat16
F32 = jnp.float32


def _params(*semantics):
    return pltpu.CompilerParams(dimension_semantics=semantics,
                                vmem_limit_bytes=VMEM_LIMIT_BYTES)


def _silu(x):
    half = 0.5 * x
    return half + half * jnp.tanh(half)


def _softplus(x):
    return jnp.maximum(x, 0.0) + jnp.log(1.0 + jnp.exp(-jnp.abs(x)))


def _split3(x):
    hi = x.astype(BF16)
    r1 = x - hi.astype(F32)
    mid = r1.astype(BF16)
    lo = (r1 - mid.astype(F32)).astype(BF16)
    return hi, mid, lo


def _rmsnorm_kernel(x_ref, w_ref, o_ref):
    x = x_ref[...]
    ms = jnp.mean(x * x, axis=-1, keepdims=True)
    o_ref[...] = (x * lax.rsqrt(ms + RMS_EPS) * w_ref[...]).astype(o_ref.dtype)


def rmsnorm(x, w, *, tm=256):
    m, d = x.shape
    tm = min(tm, m)
    return pl.pallas_call(
        _rmsnorm_kernel,
        grid=(m // tm,),
        in_specs=[pl.BlockSpec((tm, d), lambda i: (i, 0)),
                  pl.BlockSpec((1, d), lambda i: (0, 0))],
        out_specs=pl.BlockSpec((tm, d), lambda i: (i, 0)),
        out_shape=jax.ShapeDtypeStruct((m, d), BF16),
        compiler_params=_params("parallel"),
        name="rmsnorm",
    )(x, w.reshape(1, d))


def _matmul_kernel(*refs, has_residual):
    if has_residual:
        a_ref, w_ref, r_ref, o_ref = refs
    else:
        a_ref, w_ref, o_ref = refs
    acc = jnp.dot(a_ref[...], w_ref[...].astype(BF16), preferred_element_type=F32)
    if has_residual:
        acc = acc + r_ref[...]
    o_ref[...] = acc.astype(o_ref.dtype)


def matmul(a, w, *, k, n_out, tm, tn, a_kblk=0, w_kblk=0, w_col0=0, residual=None,
           out_dtype=F32, name="matmul"):
    m = a.shape[0]
    tm = min(tm, m)
    tn = min(tn, n_out)
    assert m % tm == 0 and n_out % tn == 0 and w_col0 % tn == 0
    col0_blk = w_col0 // tn
    in_specs = [pl.BlockSpec((tm, k), lambda i, j: (i, a_kblk), pipeline_mode=pl.Buffered(1)),
                pl.BlockSpec((k, tn), lambda i, j: (w_kblk, j + col0_blk))]
    args = [a, w]
    if residual is not None:
        in_specs.append(pl.BlockSpec((tm, tn), lambda i, j: (i, j)))
        args.append(residual)
    return pl.pallas_call(
        functools.partial(_matmul_kernel, has_residual=residual is not None),
        grid=(m // tm, n_out // tn),
        in_specs=in_specs,
        out_specs=pl.BlockSpec((tm, tn), lambda i, j: (i, j)),
        out_shape=jax.ShapeDtypeStruct((m, n_out), out_dtype),
        compiler_params=_params("parallel", "arbitrary"),
        name=name,
    )(*args)


def _conv_silu(raw_ref, hist_ref, w_ref, b_ref, cols=slice(None)):
    l = raw_ref.shape[0]
    raw = raw_ref[:, cols]
    hist_ref[SUBLANES:, cols] = raw
    w = w_ref[:, cols]
    acc = b_ref[:, cols] + raw * w[SSD_CONV_WIDTH - 1:SSD_CONV_WIDTH, :]
    for j in range(1, SSD_CONV_WIDTH):
        acc = acc + hist_ref[pl.ds(SUBLANES - j, l), cols] * w[SSD_CONV_WIDTH - 1 - j:SSD_CONV_WIDTH - j, :]
    hist_ref[:SUBLANES, cols] = raw[l - SUBLANES:, :]
    return _silu(acc)


def _ssd_kernel(z_ref, x_ref, b_ref, c_ref, dt_ref, dtt_ref,
                cwx_ref, cbx_ref, cwb_ref, cbb_ref, cwc_ref, cbc_ref,
                bias_ref, biast_ref, alog_ref, alogt_ref, dexp_ref, nw_ref, e_ref,
                o_ref, state_ref, carx_ref, carb_ref, carc_ref, yz_ref, *, n_heads, head_dim):
    chunk = pl.program_id(1)

    @pl.when(chunk == 0)
    def _():
        state_ref[...] = jnp.zeros_like(state_ref)
        for hist_ref in (carx_ref, carb_ref, carc_ref):
            hist_ref[:SUBLANES, :] = jnp.zeros((SUBLANES, hist_ref.shape[1]), F32)

    l = x_ref.shape[0]
    gw = x_ref.shape[1]
    bm = _conv_silu(b_ref, carb_ref, cwb_ref, cbb_ref)
    cm = _conv_silu(c_ref, carc_ref, cwc_ref, cbc_ref)

    log2e = math.log2(math.e)
    dt = _softplus(dt_ref[0] + bias_ref[0])
    a = dt * (-log2e * jnp.exp(alog_ref[0]))
    dtt = _softplus(dtt_ref[0] + biast_ref[0])
    at = dtt * (-log2e * jnp.exp(alogt_ref[0]))

    row = lax.broadcasted_iota(jnp.int32, (l, l), 0)
    col = lax.broadcasted_iota(jnp.int32, (l, l), 1)
    tril = row >= col
    tril_b = tril.astype(BF16)
    triu_b = (row <= col).astype(BF16)

    a_cs = sum(jnp.dot(tril_b, p, preferred_element_type=F32) for p in _split3(a))
    a_cst = sum(jnp.dot(p, triu_b, preferred_element_type=F32) for p in _split3(at))

    a_last = a_cs[l - 1:l, :]
    dte = jnp.exp2(a_last - a_cs)
    e3 = jnp.exp2(a_cs)

    stacked = jnp.concatenate([dt, dte, e3], axis=0)
    hi, mid, lo = _split3(stacked)
    grp = lax.broadcasted_iota(jnp.int32, stacked.shape, 1) // n_heads
    pieces = jnp.where(grp == 0, hi, jnp.where(grp == 1, mid, lo))

    cm_b = cm.astype(BF16)
    bmt_b = bm.T.astype(BF16)
    cb = lax.dot_general(cm_b, bm.astype(BF16), (((1,), (1,)), ((), ())),
                         preferred_element_type=F32)
    cb = jnp.where(tril, cb, 0.0)

    heads_per_vreg = LANES // head_dim
    lane = lax.broadcasted_iota(jnp.int32, (l, LANES), 1)
    stripe = min(SSD_STRIPE, gw)
    ssq = jnp.zeros((l, 1), F32)
    for c0 in range(0, gw, stripe):
        cols = slice(c0, c0 + stripe)
        xs = _conv_silu(x_ref, carx_ref, cwx_ref, cbx_ref, cols)
        expanded = jnp.dot(pieces, e_ref[:, cols], preferred_element_type=F32)
        dt_exp = expanded[:l]
        dte_exp = expanded[l:2 * l]
        e3_exp = expanded[2 * l:]
        xt = xs * dt_exp
        xt_b = xt.astype(BF16)

        y_cols = []
        for v in range(stripe // LANES):
            x_pair = xt_b[:, v * LANES:(v + 1) * LANES]
            acc = jnp.zeros((l, LANES), F32)
            for s in range(heads_per_vreg):
                h = (c0 // LANES + v) * heads_per_vreg + s
                seg = jnp.minimum(a_cs[:, h:h + 1] - a_cst[h:h + 1, :], 0.0)
                m_h = (cb * jnp.exp2(seg)).astype(BF16)
                in_head = (lane >= s * head_dim) & (lane < (s + 1) * head_dim)
                x_h = jnp.where(in_head, x_pair, jnp.zeros_like(x_pair))
                acc = acc + jnp.dot(m_h, x_h, preferred_element_type=F32)
            y_cols.append(acc)
        y_diag = jnp.concatenate(y_cols, axis=1) if len(y_cols) > 1 else y_cols[0]

        state = state_ref[:, cols]
        y_off = jnp.dot(cm_b, state.astype(BF16), preferred_element_type=F32) * e3_exp
        xw_b = (xt * dte_exp).astype(BF16)
        new_states = jnp.dot(bmt_b, xw_b, preferred_element_type=F32)
        state_ref[:, cols] = state * e3_exp[l - 1:l, :] + new_states

        y = y_diag + y_off + xs * dexp_ref[:, cols]
        yz = y * _silu(z_ref[:, cols])
        ssq = ssq + jnp.sum(yz * yz, axis=-1, keepdims=True)
        yz_ref[:, cols] = yz

    inv_rms = lax.rsqrt(ssq * (1.0 / gw) + RMS_EPS)
    o_ref[...] = (yz_ref[...] * inv_rms * nw_ref[...]).astype(o_ref.dtype)


def ssd_core(zxbc, dt_raw, conv_w, conv_b, dt_bias, a_log, d_skip, norm_w):
    s = zxbc.shape[0]
    g, n, p, l = SSD_N_GROUPS, SSD_D_STATE, SSD_HEAD_DIM, SSD_CHUNK
    h = dt_raw.shape[1]
    r = h // g
    gw = r * p
    di = h * p
    l = min(l, s)
    assert gw % LANES == 0 and n % LANES == 0 and di == g * gw
    nx = di // gw
    nb0 = 2 * di // n
    nc0 = nb0 + g

    dtg = dt_raw.reshape(s, g, r).transpose(1, 0, 2)
    dtg3 = jnp.concatenate([dtg, dtg, dtg], axis=-1)
    dtt = dt_raw.reshape(s, g, r).transpose(1, 2, 0)
    bias = dt_bias.reshape(g, 1, r)
    bias3 = jnp.concatenate([bias, bias, bias], axis=-1)
    alog = a_log.reshape(g, 1, r)
    alog3 = jnp.concatenate([alog, alog, alog], axis=-1)
    biast = dt_bias.reshape(g, r, 1)
    alogt = a_log.reshape(g, r, 1)
    dexp = jnp.repeat(d_skip, p).reshape(1, di)
    nw = norm_w.reshape(1, di)
    cw = conv_w
    cb = conv_b.reshape(1, -1)
    head_of_col = jnp.arange(gw) // p
    e1 = (head_of_col[None, :] == jnp.arange(r)[:, None]).astype(BF16)
    e3 = jnp.concatenate([e1, e1, e1], axis=0)

    kw = SSD_CONV_WIDTH
    in_specs = [
        pl.BlockSpec((l, gw), lambda gi, c: (c, gi)),
        pl.BlockSpec((l, gw), lambda gi, c: (c, nx + gi)),
        pl.BlockSpec((l, n), lambda gi, c: (c, nb0 + gi)),
        pl.BlockSpec((l, n), lambda gi, c: (c, nc0 + gi)),
        pl.BlockSpec((1, l, 3 * r), lambda gi, c: (gi, c, 0)),
        pl.BlockSpec((1, r, l), lambda gi, c: (gi, 0, c)),
        pl.BlockSpec((kw, gw), lambda gi, c: (0, gi)),
        pl.BlockSpec((1, gw), lambda gi, c: (0, gi)),
        pl.BlockSpec((kw, n), lambda gi, c: (0, nb0 - nx * (gw // n) + gi)),
        pl.BlockSpec((1, n), lambda gi, c: (0, nb0 - nx * (gw // n) + gi)),
        pl.BlockSpec((kw, n), lambda gi, c: (0, nc0 - nx * (gw // n) + gi)),
        pl.BlockSpec((1, n), lambda gi, c: (0, nc0 - nx * (gw // n) + gi)),
        pl.BlockSpec((1, 1, 3 * r), lambda gi, c: (gi, 0, 0)),
        pl.BlockSpec((1, r, 1), lambda gi, c: (gi, 0, 0)),
        pl.BlockSpec((1, 1, 3 * r), lambda gi, c: (gi, 0, 0)),
        pl.BlockSpec((1, r, 1), lambda gi, c: (gi, 0, 0)),
        pl.BlockSpec((1, gw), lambda gi, c: (0, gi)),
        pl.BlockSpec((1, gw), lambda gi, c: (0, gi)),
        pl.BlockSpec((3 * r, gw), lambda gi, c: (0, 0)),
    ]
    return pl.pallas_call(
        functools.partial(_ssd_kernel, n_heads=r, head_dim=p),
        grid=(g, s // l),
        in_specs=in_specs,
        out_specs=pl.BlockSpec((l, gw), lambda gi, c: (c, gi)),
        out_shape=jax.ShapeDtypeStruct((s, di), BF16),
        scratch_shapes=[pltpu.VMEM((n, gw), F32),
                        pltpu.VMEM((SUBLANES + l, gw), F32),
                        pltpu.VMEM((SUBLANES + l, n), F32),
                        pltpu.VMEM((SUBLANES + l, n), F32),
                        pltpu.VMEM((l, gw), F32)],
        compiler_params=_params("parallel", "arbitrary"),
        name="ssd_core",
    )(zxbc, zxbc, zxbc, zxbc, dtg3, dtt, cw, cb, cw, cb, cw, cb,
      bias3, biast, alog3, alogt, dexp, nw, e3)


def ssd_mixer(h, norm_w_in, w_in, conv_w, conv_b, dt_bias, a_log, d_skip, norm_w, w_out):
    d = h.shape[1]
    n_heads = dt_bias.shape[0]
    di = n_heads * SSD_HEAD_DIM
    n_main = 2 * di + 2 * SSD_N_GROUPS * SSD_D_STATE
    u = rmsnorm(h, norm_w_in)
    zxbc = matmul(u, w_in, k=d, n_out=n_main, tm=2048, tn=512, name="ssd_in_proj")
    dt_raw = matmul(u, w_in, k=d, n_out=n_heads, tm=2048, tn=LANES, w_col0=n_main,
                    name="ssd_dt_proj")
    y = ssd_core(zxbc, dt_raw, conv_w, conv_b, dt_bias, a_log, d_skip, norm_w)
    for kb in range(di // d):
        h = matmul(y, w_out, k=d, n_out=d, tm=2048, tn=256, a_kblk=kb, w_kblk=kb,
                   residual=h, name="ssd_out_proj")
    return h


def _pool_diff_kernel(v_ref, halo_ref, o_ref, *, group_dim, chunk):
    i = pl.program_id(0)
    tm = v_ref.shape[0]
    pos = i * tm + lax.broadcasted_iota(jnp.int32, (tm, chunk), 0)
    for g, window in enumerate(POOL_WINDOWS):
        inv_count = 1.0 / jnp.minimum(pos + 1, window).astype(F32)
        for c0 in range(g * group_dim, (g + 1) * group_dim, chunk):
            v = v_ref[:, c0:c0 + chunk]
            halo = jnp.where(i > 0, halo_ref[:, c0:c0 + chunk], 0.0)
            sw = jnp.concatenate([halo, v], axis=0)
            shift = 1
            while shift < window:
                sw = sw + pltpu.roll(sw, shift, 0)
                shift *= 2
            d = sw[POOL_HALO:] * inv_count - v
            o_ref[:, c0:c0 + chunk] = d.astype(o_ref.dtype)


def pool_diff(proj, width, group_dim, *, tm=256):
    s = proj.shape[0]
    tm = min(tm, s)
    chunk = min(512, group_dim)
    halo_blocks = tm // POOL_HALO
    return pl.pallas_call(
        functools.partial(_pool_diff_kernel, group_dim=group_dim, chunk=chunk),
        grid=(s // tm,),
        in_specs=[pl.BlockSpec((tm, width), lambda i: (i, 0)),
                  pl.BlockSpec((POOL_HALO, width),
                               lambda i: (jnp.maximum(i * halo_blocks - 1, 0), 0))],
        out_specs=pl.BlockSpec((tm, width), lambda i: (i, 0)),
        out_shape=jax.ShapeDtypeStruct((s, width), BF16),
        compiler_params=_params("parallel"),
        name="pool_diff",
    )(proj, proj)


def _pool_group_kernel(d_ref, w_ref, b_ref, sc_ref, z_ref, o_ref):
    acc = jnp.dot(d_ref[...], w_ref[0].astype(BF16), preferred_element_type=F32)
    acc = (acc + b_ref[0]) * sc_ref[...]
    o_ref[...] = (acc * _silu(z_ref[...])).astype(o_ref.dtype)


def pool_core(proj, w_grp, b_grp, scale, *, tm=2048, tn=512):
    s = proj.shape[0]
    ng, gd, _ = w_grp.shape
    assert ng == len(POOL_WINDOWS) and max(POOL_WINDOWS) <= POOL_HALO
    width = ng * gd
    d = pool_diff(proj, width, gd)
    tm = min(tm, s)
    tn = min(tn, gd)
    nj = gd // tn
    in_specs = [
        pl.BlockSpec((tm, gd), lambda g, i, j: (i, g), pipeline_mode=pl.Buffered(1)),
        pl.BlockSpec((1, gd, tn), lambda g, i, j: (g, 0, j)),
        pl.BlockSpec((1, 1, tn), lambda g, i, j: (g, 0, j)),
        pl.BlockSpec((1, tn), lambda g, i, j: (0, g * nj + j)),
        pl.BlockSpec((tm, tn), lambda g, i, j: (i, (width + g * gd) // tn + j)),
    ]
    return pl.pallas_call(
        _pool_group_kernel,
        grid=(ng, s // tm, nj),
        in_specs=in_specs,
        out_specs=pl.BlockSpec((tm, tn), lambda g, i, j: (i, g * nj + j)),
        out_shape=jax.ShapeDtypeStruct((s, width), BF16),
        compiler_params=_params("parallel", "parallel", "arbitrary"),
        name="pool_group",
    )(d, w_grp, b_grp.reshape(ng, 1, gd), scale.reshape(1, width), proj)


def pool_mixer(h, norm_w_in, w_in, w_grp, b_grp, scale, w_out):
    d = h.shape[1]
    width = scale.shape[0]
    u = rmsnorm(h, norm_w_in)
    proj = matmul(u, w_in, k=d, n_out=2 * width, tm=2048, tn=512, name="pool_in_proj")
    y = pool_core(proj, w_grp, b_grp, scale)
    for kb in range(width // d):
        h = matmul(y, w_out, k=d, n_out=d, tm=2048, tn=256, a_kblk=kb, w_kblk=kb,
                   residual=h, name="pool_out_proj")
    return h


def _moba_prep_kernel(q_ref, k_ref, v_ref, qw_ref, kw_ref, qs_ref, ka_ref, va_ref, km_ref):
    i = pl.program_id(0)

    @pl.when(i == 0)
    def _():
        km_ref[...] = jnp.zeros_like(km_ref)

    dh = MOBA_HEAD_DIM
    rows = q_ref.shape[0]
    n_heads = q_ref.shape[1] // dh
    scale = math.log2(math.e) / math.sqrt(dh)
    qw = qw_ref[...] * scale
    kw = kw_ref[...]
    lane = lax.broadcasted_iota(jnp.int32, (rows, dh), 1)
    onehot = jnp.where(lane == i, 1.0, 0.0).astype(ka_ref.dtype)
    ones = jnp.ones((rows, dh), va_ref.dtype)
    means = []
    for h in range(n_heads):
        sl = slice(h * dh, (h + 1) * dh)
        lo = slice(2 * h * dh, (2 * h + 1) * dh)
        hi = slice((2 * h + 1) * dh, (2 * h + 2) * dh)
        q = q_ref[:, sl]
        qs = q * lax.rsqrt(jnp.mean(q * q, axis=-1, keepdims=True) + RMS_EPS) * qw
        qs_ref[:, sl] = qs.astype(qs_ref.dtype)
        k = k_ref[:, sl]
        kn = k * lax.rsqrt(jnp.mean(k * k, axis=-1, keepdims=True) + RMS_EPS) * kw
        ka_ref[:, lo] = kn.astype(ka_ref.dtype)
        ka_ref[:, hi] = onehot
        va_ref[:, lo] = v_ref[:, sl].astype(va_ref.dtype)
        va_ref[:, hi] = ones
        means.append(jnp.mean(kn, axis=0, keepdims=True))
    mean_row = jnp.concatenate(means, axis=1)
    blk = lax.broadcasted_iota(jnp.int32, km_ref.shape, 0)
    km_ref[...] = jnp.where(blk == i, mean_row, km_ref[...])


def moba_prep(proj, q_norm_w, k_norm_w):
    s = proj.shape[0]
    width = proj.shape[1] // 4
    blk = MOBA_BLOCK
    nb = s // blk
    assert nb <= MOBA_HEAD_DIM
    row_spec = lambda c: pl.BlockSpec((blk, width), lambda i: (i, c))
    return pl.pallas_call(
        _moba_prep_kernel,
        grid=(nb,),
        in_specs=[row_spec(0), row_spec(1), row_spec(2),
                  pl.BlockSpec((1, MOBA_HEAD_DIM), lambda i: (0, 0)),
                  pl.BlockSpec((1, MOBA_HEAD_DIM), lambda i: (0, 0))],
        out_specs=[pl.BlockSpec((blk, width), lambda i: (i, 0)),
                   pl.BlockSpec((blk, 2 * width), lambda i: (i, 0)),
                   pl.BlockSpec((blk, 2 * width), lambda i: (i, 0)),
                   pl.BlockSpec((nb, width), lambda i: (0, 0))],
        out_shape=[jax.ShapeDtypeStruct((s, width), BF16),
                   jax.ShapeDtypeStruct((s, 2 * width), BF16),
                   jax.ShapeDtypeStruct((s, 2 * width), BF16),
                   jax.ShapeDtypeStruct((nb, width), F32)],
        compiler_params=_params("arbitrary"),
        name="moba_prep",
    )(proj, proj, proj, q_norm_w.reshape(1, -1), k_norm_w.reshape(1, -1))


def _moba_attn_kernel(qa_in_ref, qb_in_ref, k_ref, v_ref, km_ref, ga_ref, gb_ref, oa_ref, ob_ref,
                      *scratch, pairs):
    g = pl.program_id(1)
    tq = MOBA_BLOCK
    per_pair = len(scratch) // pairs
    stages = {}
    for e in range(pairs):
        a_rows = slice(e * tq, (e + 1) * tq)
        b_rows = slice((pairs - 1 - e) * tq, (pairs - e) * tq)
        stages[e] = _moba_pair(
            pairs * g + e, qa_in_ref.at[a_rows], qb_in_ref.at[b_rows], k_ref, v_ref, km_ref,
            ga_ref.at[a_rows], gb_ref.at[b_rows], oa_ref.at[a_rows], ob_ref.at[b_rows],
            *scratch[e * per_pair:(e + 1) * per_pair])
    tick = 0
    while stages:
        for e in sorted(stages):
            if tick >= e * MOBA_STAGE_LAG and next(stages[e], True):
                del stages[e]
        tick += 1


def _moba_pair(p, qa_in_ref, qb_in_ref, k_ref, v_ref, km_ref, ga_ref, gb_ref, oa_ref, ob_ref,
               s_ref, sstat_ref, sown_ref, qa_ref, mrun_ref):
    blk = MOBA_BLOCK
    dh = MOBA_HEAD_DIM
    tq = qa_in_ref.shape[0]
    nb = km_ref.shape[0]
    n_dynamic = nb // 2 - 1
    nt = (((1,), (1,)), ((), ()))
    q_blocks = (p, nb - 1 - p)
    q2 = jnp.concatenate([qa_in_ref[...], qb_in_ref[...]], axis=0)

    gscore = lax.dot_general(km_ref[...].astype(BF16), q2, nt, preferred_element_type=F32)
    brow = lax.broadcasted_iota(jnp.int32, (nb, 2 * tq), 0)
    qcol = lax.broadcasted_iota(jnp.int32, (nb, 2 * tq), 1)
    browf = brow.astype(F32)
    own_blk = jnp.where(qcol < tq, q_blocks[0], q_blocks[1])
    gwork = jnp.where(brow < own_blk, gscore, -jnp.inf)
    sel = jnp.zeros((nb, 2 * tq), F32)
    for _ in range(min(MOBA_TOPK, nb)):
        top = jnp.max(gwork, axis=0, keepdims=True)
        is_top = (gwork == top) & (top > -jnp.inf)
        first = jnp.min(jnp.where(is_top, browf, float(nb)), axis=0, keepdims=True)
        pick = browf == first
        sel = jnp.where(pick, 1.0, sel)
        gwork = jnp.where(pick, -jnp.inf, gwork)
    bias_t = jnp.where(sel > 0.5, 0.0, MOBA_MASKED)
    if nb < dh:
        bias_t = jnp.concatenate([bias_t, jnp.zeros((dh - nb, 2 * tq), F32)], axis=0)
    q2a = jnp.concatenate([q2, bias_t.T.astype(BF16)], axis=1)
    yield

    def lane_max(s):
        m = s[:, :LANES]
        for c0 in range(LANES, blk, LANES):
            m = jnp.maximum(m, s[:, c0:c0 + LANES])
        return m

    r = lax.broadcasted_iota(jnp.int32, (tq, blk), 0)
    c = lax.broadcasted_iota(jnp.int32, (tq, blk), 1)
    own_start = []
    for t in range(2):
        qa_ref[t] = q2a[t * tq:(t + 1) * tq]
        start = pl.multiple_of(q_blocks[t] * blk, blk)
        own_start.append(start)
        s_own = lax.dot_general(q2[t * tq:(t + 1) * tq], k_ref[pl.ds(start, blk), pl.ds(0, dh)], nt,
                                preferred_element_type=F32)
        s_own = jnp.where(c <= r, s_own, MOBA_MASKED)
        sown_ref[t] = s_own
        mrun_ref[t] = lane_max(s_own)
    yield

    n_static = nb // 2
    sc_static = lax.dot_general(qa_ref[1], k_ref[pl.ds(0, n_static * blk), :], nt,
                                preferred_element_type=F32)
    sstat_ref[...] = sc_static
    m_a = mrun_ref[0]
    m_b = mrun_ref[1]
    for c0 in range(0, n_static * blk, LANES):
        m_b = jnp.maximum(m_b, sc_static[:, c0:c0 + LANES])
    yield

    def slot(u):
        is_a = u < p
        tile = jnp.where(is_a, 0, 1)
        kj = jnp.where(is_a, u, n_static + u - p)
        return is_a, tile, pl.multiple_of(kj * blk, blk)

    for u in range(n_dynamic):
        is_a, tile, start = slot(u)
        sc = lax.dot_general(qa_ref[tile], k_ref[pl.ds(start, blk), :], nt,
                             preferred_element_type=F32)
        s_ref[u] = sc
        sm = lane_max(sc)
        m_a = jnp.maximum(m_a, jnp.where(is_a, sm, MOBA_MASKED))
        m_b = jnp.maximum(m_b, jnp.where(is_a, MOBA_MASKED, sm))
        if u % MOBA_STAGE_SLOTS == MOBA_STAGE_SLOTS - 1:
            yield

    def probs(sc, m_lanes):
        return jnp.concatenate([jnp.exp2(sc[:, c0:c0 + LANES] - m_lanes)
                                for c0 in range(0, sc.shape[1], LANES)], axis=1).astype(BF16)

    m_a = jnp.broadcast_to(jnp.max(m_a, axis=1, keepdims=True), (tq, LANES))
    m_b = jnp.broadcast_to(jnp.max(m_b, axis=1, keepdims=True), (tq, LANES))
    mrun_ref[0] = m_a
    mrun_ref[1] = m_b
    yield

    acc = [jnp.dot(probs(sown_ref[t], mrun_ref[t]), v_ref[pl.ds(own_start[t], blk), :],
                   preferred_element_type=F32) for t in range(2)]
    acc[1] = acc[1] + jnp.dot(probs(sstat_ref[...], mrun_ref[1]), v_ref[pl.ds(0, n_static * blk), :],
                              preferred_element_type=F32)
    yield
    for u in range(n_dynamic):
        is_a, tile, start = slot(u)
        part = jnp.dot(probs(s_ref[u], mrun_ref[tile]), v_ref[pl.ds(start, blk), :],
                       preferred_element_type=F32)
        zero = jnp.zeros_like(part)
        acc = [acc[0] + jnp.where(is_a, part, zero), acc[1] + jnp.where(is_a, zero, part)]
        if u % MOBA_STAGE_SLOTS == MOBA_STAGE_SLOTS - 1:
            yield

    for t, (g_ref, o_ref) in enumerate(((ga_ref, oa_ref), (gb_ref, ob_ref))):
        out = acc[t][:, :dh] / acc[t][:, dh:]
        o_ref[...] = (out * _silu(g_ref[...])).astype(o_ref.dtype)


def moba_attn(qs, ka, va, kmean, proj):
    s, width = qs.shape
    dh = MOBA_HEAD_DIM
    n_heads = width // dh
    blk = MOBA_BLOCK
    tq = blk
    nb = kmean.shape[0]
    assert nb % 2 == 0
    half = nb // 2
    pairs = math.gcd(half, MOBA_PAIRS)
    steps = half // pairs
    rows = pairs * tq
    gate0 = 3 * n_heads
    out_a, out_b = pl.pallas_call(
        functools.partial(_moba_attn_kernel, pairs=pairs),
        grid=(n_heads, steps),
        in_specs=[pl.BlockSpec((rows, dh), lambda h, g: (g, h)),
                  pl.BlockSpec((rows, dh), lambda h, g: (2 * steps - 1 - g, h)),
                  pl.BlockSpec((s, 2 * dh), lambda h, g: (0, h)),
                  pl.BlockSpec((s, 2 * dh), lambda h, g: (0, h)),
                  pl.BlockSpec((nb, dh), lambda h, g: (0, h)),
                  pl.BlockSpec((rows, dh), lambda h, g: (g, gate0 + h)),
                  pl.BlockSpec((rows, dh), lambda h, g: (2 * steps - 1 - g, gate0 + h))],
        out_specs=[pl.BlockSpec((rows, dh), lambda h, g: (g, h)),
                   pl.BlockSpec((rows, dh), lambda h, g: (steps - 1 - g, h))],
        out_shape=[jax.ShapeDtypeStruct((s // 2, width), BF16),
                   jax.ShapeDtypeStruct((s // 2, width), BF16)],
        scratch_shapes=[pltpu.VMEM((max(half - 1, 1), tq, blk), F32),
                        pltpu.VMEM((tq, half * blk), F32),
                        pltpu.VMEM((2, tq, blk), F32),
                        pltpu.VMEM((2, tq, 2 * dh), BF16),
                        pltpu.VMEM((2, tq, LANES), F32)] * pairs,
        compiler_params=_params("parallel", "arbitrary"),
        name="moba_attn",
    )(qs, qs, ka, va, kmean, proj, proj)
    return jnp.concatenate([out_a, out_b], axis=0)


def moba_mixer(h, norm_w_in, w_in, q_norm_w, k_norm_w, w_out):
    s, d = h.shape
    assert s % MOBA_BLOCK == 0
    width = w_out.shape[0]
    u = rmsnorm(h, norm_w_in)
    proj = matmul(u, w_in, k=d, n_out=4 * width, tm=2048, tn=512, name="moba_in_proj")
    qs, ka, va, kmean = moba_prep(proj, q_norm_w, k_norm_w)
    y = moba_attn(qs, ka, va, kmean, proj)
    for kb in range(width // d):
        h = matmul(y, w_out, k=d, n_out=d, tm=2048, tn=256, a_kblk=kb, w_kblk=kb,
                   residual=h, name="moba_out_proj")
    return h


def kernel(x, norm0, ssd0_w_in, ssd0_conv_w, ssd0_conv_b, ssd0_dt_bias, ssd0_a_log, ssd0_d, ssd0_norm_w, ssd0_w_out, norm1, pool1_w_in, pool1_w_grp, pool1_b_grp, pool1_scale, pool1_w_out, norm2, moba2_w_in, moba2_q_norm, moba2_k_norm, moba2_w_out, norm3, ssd3_w_in, ssd3_conv_w, ssd3_conv_b, ssd3_dt_bias, ssd3_a_log, ssd3_d, ssd3_norm_w, ssd3_w_out):
    b, s, d = x.shape
    outs = []
    for bi in range(b):
        h = x[bi]
        h = ssd_mixer(h, norm0, ssd0_w_in, ssd0_conv_w, ssd0_conv_b, ssd0_dt_bias,
                      ssd0_a_log, ssd0_d, ssd0_norm_w, ssd0_w_out)
        h = pool_mixer(h, norm1, pool1_w_in, pool1_w_grp, pool1_b_grp, pool1_scale, pool1_w_out)
        h = moba_mixer(h, norm2, moba2_w_in, moba2_q_norm, moba2_k_norm, moba2_w_out)
        h = ssd_mixer(h, norm3, ssd3_w_in, ssd3_conv_w, ssd3_conv_b, ssd3_dt_bias,
                      ssd3_a_log, ssd3_d, ssd3_norm_w, ssd3_w_out)
        outs.append(h)
    return jnp.stack(outs, axis=0)
```

```python
import functools
import math

import jax
import jax.numpy as jnp
from jax import lax
from jax.experimental import pallas as pl
from jax.experimental.pallas import tpu as pltpu

RMS_EPS = 1e-6

SSD_HEAD_DIM = 64
SSD_N_GROUPS = 8
SSD_D_STATE = 128
SSD_CONV_WIDTH = 4
SSD_CHUNK = 128
SSD_STRIPE = 512
SSD_GROUPS_PER_STEP = 2

POOL_WINDOWS = (2, 4, 8, 16)
POOL_HALO = 16

MOBA_HEAD_DIM = 128
MOBA_BLOCK = 256
MOBA_TOPK = 3
MOBA_MASKED = -1e30
MOBA_PAIRS = 2
MOBA_STAGE_SLOTS = 3
MOBA_STAGE_LAG = 1

MATMUL_ROW_BLOCK = 256

LANES = 128
SUBLANES = 8
VMEM_LIMIT_BYTES = 56 * 1024 * 1024

BF16 = jnp.bfloat16
F32 = jnp.float32


def _params(*semantics):
    return pltpu.CompilerParams(dimension_semantics=semantics,
                                vmem_limit_bytes=VMEM_LIMIT_BYTES)


def _silu(x):
    half = 0.5 * x
    return half + half * jnp.tanh(half)


def _softplus(x):
    return jnp.maximum(x, 0.0) + jnp.log(1.0 + jnp.exp(-jnp.abs(x)))


def _split3(x):
    hi = x.astype(BF16)
    r1 = x - hi.astype(F32)
    mid = r1.astype(BF16)
    lo = (r1 - mid.astype(F32)).astype(BF16)
    return hi, mid, lo


def _rmsnorm_kernel(x_ref, w_ref, o_ref):
    x = x_ref[...]
    ms = jnp.mean(x * x, axis=-1, keepdims=True)
    o_ref[...] = (x * lax.rsqrt(ms + RMS_EPS) * w_ref[...]).astype(o_ref.dtype)


def rmsnorm(x, w, *, tm=256):
    m, d = x.shape
    tm = min(tm, m)
    return pl.pallas_call(
        _rmsnorm_kernel,
        grid=(m // tm,),
        in_specs=[pl.BlockSpec((tm, d), lambda i: (i, 0)),
                  pl.BlockSpec((1, d), lambda i: (0, 0))],
        out_specs=pl.BlockSpec((tm, d), lambda i: (i, 0)),
        out_shape=jax.ShapeDtypeStruct((m, d), BF16),
        compiler_params=_params("parallel"),
        name="rmsnorm",
    )(x, w.reshape(1, d))


def _matmul_kernel(*refs, has_residual):
    if has_residual:
        a_ref, w_ref, r_ref, o_ref = refs
    else:
        a_ref, w_ref, o_ref = refs
    w_b = w_ref[...].astype(BF16)
    if not has_residual:
        o_ref[...] = jnp.dot(a_ref[...], w_b, preferred_element_type=F32).astype(o_ref.dtype)
        return
    rows = a_ref.shape[0]
    rb = math.gcd(rows, MATMUL_ROW_BLOCK)
    for r0 in range(0, rows, rb):
        acc = jnp.dot(a_ref[r0:r0 + rb, :], w_b, preferred_element_type=F32)
        o_ref[r0:r0 + rb, :] = (acc + r_ref[r0:r0 + rb, :]).astype(o_ref.dtype)


def matmul(a, w, *, k, n_out, tm, tn, a_kblk=0, w_kblk=0, w_col0=0, residual=None,
           out_dtype=F32, name="matmul"):
    m = a.shape[0]
    tm = min(tm, m)
    tn = min(tn, n_out)
    assert m % tm == 0 and n_out % tn == 0 and w_col0 % tn == 0
    col0_blk = w_col0 // tn
    in_specs = [pl.BlockSpec((tm, k), lambda i, j: (i, a_kblk), pipeline_mode=pl.Buffered(1)),
                pl.BlockSpec((k, tn), lambda i, j: (w_kblk, j + col0_blk))]
    args = [a, w]
    if residual is not None:
        in_specs.append(pl.BlockSpec((tm, tn), lambda i, j: (i, j)))
        args.append(residual)
    return pl.pallas_call(
        functools.partial(_matmul_kernel, has_residual=residual is not None),
        grid=(m // tm, n_out // tn),
        in_specs=in_specs,
        out_specs=pl.BlockSpec((tm, tn), lambda i, j: (i, j)),
        out_shape=jax.ShapeDtypeStruct((m, n_out), out_dtype),
        compiler_params=_params("parallel", "arbitrary"),
        name=name,
    )(*args)


def _conv_silu(raw_ref, hist_ref, w_ref, b_ref, cols=slice(None)):
    l = raw_ref.shape[0]
    raw = raw_ref[:, cols]
    hist_ref[SUBLANES:, cols] = raw
    w = w_ref[:, cols]
    acc = b_ref[:, cols] + raw * w[SSD_CONV_WIDTH - 1:SSD_CONV_WIDTH, :]
    for j in range(1, SSD_CONV_WIDTH):
        acc = acc + hist_ref[pl.ds(SUBLANES - j, l), cols] * w[SSD_CONV_WIDTH - 1 - j:SSD_CONV_WIDTH - j, :]
    hist_ref[:SUBLANES, cols] = raw[l - SUBLANES:, :]
    return _silu(acc)


SSD_N_INPUTS = 19
SSD_N_SCRATCH = 5


def _ssd_kernel(*refs, n_heads, head_dim, groups):
    n_in = SSD_N_INPUTS * groups
    o_ref = refs[n_in]
    scratch = refs[n_in + 1:]
    gw = o_ref.shape[1] // groups
    chunk = pl.program_id(1)

    @pl.when(chunk == 0)
    def _():
        for e in range(groups):
            state_ref, carx_ref, carb_ref, carc_ref, _ = scratch[e * SSD_N_SCRATCH:(e + 1) * SSD_N_SCRATCH]
            state_ref[...] = jnp.zeros_like(state_ref)
            for hist_ref in (carx_ref, carb_ref, carc_ref):
                hist_ref[:SUBLANES, :] = jnp.zeros((SUBLANES, hist_ref.shape[1]), F32)

    stages = [_ssd_group(*refs[e * SSD_N_INPUTS:(e + 1) * SSD_N_INPUTS],
                         o_ref.at[:, e * gw:(e + 1) * gw],
                         *scratch[e * SSD_N_SCRATCH:(e + 1) * SSD_N_SCRATCH],
                         n_heads=n_heads, head_dim=head_dim) for e in range(groups)]
    while stages:
        stages = [stage for stage in stages if not next(stage, True)]


def _ssd_group(z_ref, x_ref, b_ref, c_ref, dt_ref, dtt_ref,
               cwx_ref, cbx_ref, cwb_ref, cbb_ref, cwc_ref, cbc_ref,
               bias_ref, biast_ref, alog_ref, alogt_ref, dexp_ref, nw_ref, e_ref,
               o_ref, state_ref, carx_ref, carb_ref, carc_ref, yz_ref, *, n_heads, head_dim):
    l = x_ref.shape[0]
    gw = x_ref.shape[1]
    bm = _conv_silu(b_ref, carb_ref, cwb_ref, cbb_ref)
    cm = _conv_silu(c_ref, carc_ref, cwc_ref, cbc_ref)

    log2e = math.log2(math.e)
    dt = _softplus(dt_ref[0] + bias_ref[0])
    a = dt * (-log2e * jnp.exp(alog_ref[0]))
    dtt = _softplus(dtt_ref[0] + biast_ref[0])
    at = dtt * (-log2e * jnp.exp(alogt_ref[0]))

    row = lax.broadcasted_iota(jnp.int32, (l, l), 0)
    col = lax.broadcasted_iota(jnp.int32, (l, l), 1)
    tril = row >= col
    tril_b = tril.astype(BF16)
    triu_b = (row <= col).astype(BF16)

    a_cs = sum(jnp.dot(tril_b, p, preferred_element_type=F32) for p in _split3(a))
    a_cst = sum(jnp.dot(p, triu_b, preferred_element_type=F32) for p in _split3(at))

    a_last = a_cs[l - 1:l, :]
    dte = jnp.exp2(a_last - a_cs)
    e3 = jnp.exp2(a_cs)

    stacked = jnp.concatenate([dt, dte, e3], axis=0)
    hi, mid, lo = _split3(stacked)
    grp = lax.broadcasted_iota(jnp.int32, stacked.shape, 1) // n_heads
    pieces = jnp.where(grp == 0, hi, jnp.where(grp == 1, mid, lo))

    cm_b = cm.astype(BF16)
    bmt_b = bm.T.astype(BF16)
    cb = lax.dot_general(cm_b, bm.astype(BF16), (((1,), (1,)), ((), ())),
                         preferred_element_type=F32)
    cb = jnp.where(tril, cb, 0.0)
    yield

    heads_per_vreg = LANES // head_dim
    lane = lax.broadcasted_iota(jnp.int32, (l, LANES), 1)
    stripe = min(SSD_STRIPE, gw)
    ssq = jnp.zeros((l, 1), F32)
    for c0 in range(0, gw, stripe):
        cols = slice(c0, c0 + stripe)
        xs = _conv_silu(x_ref, carx_ref, cwx_ref, cbx_ref, cols)
        expanded = jnp.dot(pieces, e_ref[:, cols], preferred_element_type=F32)
        dt_exp = expanded[:l]
        dte_exp = expanded[l:2 * l]
        e3_exp = expanded[2 * l:]
        xt = xs * dt_exp
        xt_b = xt.astype(BF16)
        yield

        y_cols = []
        for v in range(stripe // LANES):
            x_pair = xt_b[:, v * LANES:(v + 1) * LANES]
            acc = jnp.zeros((l, LANES), F32)
            for s in range(heads_per_vreg):
                h = (c0 // LANES + v) * heads_per_vreg + s
                seg = jnp.minimum(a_cs[:, h:h + 1] - a_cst[h:h + 1, :], 0.0)
                m_h = (cb * jnp.exp2(seg)).astype(BF16)
                in_head = (lane >= s * head_dim) & (lane < (s + 1) * head_dim)
                x_h = jnp.where(in_head, x_pair, jnp.zeros_like(x_pair))
                acc = acc + jnp.dot(m_h, x_h, preferred_element_type=F32)
            y_cols.append(acc)
        y_diag = jnp.concatenate(y_cols, axis=1) if len(y_cols) > 1 else y_cols[0]
        yield

        state = state_ref[:, cols]
        y_off = jnp.dot(cm_b, state.astype(BF16), preferred_element_type=F32) * e3_exp
        xw_b = (xt * dte_exp).astype(BF16)
        new_states = jnp.dot(bmt_b, xw_b, preferred_element_type=F32)
        state_ref[:, cols] = state * e3_exp[l - 1:l, :] + new_states

        y = y_diag + y_off + xs * dexp_ref[:, cols]
        yz = y * _silu(z_ref[:, cols])
        ssq = ssq + jnp.sum(yz * yz, axis=-1, keepdims=True)
        yz_ref[:, cols] = yz
        yield

    inv_rms = lax.rsqrt(ssq * (1.0 / gw) + RMS_EPS)
    o_ref[...] = (yz_ref[...] * inv_rms * nw_ref[...]).astype(o_ref.dtype)


def ssd_core(zxbc, dt_raw, conv_w, conv_b, dt_bias, a_log, d_skip, norm_w):
    s = zxbc.shape[0]
    g, n, p, l = SSD_N_GROUPS, SSD_D_STATE, SSD_HEAD_DIM, SSD_CHUNK
    h = dt_raw.shape[1]
    r = h // g
    gw = r * p
    di = h * p
    l = min(l, s)
    assert gw % LANES == 0 and n % LANES == 0 and di == g * gw
    nx = di // gw
    nb0 = 2 * di // n
    nc0 = nb0 + g

    dtg = dt_raw.reshape(s, g, r).transpose(1, 0, 2)
    dtg3 = jnp.concatenate([dtg, dtg, dtg], axis=-1)
    dtt = dt_raw.reshape(s, g, r).transpose(1, 2, 0)
    bias = dt_bias.reshape(g, 1, r)
    bias3 = jnp.concatenate([bias, bias, bias], axis=-1)
    alog = a_log.reshape(g, 1, r)
    alog3 = jnp.concatenate([alog, alog, alog], axis=-1)
    biast = dt_bias.reshape(g, r, 1)
    alogt = a_log.reshape(g, r, 1)
    dexp = jnp.repeat(d_skip, p).reshape(1, di)
    nw = norm_w.reshape(1, di)
    cw = conv_w
    cb = conv_b.reshape(1, -1)
    head_of_col = jnp.arange(gw) // p
    e1 = (head_of_col[None, :] == jnp.arange(r)[:, None]).astype(BF16)
    e3 = jnp.concatenate([e1, e1, e1], axis=0)

    kw = SSD_CONV_WIDTH
    groups = math.gcd(g, SSD_GROUPS_PER_STEP)
    cb0 = nb0 - nx * (gw // n)
    cc0 = nc0 - nx * (gw // n)

    def group_specs(e):
        grp = lambda gs: gs * groups + e
        specs = [
            pl.BlockSpec((l, gw), lambda gs, c: (c, grp(gs))),
            pl.BlockSpec((l, gw), lambda gs, c: (c, nx + grp(gs))),
            pl.BlockSpec((l, n), lambda gs, c: (c, nb0 + grp(gs))),
            pl.BlockSpec((l, n), lambda gs, c: (c, nc0 + grp(gs))),
            pl.BlockSpec((1, l, 3 * r), lambda gs, c: (grp(gs), c, 0)),
            pl.BlockSpec((1, r, l), lambda gs, c: (grp(gs), 0, c)),
            pl.BlockSpec((kw, gw), lambda gs, c: (0, grp(gs))),
            pl.BlockSpec((1, gw), lambda gs, c: (0, grp(gs))),
            pl.BlockSpec((kw, n), lambda gs, c: (0, cb0 + grp(gs))),
            pl.BlockSpec((1, n), lambda gs, c: (0, cb0 + grp(gs))),
            pl.BlockSpec((kw, n), lambda gs, c: (0, cc0 + grp(gs))),
            pl.BlockSpec((1, n), lambda gs, c: (0, cc0 + grp(gs))),
            pl.BlockSpec((1, 1, 3 * r), lambda gs, c: (grp(gs), 0, 0)),
            pl.BlockSpec((1, r, 1), lambda gs, c: (grp(gs), 0, 0)),
            pl.BlockSpec((1, 1, 3 * r), lambda gs, c: (grp(gs), 0, 0)),
            pl.BlockSpec((1, r, 1), lambda gs, c: (grp(gs), 0, 0)),
            pl.BlockSpec((1, gw), lambda gs, c: (0, grp(gs))),
            pl.BlockSpec((1, gw), lambda gs, c: (0, grp(gs))),
            pl.BlockSpec((3 * r, gw), lambda gs, c: (0, 0)),
        ]
        assert len(specs) == SSD_N_INPUTS
        return specs

    args = (zxbc, zxbc, zxbc, zxbc, dtg3, dtt, cw, cb, cw, cb, cw, cb,
            bias3, biast, alog3, alogt, dexp, nw, e3)
    scratch = [pltpu.VMEM((n, gw), F32),
               pltpu.VMEM((SUBLANES + l, gw), F32),
               pltpu.VMEM((SUBLANES + l, n), F32),
               pltpu.VMEM((SUBLANES + l, n), F32),
               pltpu.VMEM((l, gw), F32)]
    assert len(scratch) == SSD_N_SCRATCH
    return pl.pallas_call(
        functools.partial(_ssd_kernel, n_heads=r, head_dim=p, groups=groups),
        grid=(g // groups, s // l),
        in_specs=[spec for e in range(groups) for spec in group_specs(e)],
        out_specs=pl.BlockSpec((l, groups * gw), lambda gs, c: (c, gs)),
        out_shape=jax.ShapeDtypeStruct((s, di), BF16),
        scratch_shapes=scratch * groups,
        compiler_params=_params("parallel", "arbitrary"),
        name="ssd_core",
    )(*(args * groups))


def ssd_mixer(h, norm_w_in, w_in, conv_w, conv_b, dt_bias, a_log, d_skip, norm_w, w_out):
    d = h.shape[1]
    n_heads = dt_bias.shape[0]
    di = n_heads * SSD_HEAD_DIM
    n_main = 2 * di + 2 * SSD_N_GROUPS * SSD_D_STATE
    u = rmsnorm(h, norm_w_in)
    zxbc = matmul(u, w_in, k=d, n_out=n_main, tm=2048, tn=512, name="ssd_in_proj")
    dt_raw = matmul(u, w_in, k=d, n_out=n_heads, tm=2048, tn=LANES, w_col0=n_main,
                    name="ssd_dt_proj")
    y = ssd_core(zxbc, dt_raw, conv_w, conv_b, dt_bias, a_log, d_skip, norm_w)
    for kb in range(di // d):
        h = matmul(y, w_out, k=d, n_out=d, tm=2048, tn=512, a_kblk=kb, w_kblk=kb,
                   residual=h, name="ssd_out_proj")
    return h


def _pool_diff_kernel(v_ref, halo_ref, o_ref, *, group_dim, chunk):
    i = pl.program_id(0)
    tm = v_ref.shape[0]
    pos = i * tm + lax.broadcasted_iota(jnp.int32, (tm, chunk), 0)
    for g, window in enumerate(POOL_WINDOWS):
        inv_count = 1.0 / jnp.minimum(pos + 1, window).astype(F32)
        for c0 in range(g * group_dim, (g + 1) * group_dim, chunk):
            v = v_ref[:, c0:c0 + chunk]
            halo = jnp.where(i > 0, halo_ref[:, c0:c0 + chunk], 0.0)
            sw = jnp.concatenate([halo, v], axis=0)
            shift = 1
            while shift < window:
                sw = sw + pltpu.roll(sw, shift, 0)
                shift *= 2
            d = sw[POOL_HALO:] * inv_count - v
            o_ref[:, c0:c0 + chunk] = d.astype(o_ref.dtype)


def pool_diff(proj, width, group_dim, *, tm=256):
    s = proj.shape[0]
    tm = min(tm, s)
    chunk = min(512, group_dim)
    halo_blocks = tm // POOL_HALO
    return pl.pallas_call(
        functools.partial(_pool_diff_kernel, group_dim=group_dim, chunk=chunk),
        grid=(s // tm,),
        in_specs=[pl.BlockSpec((tm, width), lambda i: (i, 0)),
                  pl.BlockSpec((POOL_HALO, width),
                               lambda i: (jnp.maximum(i * halo_blocks - 1, 0), 0))],
        out_specs=pl.BlockSpec((tm, width), lambda i: (i, 0)),
        out_shape=jax.ShapeDtypeStruct((s, width), BF16),
        compiler_params=_params("parallel"),
        name="pool_diff",
    )(proj, proj)


def _pool_group_kernel(d_ref, w_ref, b_ref, sc_ref, z_ref, o_ref):
    w_b = w_ref[0].astype(BF16)
    bias = b_ref[0]
    scale = sc_ref[...]
    rows = d_ref.shape[0]
    rb = math.gcd(rows, MATMUL_ROW_BLOCK)
    for r0 in range(0, rows, rb):
        acc = jnp.dot(d_ref[r0:r0 + rb, :], w_b, preferred_element_type=F32)
        acc = (acc + bias) * scale
        o_ref[r0:r0 + rb, :] = (acc * _silu(z_ref[r0:r0 + rb, :])).astype(o_ref.dtype)


def pool_core(proj, w_grp, b_grp, scale, *, tm=2048, tn=1024):
    s = proj.shape[0]
    ng, gd, _ = w_grp.shape
    assert ng == len(POOL_WINDOWS) and max(POOL_WINDOWS) <= POOL_HALO
    width = ng * gd
    d = pool_diff(proj, width, gd)
    tm = min(tm, s)
    tn = min(tn, gd)
    nj = gd // tn
    in_specs = [
        pl.BlockSpec((tm, gd), lambda g, i, j: (i, g), pipeline_mode=pl.Buffered(1)),
        pl.BlockSpec((1, gd, tn), lambda g, i, j: (g, 0, j)),
        pl.BlockSpec((1, 1, tn), lambda g, i, j: (g, 0, j)),
        pl.BlockSpec((1, tn), lambda g, i, j: (0, g * nj + j)),
        pl.BlockSpec((tm, tn), lambda g, i, j: (i, (width + g * gd) // tn + j)),
    ]
    return pl.pallas_call(
        _pool_group_kernel,
        grid=(ng, s // tm, nj),
        in_specs=in_specs,
        out_specs=pl.BlockSpec((tm, tn), lambda g, i, j: (i, g * nj + j)),
        out_shape=jax.ShapeDtypeStruct((s, width), BF16),
        compiler_params=_params("parallel", "parallel", "arbitrary"),
        name="pool_group",
    )(d, w_grp, b_grp.reshape(ng, 1, gd), scale.reshape(1, width), proj)


def pool_mixer(h, norm_w_in, w_in, w_grp, b_grp, scale, w_out):
    d = h.shape[1]
    width = scale.shape[0]
    u = rmsnorm(h, norm_w_in)
    proj = matmul(u, w_in, k=d, n_out=2 * width, tm=2048, tn=512, name="pool_in_proj")
    y = pool_core(proj, w_grp, b_grp, scale)
    for kb in range(width // d):
        h = matmul(y, w_out, k=d, n_out=d, tm=2048, tn=512, a_kblk=kb, w_kblk=kb,
                   residual=h, name="pool_out_proj")
    return h


def _moba_prep_kernel(q_ref, k_ref, v_ref, qw_ref, kw_ref, qs_ref, ka_ref, va_ref, km_ref):
    i = pl.program_id(0)

    @pl.when(i == 0)
    def _():
        km_ref[...] = jnp.zeros_like(km_ref)

    dh = MOBA_HEAD_DIM
    rows = q_ref.shape[0]
    n_heads = q_ref.shape[1] // dh
    scale = math.log2(math.e) / math.sqrt(dh)
    qw = qw_ref[...] * scale
    kw = kw_ref[...]
    lane = lax.broadcasted_iota(jnp.int32, (rows, dh), 1)
    onehot = jnp.where(lane == i, 1.0, 0.0).astype(ka_ref.dtype)
    ones = jnp.ones((rows, dh), va_ref.dtype)
    means = []
    for h in range(n_heads):
        sl = slice(h * dh, (h + 1) * dh)
        lo = slice(2 * h * dh, (2 * h + 1) * dh)
        hi = slice((2 * h + 1) * dh, (2 * h + 2) * dh)
        q = q_ref[:, sl]
        qs = q * lax.rsqrt(jnp.mean(q * q, axis=-1, keepdims=True) + RMS_EPS) * qw
        qs_ref[:, sl] = qs.astype(qs_ref.dtype)
        k = k_ref[:, sl]
        kn = k * lax.rsqrt(jnp.mean(k * k, axis=-1, keepdims=True) + RMS_EPS) * kw
        ka_ref[:, lo] = kn.astype(ka_ref.dtype)
        ka_ref[:, hi] = onehot
        va_ref[:, lo] = v_ref[:, sl].astype(va_ref.dtype)
        va_ref[:, hi] = ones
        means.append(jnp.mean(kn, axis=0, keepdims=True))
    mean_row = jnp.concatenate(means, axis=1)
    blk = lax.broadcasted_iota(jnp.int32, km_ref.shape, 0)
    km_ref[...] = jnp.where(blk == i, mean_row, km_ref[...])


def moba_prep(proj, q_norm_w, k_norm_w):
    s = proj.shape[0]
    width = proj.shape[1] // 4
    blk = MOBA_BLOCK
    nb = s // blk
    assert nb <= MOBA_HEAD_DIM
    row_spec = lambda c: pl.BlockSpec((blk, width), lambda i: (i, c))
    return pl.pallas_call(
        _moba_prep_kernel,
        grid=(nb,),
        in_specs=[row_spec(0), row_spec(1), row_spec(2),
                  pl.BlockSpec((1, MOBA_HEAD_DIM), lambda i: (0, 0)),
                  pl.BlockSpec((1, MOBA_HEAD_DIM), lambda i: (0, 0))],
        out_specs=[pl.BlockSpec((blk, width), lambda i: (i, 0)),
                   pl.BlockSpec((blk, 2 * width), lambda i: (i, 0)),
                   pl.BlockSpec((blk, 2 * width), lambda i: (i, 0)),
                   pl.BlockSpec((nb, width), lambda i: (0, 0))],
        out_shape=[jax.ShapeDtypeStruct((s, width), BF16),
                   jax.ShapeDtypeStruct((s, 2 * width), BF16),
                   jax.ShapeDtypeStruct((s, 2 * width), BF16),
                   jax.ShapeDtypeStruct((nb, width), F32)],
        compiler_params=_params("arbitrary"),
        name="moba_prep",
    )(proj, proj, proj, q_norm_w.reshape(1, -1), k_norm_w.reshape(1, -1))


def _moba_attn_kernel(qa_in_ref, qb_in_ref, k_ref, v_ref, km_ref, ga_ref, gb_ref, oa_ref, ob_ref,
                      *scratch, pairs):
    g = pl.program_id(1)
    tq = MOBA_BLOCK
    per_pair = len(scratch) // pairs
    stages = {}
    for e in range(pairs):
        a_rows = slice(e * tq, (e + 1) * tq)
        b_rows = slice((pairs - 1 - e) * tq, (pairs - e) * tq)
        stages[e] = _moba_pair(
            pairs * g + e, qa_in_ref.at[a_rows], qb_in_ref.at[b_rows], k_ref, v_ref, km_ref,
            ga_ref.at[a_rows], gb_ref.at[b_rows], oa_ref.at[a_rows], ob_ref.at[b_rows],
            *scratch[e * per_pair:(e + 1) * per_pair])
    tick = 0
    while stages:
        for e in sorted(stages):
            if tick >= e * MOBA_STAGE_LAG and next(stages[e], True):
                del stages[e]
        tick += 1


def _moba_pair(p, qa_in_ref, qb_in_ref, k_ref, v_ref, km_ref, ga_ref, gb_ref, oa_ref, ob_ref,
               s_ref, sstat_ref, sown_ref, qa_ref, mrun_ref):
    blk = MOBA_BLOCK
    dh = MOBA_HEAD_DIM
    tq = qa_in_ref.shape[0]
    nb = km_ref.shape[0]
    n_dynamic = nb // 2 - 1
    nt = (((1,), (1,)), ((), ()))
    q_blocks = (p, nb - 1 - p)
    q2 = jnp.concatenate([qa_in_ref[...], qb_in_ref[...]], axis=0)

    gscore = lax.dot_general(km_ref[...].astype(BF16), q2, nt, preferred_element_type=F32)
    brow = lax.broadcasted_iota(jnp.int32, (nb, 2 * tq), 0)
    qcol = lax.broadcasted_iota(jnp.int32, (nb, 2 * tq), 1)
    browf = brow.astype(F32)
    own_blk = jnp.where(qcol < tq, q_blocks[0], q_blocks[1])
    gwork = jnp.where(brow < own_blk, gscore, -jnp.inf)
    sel = jnp.zeros((nb, 2 * tq), F32)
    for _ in range(min(MOBA_TOPK, nb)):
        top = jnp.max(gwork, axis=0, keepdims=True)
        is_top = (gwork == top) & (top > -jnp.inf)
        first = jnp.min(jnp.where(is_top, browf, float(nb)), axis=0, keepdims=True)
        pick = browf == first
        sel = jnp.where(pick, 1.0, sel)
        gwork = jnp.where(pick, -jnp.inf, gwork)
    bias_t = jnp.where(sel > 0.5, 0.0, MOBA_MASKED)
    if nb < dh:
        bias_t = jnp.concatenate([bias_t, jnp.zeros((dh - nb, 2 * tq), F32)], axis=0)
    q2a = jnp.concatenate([q2, bias_t.T.astype(BF16)], axis=1)
    yield

    def lane_max(s):
        m = s[:, :LANES]
        for c0 in range(LANES, blk, LANES):
            m = jnp.maximum(m, s[:, c0:c0 + LANES])
        return m

    r = lax.broadcasted_iota(jnp.int32, (tq, blk), 0)
    c = lax.broadcasted_iota(jnp.int32, (tq, blk), 1)
    own_start = []
    for t in range(2):
        qa_ref[t] = q2a[t * tq:(t + 1) * tq]
        start = pl.multiple_of(q_blocks[t] * blk, blk)
        own_start.append(start)
        s_own = lax.dot_general(q2[t * tq:(t + 1) * tq], k_ref[pl.ds(start, blk), pl.ds(0, dh)], nt,
                                preferred_element_type=F32)
        s_own = jnp.where(c <= r, s_own, MOBA_MASKED)
        sown_ref[t] = s_own
        mrun_ref[t] = lane_max(s_own)
    yield

    n_static = nb // 2
    sc_static = lax.dot_general(qa_ref[1], k_ref[pl.ds(0, n_static * blk), :], nt,
                                preferred_element_type=F32)
    sstat_ref[...] = sc_static
    m_a = mrun_ref[0]
    m_b = mrun_ref[1]
    for c0 in range(0, n_static * blk, LANES):
        m_b = jnp.maximum(m_b, sc_static[:, c0:c0 + LANES])
    yield

    def slot(u):
        is_a = u < p
        tile = jnp.where(is_a, 0, 1)
        kj = jnp.where(is_a, u, n_static + u - p)
        return is_a, tile, pl.multiple_of(kj * blk, blk)

    for u in range(n_dynamic):
        is_a, tile, start = slot(u)
        sc = lax.dot_general(qa_ref[tile], k_ref[pl.ds(start, blk), :], nt,
                             preferred_element_type=F32)
        s_ref[u] = sc
        sm = lane_max(sc)
        m_a = jnp.maximum(m_a, jnp.where(is_a, sm, MOBA_MASKED))
        m_b = jnp.maximum(m_b, jnp.where(is_a, MOBA_MASKED, sm))
        if u % MOBA_STAGE_SLOTS == MOBA_STAGE_SLOTS - 1:
            yield

    def probs(sc, m_lanes):
        return jnp.concatenate([jnp.exp2(sc[:, c0:c0 + LANES] - m_lanes)
                                for c0 in range(0, sc.shape[1], LANES)], axis=1).astype(BF16)

    m_a = jnp.broadcast_to(jnp.max(m_a, axis=1, keepdims=True), (tq, LANES))
    m_b = jnp.broadcast_to(jnp.max(m_b, axis=1, keepdims=True), (tq, LANES))
    mrun_ref[0] = m_a
    mrun_ref[1] = m_b
    yield

    acc = [jnp.dot(probs(sown_ref[t], mrun_ref[t]), v_ref[pl.ds(own_start[t], blk), :],
                   preferred_element_type=F32) for t in range(2)]
    acc[1] = acc[1] + jnp.dot(probs(sstat_ref[...], mrun_ref[1]), v_ref[pl.ds(0, n_static * blk), :],
                              preferred_element_type=F32)
    yield
    for u in range(n_dynamic):
        is_a, tile, start = slot(u)
        part = jnp.dot(probs(s_ref[u], mrun_ref[tile]), v_ref[pl.ds(start, blk), :],
                       preferred_element_type=F32)
        zero = jnp.zeros_like(part)
        acc = [acc[0] + jnp.where(is_a, part, zero), acc[1] + jnp.where(is_a, zero, part)]
        if u % MOBA_STAGE_SLOTS == MOBA_STAGE_SLOTS - 1:
            yield

    for t, (g_ref, o_ref) in enumerate(((ga_ref, oa_ref), (gb_ref, ob_ref))):
        out = acc[t][:, :dh] / acc[t][:, dh:]
        o_ref[...] = (out * _silu(g_ref[...])).astype(o_ref.dtype)


def moba_attn(qs, ka, va, kmean, proj):
    s, width = qs.shape
    dh = MOBA_HEAD_DIM
    n_heads = width // dh
    blk = MOBA_BLOCK
    tq = blk
    nb = kmean.shape[0]
    assert nb % 2 == 0
    half = nb // 2
    pairs = math.gcd(half, MOBA_PAIRS)
    steps = half // pairs
    rows = pairs * tq
    gate0 = 3 * n_heads
    out_a, out_b = pl.pallas_call(
        functools.partial(_moba_attn_kernel, pairs=pairs),
        grid=(n_heads, steps),
        in_specs=[pl.BlockSpec((rows, dh), lambda h, g: (g, h)),
                  pl.BlockSpec((rows, dh), lambda h, g: (2 * steps - 1 - g, h)),
                  pl.BlockSpec((s, 2 * dh), lambda h, g: (0, h)),
                  pl.BlockSpec((s, 2 * dh), lambda h, g: (0, h)),
                  pl.BlockSpec((nb, dh), lambda h, g: (0, h)),
                  pl.BlockSpec((rows, dh), lambda h, g: (g, gate0 + h)),
                  pl.BlockSpec((rows, dh), lambda h, g: (2 * steps - 1 - g, gate0 + h))],
        out_specs=[pl.BlockSpec((rows, dh), lambda h, g: (g, h)),
                   pl.BlockSpec((rows, dh), lambda h, g: (steps - 1 - g, h))],
        out_shape=[jax.ShapeDtypeStruct((s // 2, width), BF16),
                   jax.ShapeDtypeStruct((s // 2, width), BF16)],
        scratch_shapes=[pltpu.VMEM((max(half - 1, 1), tq, blk), F32),
                        pltpu.VMEM((tq, half * blk), F32),
                        pltpu.VMEM((2, tq, blk), F32),
                        pltpu.VMEM((2, tq, 2 * dh), BF16),
                        pltpu.VMEM((2, tq, LANES), F32)] * pairs,
        compiler_params=_params("parallel", "arbitrary"),
        name="moba_attn",
    )(qs, qs, ka, va, kmean, proj, proj)
    return jnp.concatenate([out_a, out_b], axis=0)


def moba_mixer(h, norm_w_in, w_in, q_norm_w, k_norm_w, w_out):
    s, d = h.shape
    assert s % MOBA_BLOCK == 0
    width = w_out.shape[0]
    u = rmsnorm(h, norm_w_in)
    proj = matmul(u, w_in, k=d, n_out=4 * width, tm=2048, tn=512, name="moba_in_proj")
    qs, ka, va, kmean = moba_prep(proj, q_norm_w, k_norm_w)
    y = moba_attn(qs, ka, va, kmean, proj)
    for kb in range(width // d):
        h = matmul(y, w_out, k=d, n_out=d, tm=2048, tn=512, a_kblk=kb, w_kblk=kb,
                   residual=h, name="moba_out_proj")
    return h


def kernel(x, norm0, ssd0_w_in, ssd0_conv_w, ssd0_conv_b, ssd0_dt_bias, ssd0_a_log, ssd0_d, ssd0_norm_w, ssd0_w_out, norm1, pool1_w_in, pool1_w_grp, pool1_b_grp, pool1_scale, pool1_w_out, norm2, moba2_w_in, moba2_q_norm, moba2_k_norm, moba2_w_out, norm3, ssd3_w_in, ssd3_conv_w, ssd3_conv_b, ssd3_dt_bias, ssd3_a_log, ssd3_d, ssd3_norm_w, ssd3_w_out):
    b, s, d = x.shape
    outs = []
    for bi in range(b):
        h = x[bi]
        h = ssd_mixer(h, norm0, ssd0_w_in, ssd0_conv_w, ssd0_conv_b, ssd0_dt_bias,
                      ssd0_a_log, ssd0_d, ssd0_norm_w, ssd0_w_out)
        h = pool_mixer(h, norm1, pool1_w_in, pool1_w_grp, pool1_b_grp, pool1_scale, pool1_w_out)
        h = moba_mixer(h, norm2, moba2_w_in, moba2_q_norm, moba2_k_norm, moba2_w_out)
        h = ssd_mixer(h, norm3, ssd3_w_in, ssd3_conv_w, ssd3_conv_b, ssd3_dt_bias,
                      ssd3_a_log, ssd3_d, ssd3_norm_w, ssd3_w_out)
        outs.append(h)
    return jnp.stack(outs, axis=0)
```

```python
import functools
import math

import jax
import jax.numpy as jnp
from jax import lax
from jax.experimental import pallas as pl
from jax.experimental.pallas import tpu as pltpu

RMS_EPS = 1e-6

SSD_HEAD_DIM = 64
SSD_N_GROUPS = 8
SSD_D_STATE = 128
SSD_CONV_WIDTH = 4
SSD_CHUNK = 128
SSD_STRIPE = 512
SSD_GROUPS_PER_STEP = 2

POOL_WINDOWS = (2, 4, 8, 16)
POOL_HALO = 16

MOBA_HEAD_DIM = 128
MOBA_BLOCK = 256
MOBA_TOPK = 3
MOBA_MASKED = -1e30
MOBA_PAIRS = 2
MOBA_STAGE_SLOTS = 3
MOBA_STAGE_LAG = 1

MATMUL_ROW_BLOCK = 256

LANES = 128
SUBLANES = 8
VMEM_LIMIT_BYTES = 56 * 1024 * 1024

BF16 = jnp.bfloat16
F32 = jnp.float32


def _params(*semantics):
    return pltpu.CompilerParams(dimension_semantics=semantics,
                                vmem_limit_bytes=VMEM_LIMIT_BYTES)


def _silu(x):
    half = 0.5 * x
    return half + half * jnp.tanh(half)


def _softplus(x):
    return jnp.maximum(x, 0.0) + jnp.log(1.0 + jnp.exp(-jnp.abs(x)))


def _split3(x):
    hi = x.astype(BF16)
    r1 = x - hi.astype(F32)
    mid = r1.astype(BF16)
    lo = (r1 - mid.astype(F32)).astype(BF16)
    return hi, mid, lo


def _rmsnorm_kernel(x_ref, w_ref, o_ref):
    x = x_ref[...]
    ms = jnp.mean(x * x, axis=-1, keepdims=True)
    o_ref[...] = (x * lax.rsqrt(ms + RMS_EPS) * w_ref[...]).astype(o_ref.dtype)


def rmsnorm(x, w, *, tm=256):
    m, d = x.shape
    tm = min(tm, m)
    return pl.pallas_call(
        _rmsnorm_kernel,
        grid=(m // tm,),
        in_specs=[pl.BlockSpec((tm, d), lambda i: (i, 0)),
                  pl.BlockSpec((1, d), lambda i: (0, 0))],
        out_specs=pl.BlockSpec((tm, d), lambda i: (i, 0)),
        out_shape=jax.ShapeDtypeStruct((m, d), BF16),
        compiler_params=_params("parallel"),
        name="rmsnorm",
    )(x, w.reshape(1, d))


def _matmul_kernel(*refs, has_residual):
    if has_residual:
        a_ref, w_ref, r_ref, o_ref = refs
    else:
        a_ref, w_ref, o_ref = refs
    w_b = w_ref[...].astype(BF16)
    if not has_residual:
        o_ref[...] = jnp.dot(a_ref[...], w_b, preferred_element_type=F32).astype(o_ref.dtype)
        return
    rows = a_ref.shape[0]
    rb = math.gcd(rows, MATMUL_ROW_BLOCK)
    for r0 in range(0, rows, rb):
        acc = jnp.dot(a_ref[r0:r0 + rb, :], w_b, preferred_element_type=F32)
        o_ref[r0:r0 + rb, :] = (acc + r_ref[r0:r0 + rb, :]).astype(o_ref.dtype)


def matmul(a, w, *, k, n_out, tm, tn, a_kblk=0, w_kblk=0, w_col0=0, residual=None,
           out_dtype=F32, name="matmul"):
    m = a.shape[0]
    tm = min(tm, m)
    tn = min(tn, n_out)
    assert m % tm == 0 and n_out % tn == 0 and w_col0 % tn == 0
    col0_blk = w_col0 // tn
    in_specs = [pl.BlockSpec((tm, k), lambda i, j: (i, a_kblk), pipeline_mode=pl.Buffered(1)),
                pl.BlockSpec((k, tn), lambda i, j: (w_kblk, j + col0_blk))]
    args = [a, w]
    if residual is not None:
        in_specs.append(pl.BlockSpec((tm, tn), lambda i, j: (i, j)))
        args.append(residual)
    return pl.pallas_call(
        functools.partial(_matmul_kernel, has_residual=residual is not None),
        grid=(m // tm, n_out // tn),
        in_specs=in_specs,
        out_specs=pl.BlockSpec((tm, tn), lambda i, j: (i, j)),
        out_shape=jax.ShapeDtypeStruct((m, n_out), out_dtype),
        compiler_params=_params("parallel", "arbitrary"),
        name=name,
    )(*args)


def _conv_silu(raw_ref, hist_ref, w_ref, b_ref, cols=slice(None)):
    l = raw_ref.shape[0]
    raw = raw_ref[:, cols]
    hist_ref[SUBLANES:, cols] = raw
    w = w_ref[:, cols]
    acc = b_ref[:, cols] + raw * w[SSD_CONV_WIDTH - 1:SSD_CONV_WIDTH, :]
    for j in range(1, SSD_CONV_WIDTH):
        acc = acc + hist_ref[pl.ds(SUBLANES - j, l), cols] * w[SSD_CONV_WIDTH - 1 - j:SSD_CONV_WIDTH - j, :]
    hist_ref[:SUBLANES, cols] = raw[l - SUBLANES:, :]
    return _silu(acc)


SSD_N_STREAMED = 6
SSD_N_PARAMS = 13
SSD_N_SCRATCH = 5


def _ssd_kernel(*refs, n_heads, head_dim, groups):
    n_in = SSD_N_STREAMED * groups
    params = refs[n_in:n_in + SSD_N_PARAMS]
    o_ref = refs[n_in + SSD_N_PARAMS]
    scratch = refs[n_in + SSD_N_PARAMS + 1:]
    gw = o_ref.shape[1] // groups
    chunk = pl.program_id(1)

    def group_params(e):
        grp = pl.program_id(0) * groups + e
        return [ref.at[grp] for ref in params[:-1]] + [params[-1]]

    @pl.when(chunk == 0)
    def _():
        for e in range(groups):
            state_ref, carx_ref, carb_ref, carc_ref, _ = scratch[e * SSD_N_SCRATCH:(e + 1) * SSD_N_SCRATCH]
            state_ref[...] = jnp.zeros_like(state_ref)
            for hist_ref in (carx_ref, carb_ref, carc_ref):
                hist_ref[:SUBLANES, :] = jnp.zeros((SUBLANES, hist_ref.shape[1]), F32)

    stages = [_ssd_group(*refs[e * SSD_N_STREAMED:(e + 1) * SSD_N_STREAMED], *group_params(e),
                         o_ref.at[:, e * gw:(e + 1) * gw],
                         *scratch[e * SSD_N_SCRATCH:(e + 1) * SSD_N_SCRATCH],
                         n_heads=n_heads, head_dim=head_dim) for e in range(groups)]
    while stages:
        stages = [stage for stage in stages if not next(stage, True)]


def _ssd_group(z_ref, x_ref, b_ref, c_ref, dt_ref, dtt_ref,
               cwx_ref, cbx_ref, cwb_ref, cbb_ref, cwc_ref, cbc_ref,
               bias_ref, biast_ref, alog_ref, alogt_ref, dexp_ref, nw_ref, e_ref,
               o_ref, state_ref, carx_ref, carb_ref, carc_ref, yz_ref, *, n_heads, head_dim):
    l = x_ref.shape[0]
    gw = x_ref.shape[1]
    bm = _conv_silu(b_ref, carb_ref, cwb_ref, cbb_ref)
    cm = _conv_silu(c_ref, carc_ref, cwc_ref, cbc_ref)

    log2e = math.log2(math.e)
    dt = _softplus(dt_ref[0] + bias_ref[0])
    a = dt * (-log2e * jnp.exp(alog_ref[0]))
    dtt = _softplus(dtt_ref[0] + biast_ref[0])
    at = dtt * (-log2e * jnp.exp(alogt_ref[0]))

    row = lax.broadcasted_iota(jnp.int32, (l, l), 0)
    col = lax.broadcasted_iota(jnp.int32, (l, l), 1)
    tril = row >= col
    tril_b = tril.astype(BF16)
    triu_b = (row <= col).astype(BF16)

    a_cs = sum(jnp.dot(tril_b, p, preferred_element_type=F32) for p in _split3(a))
    a_cst = sum(jnp.dot(p, triu_b, preferred_element_type=F32) for p in _split3(at))

    a_last = a_cs[l - 1:l, :]
    dte = jnp.exp2(a_last - a_cs)
    e3 = jnp.exp2(a_cs)

    stacked = jnp.concatenate([dt, dte, e3], axis=0)
    hi, mid, lo = _split3(stacked)
    grp = lax.broadcasted_iota(jnp.int32, stacked.shape, 1) // n_heads
    pieces = jnp.where(grp == 0, hi, jnp.where(grp == 1, mid, lo))

    cm_b = cm.astype(BF16)
    bmt_b = bm.T.astype(BF16)
    cb = lax.dot_general(cm_b, bm.astype(BF16), (((1,), (1,)), ((), ())),
                         preferred_element_type=F32)
    cb = jnp.where(tril, cb, 0.0)
    yield

    heads_per_vreg = LANES // head_dim
    lane = lax.broadcasted_iota(jnp.int32, (l, LANES), 1)
    stripe = min(SSD_STRIPE, gw)
    ssq = jnp.zeros((l, 1), F32)
    for c0 in range(0, gw, stripe):
        cols = slice(c0, c0 + stripe)
        xs = _conv_silu(x_ref, carx_ref, cwx_ref, cbx_ref, cols)
        expanded = jnp.dot(pieces, e_ref[:, cols], preferred_element_type=F32)
        dt_exp = expanded[:l]
        dte_exp = expanded[l:2 * l]
        e3_exp = expanded[2 * l:]
        xt = xs * dt_exp
        xt_b = xt.astype(BF16)
        yield

        y_cols = []
        for v in range(stripe // LANES):
            x_pair = xt_b[:, v * LANES:(v + 1) * LANES]
            acc = jnp.zeros((l, LANES), F32)
            for s in range(heads_per_vreg):
                h = (c0 // LANES + v) * heads_per_vreg + s
                seg = jnp.minimum(a_cs[:, h:h + 1] - a_cst[h:h + 1, :], 0.0)
                m_h = (cb * jnp.exp2(seg)).astype(BF16)
                in_head = (lane >= s * head_dim) & (lane < (s + 1) * head_dim)
                x_h = jnp.where(in_head, x_pair, jnp.zeros_like(x_pair))
                acc = acc + jnp.dot(m_h, x_h, preferred_element_type=F32)
            y_cols.append(acc)
        y_diag = jnp.concatenate(y_cols, axis=1) if len(y_cols) > 1 else y_cols[0]
        yield

        state = state_ref[:, cols]
        y_off = jnp.dot(cm_b, state.astype(BF16), preferred_element_type=F32) * e3_exp
        xw_b = (xt * dte_exp).astype(BF16)
        new_states = jnp.dot(bmt_b, xw_b, preferred_element_type=F32)
        state_ref[:, cols] = state * e3_exp[l - 1:l, :] + new_states

        y = y_diag + y_off + xs * dexp_ref[:, cols]
        yz = y * _silu(z_ref[:, cols])
        ssq = ssq + jnp.sum(yz * yz, axis=-1, keepdims=True)
        yz_ref[:, cols] = yz
        yield

    inv_rms = lax.rsqrt(ssq * (1.0 / gw) + RMS_EPS)
    o_ref[...] = (yz_ref[...] * inv_rms * nw_ref[...]).astype(o_ref.dtype)


def ssd_core(zxbc, dt_raw, conv_w, conv_b, dt_bias, a_log, d_skip, norm_w):
    s = zxbc.shape[0]
    g, n, p, l = SSD_N_GROUPS, SSD_D_STATE, SSD_HEAD_DIM, SSD_CHUNK
    h = dt_raw.shape[1]
    r = h // g
    gw = r * p
    di = h * p
    l = min(l, s)
    assert gw % LANES == 0 and n % LANES == 0 and di == g * gw
    nx = di // gw
    nb0 = 2 * di // n
    nc0 = nb0 + g

    dtg = dt_raw.reshape(s, g, r).transpose(1, 0, 2)
    dtg3 = jnp.concatenate([dtg, dtg, dtg], axis=-1)
    dtt = dt_raw.reshape(s, g, r).transpose(1, 2, 0)
    kw = SSD_CONV_WIDTH
    bias = dt_bias.reshape(g, 1, 1, r)
    bias3 = jnp.concatenate([bias, bias, bias], axis=-1)
    alog = a_log.reshape(g, 1, 1, r)
    alog3 = jnp.concatenate([alog, alog, alog], axis=-1)
    biast = dt_bias.reshape(g, 1, r, 1)
    alogt = a_log.reshape(g, 1, r, 1)
    dexp = jnp.repeat(d_skip, p).reshape(g, 1, gw)
    nw = norm_w.reshape(g, 1, gw)
    per_group = lambda a, width: a.reshape(a.shape[0], g, width).transpose(1, 0, 2)
    cb_row = conv_b.reshape(1, -1)
    cwx, cbx = per_group(conv_w[:, :di], gw), per_group(cb_row[:, :di], gw)
    cwb, cbb = per_group(conv_w[:, di:di + g * n], n), per_group(cb_row[:, di:di + g * n], n)
    cwc, cbc = per_group(conv_w[:, di + g * n:], n), per_group(cb_row[:, di + g * n:], n)
    head_of_col = jnp.arange(gw) // p
    e1 = (head_of_col[None, :] == jnp.arange(r)[:, None]).astype(BF16)
    e3 = jnp.concatenate([e1, e1, e1], axis=0)
    params = (cwx, cbx, cwb, cbb, cwc, cbc, bias3, biast, alog3, alogt, dexp, nw, e3)
    assert len(params) == SSD_N_PARAMS

    groups = math.gcd(g, SSD_GROUPS_PER_STEP)

    def group_specs(e):
        grp = lambda gs: gs * groups + e
        specs = [
            pl.BlockSpec((l, gw), lambda gs, c: (c, grp(gs))),
            pl.BlockSpec((l, gw), lambda gs, c: (c, nx + grp(gs))),
            pl.BlockSpec((l, n), lambda gs, c: (c, nb0 + grp(gs))),
            pl.BlockSpec((l, n), lambda gs, c: (c, nc0 + grp(gs))),
            pl.BlockSpec((1, l, 3 * r), lambda gs, c: (grp(gs), c, 0)),
            pl.BlockSpec((1, r, l), lambda gs, c: (grp(gs), 0, c)),
        ]
        assert len(specs) == SSD_N_STREAMED
        return specs

    streamed = (zxbc, zxbc, zxbc, zxbc, dtg3, dtt)
    scratch = [pltpu.VMEM((n, gw), F32),
               pltpu.VMEM((SUBLANES + l, gw), F32),
               pltpu.VMEM((SUBLANES + l, n), F32),
               pltpu.VMEM((SUBLANES + l, n), F32),
               pltpu.VMEM((l, gw), F32)]
    assert len(scratch) == SSD_N_SCRATCH
    return pl.pallas_call(
        functools.partial(_ssd_kernel, n_heads=r, head_dim=p, groups=groups),
        grid=(g // groups, s // l),
        in_specs=([spec for e in range(groups) for spec in group_specs(e)]
                  + [pl.BlockSpec(memory_space=pltpu.VMEM)] * SSD_N_PARAMS),
        out_specs=pl.BlockSpec((l, groups * gw), lambda gs, c: (c, gs)),
        out_shape=jax.ShapeDtypeStruct((s, di), BF16),
        scratch_shapes=scratch * groups,
        compiler_params=_params("parallel", "arbitrary"),
        name="ssd_core",
    )(*(streamed * groups), *params)


def ssd_mixer(h, norm_w_in, w_in, conv_w, conv_b, dt_bias, a_log, d_skip, norm_w, w_out):
    d = h.shape[1]
    n_heads = dt_bias.shape[0]
    di = n_heads * SSD_HEAD_DIM
    n_main = 2 * di + 2 * SSD_N_GROUPS * SSD_D_STATE
    u = rmsnorm(h, norm_w_in)
    zxbc = matmul(u, w_in, k=d, n_out=n_main, tm=2048, tn=512, name="ssd_in_proj")
    dt_raw = matmul(u, w_in, k=d, n_out=n_heads, tm=2048, tn=LANES, w_col0=n_main,
                    name="ssd_dt_proj")
    y = ssd_core(zxbc, dt_raw, conv_w, conv_b, dt_bias, a_log, d_skip, norm_w)
    for kb in range(di // d):
        h = matmul(y, w_out, k=d, n_out=d, tm=2048, tn=512, a_kblk=kb, w_kblk=kb,
                   residual=h, name="ssd_out_proj")
    return h


def _pool_diff_kernel(v_ref, halo_ref, o_ref, *, group_dim, chunk):
    i = pl.program_id(0)
    tm = v_ref.shape[0]
    pos = i * tm + lax.broadcasted_iota(jnp.int32, (tm, chunk), 0)
    for g, window in enumerate(POOL_WINDOWS):
        inv_count = 1.0 / jnp.minimum(pos + 1, window).astype(F32)
        for c0 in range(g * group_dim, (g + 1) * group_dim, chunk):
            v = v_ref[:, c0:c0 + chunk]
            halo = jnp.where(i > 0, halo_ref[:, c0:c0 + chunk], 0.0)
            sw = jnp.concatenate([halo, v], axis=0)
            shift = 1
            while shift < window:
                sw = sw + pltpu.roll(sw, shift, 0)
                shift *= 2
            d = sw[POOL_HALO:] * inv_count - v
            o_ref[:, c0:c0 + chunk] = d.astype(o_ref.dtype)


def pool_diff(proj, width, group_dim, *, tm=256):
    s = proj.shape[0]
    tm = min(tm, s)
    chunk = min(512, group_dim)
    halo_blocks = tm // POOL_HALO
    return pl.pallas_call(
        functools.partial(_pool_diff_kernel, group_dim=group_dim, chunk=chunk),
        grid=(s // tm,),
        in_specs=[pl.BlockSpec((tm, width), lambda i: (i, 0)),
                  pl.BlockSpec((POOL_HALO, width),
                               lambda i: (jnp.maximum(i * halo_blocks - 1, 0), 0))],
        out_specs=pl.BlockSpec((tm, width), lambda i: (i, 0)),
        out_shape=jax.ShapeDtypeStruct((s, width), BF16),
        compiler_params=_params("parallel"),
        name="pool_diff",
    )(proj, proj)


def _pool_group_kernel(d_ref, w_ref, b_ref, sc_ref, z_ref, o_ref):
    w_b = w_ref[0].astype(BF16)
    bias = b_ref[0]
    scale = sc_ref[...]
    rows = d_ref.shape[0]
    rb = math.gcd(rows, MATMUL_ROW_BLOCK)
    for r0 in range(0, rows, rb):
        acc = jnp.dot(d_ref[r0:r0 + rb, :], w_b, preferred_element_type=F32)
        acc = (acc + bias) * scale
        o_ref[r0:r0 + rb, :] = (acc * _silu(z_ref[r0:r0 + rb, :])).astype(o_ref.dtype)


def pool_core(proj, w_grp, b_grp, scale, *, tm=2048, tn=1024):
    s = proj.shape[0]
    ng, gd, _ = w_grp.shape
    assert ng == len(POOL_WINDOWS) and max(POOL_WINDOWS) <= POOL_HALO
    width = ng * gd
    d = pool_diff(proj, width, gd)
    tm = min(tm, s)
    tn = min(tn, gd)
    nj = gd // tn
    in_specs = [
        pl.BlockSpec((tm, gd), lambda g, i, j: (i, g), pipeline_mode=pl.Buffered(1)),
        pl.BlockSpec((1, gd, tn), lambda g, i, j: (g, 0, j)),
        pl.BlockSpec((1, 1, tn), lambda g, i, j: (g, 0, j)),
        pl.BlockSpec((1, tn), lambda g, i, j: (0, g * nj + j)),
        pl.BlockSpec((tm, tn), lambda g, i, j: (i, (width + g * gd) // tn + j)),
    ]
    return pl.pallas_call(
        _pool_group_kernel,
        grid=(ng, s // tm, nj),
        in_specs=in_specs,
        out_specs=pl.BlockSpec((tm, tn), lambda g, i, j: (i, g * nj + j)),
        out_shape=jax.ShapeDtypeStruct((s, width), BF16),
        compiler_params=_params("parallel", "parallel", "arbitrary"),
        name="pool_group",
    )(d, w_grp, b_grp.reshape(ng, 1, gd), scale.reshape(1, width), proj)


def pool_mixer(h, norm_w_in, w_in, w_grp, b_grp, scale, w_out):
    d = h.shape[1]
    width = scale.shape[0]
    u = rmsnorm(h, norm_w_in)
    proj = matmul(u, w_in, k=d, n_out=2 * width, tm=2048, tn=512, name="pool_in_proj")
    y = pool_core(proj, w_grp, b_grp, scale)
    for kb in range(width // d):
        h = matmul(y, w_out, k=d, n_out=d, tm=2048, tn=512, a_kblk=kb, w_kblk=kb,
                   residual=h, name="pool_out_proj")
    return h


def _moba_prep_kernel(q_ref, k_ref, v_ref, qw_ref, kw_ref, qs_ref, ka_ref, va_ref, km_ref):
    i = pl.program_id(0)

    @pl.when(i == 0)
    def _():
        km_ref[...] = jnp.zeros_like(km_ref)

    dh = MOBA_HEAD_DIM
    rows = q_ref.shape[0]
    n_heads = q_ref.shape[1] // dh
    scale = math.log2(math.e) / math.sqrt(dh)
    qw = qw_ref[...] * scale
    kw = kw_ref[...]
    lane = lax.broadcasted_iota(jnp.int32, (rows, dh), 1)
    onehot = jnp.where(lane == i, 1.0, 0.0).astype(ka_ref.dtype)
    ones = jnp.ones((rows, dh), va_ref.dtype)
    means = []
    for h in range(n_heads):
        sl = slice(h * dh, (h + 1) * dh)
        lo = slice(2 * h * dh, (2 * h + 1) * dh)
        hi = slice((2 * h + 1) * dh, (2 * h + 2) * dh)
        q = q_ref[:, sl]
        qs = q * lax.rsqrt(jnp.mean(q * q, axis=-1, keepdims=True) + RMS_EPS) * qw
        qs_ref[:, sl] = qs.astype(qs_ref.dtype)
        k = k_ref[:, sl]
        kn = k * lax.rsqrt(jnp.mean(k * k, axis=-1, keepdims=True) + RMS_EPS) * kw
        ka_ref[:, lo] = kn.astype(ka_ref.dtype)
        ka_ref[:, hi] = onehot
        va_ref[:, lo] = v_ref[:, sl].astype(va_ref.dtype)
        va_ref[:, hi] = ones
        means.append(jnp.mean(kn, axis=0, keepdims=True))
    mean_row = jnp.concatenate(means, axis=1)
    blk = lax.broadcasted_iota(jnp.int32, km_ref.shape, 0)
    km_ref[...] = jnp.where(blk == i, mean_row, km_ref[...])


def moba_prep(proj, q_norm_w, k_norm_w):
    s = proj.shape[0]
    width = proj.shape[1] // 4
    blk = MOBA_BLOCK
    nb = s // blk
    assert nb <= MOBA_HEAD_DIM
    row_spec = lambda c: pl.BlockSpec((blk, width), lambda i: (i, c))
    return pl.pallas_call(
        _moba_prep_kernel,
        grid=(nb,),
        in_specs=[row_spec(0), row_spec(1), row_spec(2),
                  pl.BlockSpec((1, MOBA_HEAD_DIM), lambda i: (0, 0)),
                  pl.BlockSpec((1, MOBA_HEAD_DIM), lambda i: (0, 0))],
        out_specs=[pl.BlockSpec((blk, width), lambda i: (i, 0)),
                   pl.BlockSpec((blk, 2 * width), lambda i: (i, 0)),
                   pl.BlockSpec((blk, 2 * width), lambda i: (i, 0)),
                   pl.BlockSpec((nb, width), lambda i: (0, 0))],
        out_shape=[jax.ShapeDtypeStruct((s, width), BF16),
                   jax.ShapeDtypeStruct((s, 2 * width), BF16),
                   jax.ShapeDtypeStruct((s, 2 * width), BF16),
                   jax.ShapeDtypeStruct((nb, width), F32)],
        compiler_params=_params("arbitrary"),
        name="moba_prep",
    )(proj, proj, proj, q_norm_w.reshape(1, -1), k_norm_w.reshape(1, -1))


def _moba_attn_kernel(qa_in_ref, qb_in_ref, k_ref, v_ref, km_ref, ga_ref, gb_ref, oa_ref, ob_ref,
                      *scratch, pairs):
    g = pl.program_id(1)
    tq = MOBA_BLOCK
    per_pair = len(scratch) // pairs
    stages = {}
    for e in range(pairs):
        a_rows = slice(e * tq, (e + 1) * tq)
        b_rows = slice((pairs - 1 - e) * tq, (pairs - e) * tq)
        stages[e] = _moba_pair(
            pairs * g + e, qa_in_ref.at[a_rows], qb_in_ref.at[b_rows], k_ref, v_ref, km_ref,
            ga_ref.at[a_rows], gb_ref.at[b_rows], oa_ref.at[a_rows], ob_ref.at[b_rows],
            *scratch[e * per_pair:(e + 1) * per_pair])
    tick = 0
    while stages:
        for e in sorted(stages):
            if tick >= e * MOBA_STAGE_LAG and next(stages[e], True):
                del stages[e]
        tick += 1


def _moba_pair(p, qa_in_ref, qb_in_ref, k_ref, v_ref, km_ref, ga_ref, gb_ref, oa_ref, ob_ref,
               s_ref, sstat_ref, sown_ref, qa_ref, mrun_ref, mstat_ref):
    blk = MOBA_BLOCK
    dh = MOBA_HEAD_DIM
    tq = qa_in_ref.shape[0]
    nb = km_ref.shape[0]
    n_dynamic = nb // 2 - 1
    nt = (((1,), (1,)), ((), ()))
    q_blocks = (p, nb - 1 - p)
    q2 = jnp.concatenate([qa_in_ref[...], qb_in_ref[...]], axis=0)

    gscore = lax.dot_general(km_ref[...].astype(BF16), q2, nt, preferred_element_type=F32)
    brow = lax.broadcasted_iota(jnp.int32, (nb, 2 * tq), 0)
    qcol = lax.broadcasted_iota(jnp.int32, (nb, 2 * tq), 1)
    browf = brow.astype(F32)
    own_blk = jnp.where(qcol < tq, q_blocks[0], q_blocks[1])
    gwork = jnp.where(brow < own_blk, gscore, -jnp.inf)
    sel = jnp.zeros((nb, 2 * tq), F32)
    for _ in range(min(MOBA_TOPK, nb)):
        top = jnp.max(gwork, axis=0, keepdims=True)
        is_top = (gwork == top) & (top > -jnp.inf)
        first = jnp.min(jnp.where(is_top, browf, float(nb)), axis=0, keepdims=True)
        pick = browf == first
        sel = jnp.where(pick, 1.0, sel)
        gwork = jnp.where(pick, -jnp.inf, gwork)
    bias_t = jnp.where(sel > 0.5, 0.0, MOBA_MASKED)
    if nb < dh:
        bias_t = jnp.concatenate([bias_t, jnp.zeros((dh - nb, 2 * tq), F32)], axis=0)
    q2a = jnp.concatenate([q2, bias_t.T.astype(BF16)], axis=1)
    yield

    def lane_max(s):
        m = s[:, :LANES]
        for c0 in range(LANES, blk, LANES):
            m = jnp.maximum(m, s[:, c0:c0 + LANES])
        return m

    r = lax.broadcasted_iota(jnp.int32, (tq, blk), 0)
    c = lax.broadcasted_iota(jnp.int32, (tq, blk), 1)
    own_start = []
    for t in range(2):
        qa_ref[t] = q2a[t * tq:(t + 1) * tq]
        start = pl.multiple_of(q_blocks[t] * blk, blk)
        own_start.append(start)
        s_own = lax.dot_general(q2[t * tq:(t + 1) * tq], k_ref[pl.ds(start, blk), pl.ds(0, dh)], nt,
                                preferred_element_type=F32)
        s_own = jnp.where(c <= r, s_own, MOBA_MASKED)
        sown_ref[t] = s_own
        mrun_ref[t] = lane_max(s_own)
    yield

    n_static = nb // 2
    sc_static = lax.dot_general(qa_ref[1], k_ref[pl.ds(0, n_static * blk), :], nt,
                                preferred_element_type=F32)
    sstat_ref[...] = sc_static
    m_a = mrun_ref[0]
    m_b = mrun_ref[1]
    for c0 in range(0, n_static * blk, LANES):
        m_b = jnp.maximum(m_b, sc_static[:, c0:c0 + LANES])
    m_b_static = jnp.broadcast_to(jnp.max(m_b, axis=1, keepdims=True), (tq, LANES))
    mstat_ref[...] = m_b_static
    yield

    def slot(u):
        is_a = u < p
        tile = jnp.where(is_a, 0, 1)
        kj = jnp.where(is_a, u, n_static + u - p)
        return is_a, tile, pl.multiple_of(kj * blk, blk)

    def probs(sc, m_lanes):
        return jnp.concatenate([jnp.exp2(sc[:, c0:c0 + LANES] - m_lanes)
                                for c0 in range(0, sc.shape[1], LANES)], axis=1).astype(BF16)

    def static_part():
        return (jnp.dot(probs(sown_ref[1], mstat_ref[...]), v_ref[pl.ds(own_start[1], blk), :],
                        preferred_element_type=F32)
                + jnp.dot(probs(sstat_ref[...], mstat_ref[...]), v_ref[pl.ds(0, n_static * blk), :],
                          preferred_element_type=F32))

    acc_b_static = None
    for u in range(n_dynamic):
        is_a, tile, start = slot(u)
        sc = lax.dot_general(qa_ref[tile], k_ref[pl.ds(start, blk), :], nt,
                             preferred_element_type=F32)
        s_ref[u] = sc
        sm = lane_max(sc)
        m_a = jnp.maximum(m_a, jnp.where(is_a, sm, MOBA_MASKED))
        m_b = jnp.maximum(m_b, jnp.where(is_a, MOBA_MASKED, sm))
        if u % MOBA_STAGE_SLOTS == MOBA_STAGE_SLOTS - 1:
            yield
            if acc_b_static is None:
                acc_b_static = static_part()
                yield
    if acc_b_static is None:
        acc_b_static = static_part()

    m_a = jnp.broadcast_to(jnp.max(m_a, axis=1, keepdims=True), (tq, LANES))
    m_b = jnp.maximum(jnp.broadcast_to(jnp.max(m_b, axis=1, keepdims=True), (tq, LANES)),
                      mstat_ref[...])
    mrun_ref[0] = m_a
    mrun_ref[1] = m_b
    yield

    rescale = jnp.exp2(mstat_ref[...] - mrun_ref[1])
    acc = [jnp.dot(probs(sown_ref[0], mrun_ref[0]), v_ref[pl.ds(own_start[0], blk), :],
                   preferred_element_type=F32),
           jnp.concatenate([acc_b_static[:, c0:c0 + LANES] * rescale
                            for c0 in range(0, 2 * dh, LANES)], axis=1)]
    yield
    for u in range(n_dynamic):
        is_a, tile, start = slot(u)
        part = jnp.dot(probs(s_ref[u], mrun_ref[tile]), v_ref[pl.ds(start, blk), :],
                       preferred_element_type=F32)
        zero = jnp.zeros_like(part)
        acc = [acc[0] + jnp.where(is_a, part, zero), acc[1] + jnp.where(is_a, zero, part)]
        if u % MOBA_STAGE_SLOTS == MOBA_STAGE_SLOTS - 1:
            yield

    for t, (g_ref, o_ref) in enumerate(((ga_ref, oa_ref), (gb_ref, ob_ref))):
        out = acc[t][:, :dh] / acc[t][:, dh:]
        o_ref[...] = (out * _silu(g_ref[...])).astype(o_ref.dtype)


def moba_attn(qs, ka, va, kmean, proj):
    s, width = qs.shape
    dh = MOBA_HEAD_DIM
    n_heads = width // dh
    blk = MOBA_BLOCK
    tq = blk
    nb = kmean.shape[0]
    assert nb % 2 == 0
    half = nb // 2
    pairs = math.gcd(half, MOBA_PAIRS)
    steps = half // pairs
    rows = pairs * tq
    gate0 = 3 * n_heads
    out_a, out_b = pl.pallas_call(
        functools.partial(_moba_attn_kernel, pairs=pairs),
        grid=(n_heads, steps),
        in_specs=[pl.BlockSpec((rows, dh), lambda h, g: (g, h)),
                  pl.BlockSpec((rows, dh), lambda h, g: (2 * steps - 1 - g, h)),
                  pl.BlockSpec((s, 2 * dh), lambda h, g: (0, h)),
                  pl.BlockSpec((s, 2 * dh), lambda h, g: (0, h)),
                  pl.BlockSpec((nb, dh), lambda h, g: (0, h)),
                  pl.BlockSpec((rows, dh), lambda h, g: (g, gate0 + h)),
                  pl.BlockSpec((rows, dh), lambda h, g: (2 * steps - 1 - g, gate0 + h))],
        out_specs=[pl.BlockSpec((rows, dh), lambda h, g: (g, h)),
                   pl.BlockSpec((rows, dh), lambda h, g: (steps - 1 - g, h))],
        out_shape=[jax.ShapeDtypeStruct((s // 2, width), BF16),
                   jax.ShapeDtypeStruct((s // 2, width), BF16)],
        scratch_shapes=[pltpu.VMEM((max(half - 1, 1), tq, blk), F32),
                        pltpu.VMEM((tq, half * blk), F32),
                        pltpu.VMEM((2, tq, blk), F32),
                        pltpu.VMEM((2, tq, 2 * dh), BF16),
                        pltpu.VMEM((2, tq, LANES), F32),
                        pltpu.VMEM((tq, LANES), F32)] * pairs,
        compiler_params=_params("parallel", "arbitrary"),
        name="moba_attn",
    )(qs, qs, ka, va, kmean, proj, proj)
    return jnp.concatenate([out_a, out_b], axis=0)


def moba_mixer(h, norm_w_in, w_in, q_norm_w, k_norm_w, w_out):
    s, d = h.shape
    assert s % MOBA_BLOCK == 0
    width = w_out.shape[0]
    u = rmsnorm(h, norm_w_in)
    proj = matmul(u, w_in, k=d, n_out=4 * width, tm=2048, tn=512, name="moba_in_proj")
    qs, ka, va, kmean = moba_prep(proj, q_norm_w, k_norm_w)
    y = moba_attn(qs, ka, va, kmean, proj)
    for kb in range(width // d):
        h = matmul(y, w_out, k=d, n_out=d, tm=2048, tn=512, a_kblk=kb, w_kblk=kb,
                   residual=h, name="moba_out_proj")
    return h


def kernel(x, norm0, ssd0_w_in, ssd0_conv_w, ssd0_conv_b, ssd0_dt_bias, ssd0_a_log, ssd0_d, ssd0_norm_w, ssd0_w_out, norm1, pool1_w_in, pool1_w_grp, pool1_b_grp, pool1_scale, pool1_w_out, norm2, moba2_w_in, moba2_q_norm, moba2_k_norm, moba2_w_out, norm3, ssd3_w_in, ssd3_conv_w, ssd3_conv_b, ssd3_dt_bias, ssd3_a_log, ssd3_d, ssd3_norm_w, ssd3_w_out):
    b, s, d = x.shape
    outs = []
    for bi in range(b):
        h = x[bi]
        h = ssd_mixer(h, norm0, ssd0_w_in, ssd0_conv_w, ssd0_conv_b, ssd0_dt_bias,
                      ssd0_a_log, ssd0_d, ssd0_norm_w, ssd0_w_out)
        h = pool_mixer(h, norm1, pool1_w_in, pool1_w_grp, pool1_b_grp, pool1_scale, pool1_w_out)
        h = moba_mixer(h, norm2, moba2_w_in, moba2_q_norm, moba2_k_norm, moba2_w_out)
        h = ssd_mixer(h, norm3, ssd3_w_in, ssd3_conv_w, ssd3_conv_b, ssd3_dt_bias,
                      ssd3_a_log, ssd3_d, ssd3_norm_w, ssd3_w_out)
        outs.append(h)
    return jnp.stack(outs, axis=0)
```

```python
import functools
import math

import jax
import jax.numpy as jnp
from jax import lax
from jax.experimental import pallas as pl
from jax.experimental.pallas import tpu as pltpu

RMS_EPS = 1e-6

SSD_HEAD_DIM = 64
SSD_N_GROUPS = 8
SSD_D_STATE = 128
SSD_CONV_WIDTH = 4
SSD_CHUNK = 128
SSD_STRIPE = 512
SSD_GROUPS_PER_STEP = 2

POOL_WINDOWS = (2, 4, 8, 16)
POOL_HALO = 16

MOBA_HEAD_DIM = 128
MOBA_BLOCK = 256
MOBA_TOPK = 3
MOBA_MASKED = -1e30
MOBA_PAIRS = 2
MOBA_STAGE_SLOTS = 4
MOBA_STAGE_LAG = 1

MATMUL_ROW_BLOCK = 256

LANES = 128
SUBLANES = 8
VMEM_LIMIT_BYTES = 56 * 1024 * 1024

BF16 = jnp.bfloat16
F32 = jnp.float32


def _params(*semantics):
    return pltpu.CompilerParams(dimension_semantics=semantics,
                                vmem_limit_bytes=VMEM_LIMIT_BYTES)


def _silu(x):
    half = 0.5 * x
    return half + half * jnp.tanh(half)


def _softplus(x):
    return jnp.maximum(x, 0.0) + jnp.log(1.0 + jnp.exp(-jnp.abs(x)))


def _split3(x):
    hi = x.astype(BF16)
    r1 = x - hi.astype(F32)
    mid = r1.astype(BF16)
    lo = (r1 - mid.astype(F32)).astype(BF16)
    return hi, mid, lo


def _rmsnorm_kernel(x_ref, w_ref, o_ref):
    x = x_ref[...]
    ms = jnp.mean(x * x, axis=-1, keepdims=True)
    o_ref[...] = (x * lax.rsqrt(ms + RMS_EPS) * w_ref[...]).astype(o_ref.dtype)


def rmsnorm(x, w, *, tm=256):
    m, d = x.shape
    tm = min(tm, m)
    return pl.pallas_call(
        _rmsnorm_kernel,
        grid=(m // tm,),
        in_specs=[pl.BlockSpec((tm, d), lambda i: (i, 0)),
                  pl.BlockSpec((1, d), lambda i: (0, 0))],
        out_specs=pl.BlockSpec((tm, d), lambda i: (i, 0)),
        out_shape=jax.ShapeDtypeStruct((m, d), BF16),
        compiler_params=_params("parallel"),
        name="rmsnorm",
    )(x, w.reshape(1, d))


def _matmul_kernel(*refs, has_residual):
    if has_residual:
        a_ref, w_ref, r_ref, o_ref = refs
    else:
        a_ref, w_ref, o_ref = refs
    w_b = w_ref[...].astype(BF16)
    if not has_residual:
        o_ref[...] = jnp.dot(a_ref[...], w_b, preferred_element_type=F32).astype(o_ref.dtype)
        return
    rows = a_ref.shape[0]
    rb = math.gcd(rows, MATMUL_ROW_BLOCK)
    for r0 in range(0, rows, rb):
        acc = jnp.dot(a_ref[r0:r0 + rb, :], w_b, preferred_element_type=F32)
        o_ref[r0:r0 + rb, :] = (acc + r_ref[r0:r0 + rb, :]).astype(o_ref.dtype)


def matmul(a, w, *, k, n_out, tm, tn, a_kblk=0, w_kblk=0, w_col0=0, residual=None,
           out_dtype=F32, name="matmul"):
    m = a.shape[0]
    tm = min(tm, m)
    tn = min(tn, n_out)
    assert m % tm == 0 and n_out % tn == 0 and w_col0 % tn == 0
    col0_blk = w_col0 // tn
    in_specs = [pl.BlockSpec((tm, k), lambda i, j: (i, a_kblk), pipeline_mode=pl.Buffered(1)),
                pl.BlockSpec((k, tn), lambda i, j: (w_kblk, j + col0_blk))]
    args = [a, w]
    if residual is not None:
        in_specs.append(pl.BlockSpec((tm, tn), lambda i, j: (i, j)))
        args.append(residual)
    return pl.pallas_call(
        functools.partial(_matmul_kernel, has_residual=residual is not None),
        grid=(m // tm, n_out // tn),
        in_specs=in_specs,
        out_specs=pl.BlockSpec((tm, tn), lambda i, j: (i, j)),
        out_shape=jax.ShapeDtypeStruct((m, n_out), out_dtype),
        compiler_params=_params("parallel", "arbitrary"),
        name=name,
    )(*args)


def _conv_silu(raw_ref, hist_ref, w_ref, b_ref, cols=slice(None)):
    l = raw_ref.shape[0]
    raw = raw_ref[:, cols]
    hist_ref[SUBLANES:, cols] = raw
    w = w_ref[:, cols]
    acc = b_ref[:, cols] + raw * w[SSD_CONV_WIDTH - 1:SSD_CONV_WIDTH, :]
    for j in range(1, SSD_CONV_WIDTH):
        acc = acc + hist_ref[pl.ds(SUBLANES - j, l), cols] * w[SSD_CONV_WIDTH - 1 - j:SSD_CONV_WIDTH - j, :]
    hist_ref[:SUBLANES, cols] = raw[l - SUBLANES:, :]
    return _silu(acc)


SSD_N_STREAMED = 6
SSD_N_PARAMS = 13
SSD_N_SCRATCH = 5


def _ssd_kernel(*refs, n_heads, head_dim, groups):
    n_in = SSD_N_STREAMED * groups
    params = refs[n_in:n_in + SSD_N_PARAMS]
    o_ref = refs[n_in + SSD_N_PARAMS]
    scratch = refs[n_in + SSD_N_PARAMS + 1:]
    gw = o_ref.shape[1] // groups
    chunk = pl.program_id(1)

    def group_params(e):
        grp = pl.program_id(0) * groups + e
        return [ref.at[grp] for ref in params[:-1]] + [params[-1]]

    @pl.when(chunk == 0)
    def _():
        for e in range(groups):
            state_ref, carx_ref, carb_ref, carc_ref, _ = scratch[e * SSD_N_SCRATCH:(e + 1) * SSD_N_SCRATCH]
            state_ref[...] = jnp.zeros_like(state_ref)
            for hist_ref in (carx_ref, carb_ref, carc_ref):
                hist_ref[:SUBLANES, :] = jnp.zeros((SUBLANES, hist_ref.shape[1]), F32)

    stages = [_ssd_group(*refs[e * SSD_N_STREAMED:(e + 1) * SSD_N_STREAMED], *group_params(e),
                         o_ref.at[:, e * gw:(e + 1) * gw],
                         *scratch[e * SSD_N_SCRATCH:(e + 1) * SSD_N_SCRATCH],
                         n_heads=n_heads, head_dim=head_dim) for e in range(groups)]
    while stages:
        stages = [stage for stage in stages if not next(stage, True)]


def _ssd_group(z_ref, x_ref, b_ref, c_ref, dt_ref, dtt_ref,
               cwx_ref, cbx_ref, cwb_ref, cbb_ref, cwc_ref, cbc_ref,
               bias_ref, biast_ref, alog_ref, alogt_ref, dexp_ref, nw_ref, e_ref,
               o_ref, state_ref, carx_ref, carb_ref, carc_ref, yz_ref, *, n_heads, head_dim):
    l = x_ref.shape[0]
    gw = x_ref.shape[1]
    bm = _conv_silu(b_ref, carb_ref, cwb_ref, cbb_ref)
    cm = _conv_silu(c_ref, carc_ref, cwc_ref, cbc_ref)

    log2e = math.log2(math.e)
    dt = _softplus(dt_ref[0] + bias_ref[0])
    a = dt * (-log2e * jnp.exp(alog_ref[0]))
    dtt = _softplus(dtt_ref[0] + biast_ref[0])
    at = dtt * (-log2e * jnp.exp(alogt_ref[0]))

    row = lax.broadcasted_iota(jnp.int32, (l, l), 0)
    col = lax.broadcasted_iota(jnp.int32, (l, l), 1)
    tril = row >= col
    tril_b = tril.astype(BF16)
    triu_b = (row <= col).astype(BF16)

    a_cs = sum(jnp.dot(tril_b, p, preferred_element_type=F32) for p in _split3(a))
    a_cst = sum(jnp.dot(p, triu_b, preferred_element_type=F32) for p in _split3(at))

    a_last = a_cs[l - 1:l, :]
    dte = jnp.exp2(a_last - a_cs)
    e3 = jnp.exp2(a_cs)

    stacked = jnp.concatenate([dt, dte, e3], axis=0)
    hi, mid, lo = _split3(stacked)
    grp = lax.broadcasted_iota(jnp.int32, stacked.shape, 1) // n_heads
    pieces = jnp.where(grp == 0, hi, jnp.where(grp == 1, mid, lo))

    cm_b = cm.astype(BF16)
    bmt_b = bm.T.astype(BF16)
    cb = lax.dot_general(cm_b, bm.astype(BF16), (((1,), (1,)), ((), ())),
                         preferred_element_type=F32)
    cb = jnp.where(tril, cb, 0.0)
    yield

    heads_per_vreg = LANES // head_dim
    lane = lax.broadcasted_iota(jnp.int32, (l, LANES), 1)
    stripe = min(SSD_STRIPE, gw)
    ssq = jnp.zeros((l, 1), F32)
    for c0 in range(0, gw, stripe):
        cols = slice(c0, c0 + stripe)
        xs = _conv_silu(x_ref, carx_ref, cwx_ref, cbx_ref, cols)
        expanded = jnp.dot(pieces, e_ref[:, cols], preferred_element_type=F32)
        dt_exp = expanded[:l]
        dte_exp = expanded[l:2 * l]
        e3_exp = expanded[2 * l:]
        xt = xs * dt_exp
        xt_b = xt.astype(BF16)
        yield

        y_cols = []
        for v in range(stripe // LANES):
            x_pair = xt_b[:, v * LANES:(v + 1) * LANES]
            acc = jnp.zeros((l, LANES), F32)
            for s in range(heads_per_vreg):
                h = (c0 // LANES + v) * heads_per_vreg + s
                seg = jnp.minimum(a_cs[:, h:h + 1] - a_cst[h:h + 1, :], 0.0)
                m_h = (cb * jnp.exp2(seg)).astype(BF16)
                in_head = (lane >= s * head_dim) & (lane < (s + 1) * head_dim)
                x_h = jnp.where(in_head, x_pair, jnp.zeros_like(x_pair))
                acc = acc + jnp.dot(m_h, x_h, preferred_element_type=F32)
            y_cols.append(acc)
        y_diag = jnp.concatenate(y_cols, axis=1) if len(y_cols) > 1 else y_cols[0]
        yield

        state = state_ref[:, cols]
        y_off = jnp.dot(cm_b, state.astype(BF16), preferred_element_type=F32) * e3_exp
        xw_b = (xt * dte_exp).astype(BF16)
        new_states = jnp.dot(bmt_b, xw_b, preferred_element_type=F32)
        state_ref[:, cols] = state * e3_exp[l - 1:l, :] + new_states

        y = y_diag + y_off + xs * dexp_ref[:, cols]
        yz = y * _silu(z_ref[:, cols])
        ssq = ssq + jnp.sum(yz * yz, axis=-1, keepdims=True)
        yz_ref[:, cols] = yz
        yield

    inv_rms = lax.rsqrt(ssq * (1.0 / gw) + RMS_EPS)
    o_ref[...] = (yz_ref[...] * inv_rms * nw_ref[...]).astype(o_ref.dtype)


def ssd_core(zxbc, dt_raw, conv_w, conv_b, dt_bias, a_log, d_skip, norm_w):
    s = zxbc.shape[0]
    g, n, p, l = SSD_N_GROUPS, SSD_D_STATE, SSD_HEAD_DIM, SSD_CHUNK
    h = dt_raw.shape[1]
    r = h // g
    gw = r * p
    di = h * p
    l = min(l, s)
    assert gw % LANES == 0 and n % LANES == 0 and di == g * gw
    nx = di // gw
    nb0 = 2 * di // n
    nc0 = nb0 + g

    dtg = dt_raw.reshape(s, g, r).transpose(1, 0, 2)
    dtg3 = jnp.concatenate([dtg, dtg, dtg], axis=-1)
    dtt = dt_raw.reshape(s, g, r).transpose(1, 2, 0)
    kw = SSD_CONV_WIDTH
    bias = dt_bias.reshape(g, 1, 1, r)
    bias3 = jnp.concatenate([bias, bias, bias], axis=-1)
    alog = a_log.reshape(g, 1, 1, r)
    alog3 = jnp.concatenate([alog, alog, alog], axis=-1)
    biast = dt_bias.reshape(g, 1, r, 1)
    alogt = a_log.reshape(g, 1, r, 1)
    dexp = jnp.repeat(d_skip, p).reshape(g, 1, gw)
    nw = norm_w.reshape(g, 1, gw)
    per_group = lambda a, width: a.reshape(a.shape[0], g, width).transpose(1, 0, 2)
    cb_row = conv_b.reshape(1, -1)
    cwx, cbx = per_group(conv_w[:, :di], gw), per_group(cb_row[:, :di], gw)
    cwb, cbb = per_group(conv_w[:, di:di + g * n], n), per_group(cb_row[:, di:di + g * n], n)
    cwc, cbc = per_group(conv_w[:, di + g * n:], n), per_group(cb_row[:, di + g * n:], n)
    head_of_col = jnp.arange(gw) // p
    e1 = (head_of_col[None, :] == jnp.arange(r)[:, None]).astype(BF16)
    e3 = jnp.concatenate([e1, e1, e1], axis=0)
    params = (cwx, cbx, cwb, cbb, cwc, cbc, bias3, biast, alog3, alogt, dexp, nw, e3)
    assert len(params) == SSD_N_PARAMS

    groups = math.gcd(g, SSD_GROUPS_PER_STEP)

    def group_specs(e):
        grp = lambda gs: gs * groups + e
        specs = [
            pl.BlockSpec((l, gw), lambda gs, c: (c, grp(gs))),
            pl.BlockSpec((l, gw), lambda gs, c: (c, nx + grp(gs))),
            pl.BlockSpec((l, n), lambda gs, c: (c, nb0 + grp(gs))),
            pl.BlockSpec((l, n), lambda gs, c: (c, nc0 + grp(gs))),
            pl.BlockSpec((1, l, 3 * r), lambda gs, c: (grp(gs), c, 0)),
            pl.BlockSpec((1, r, l), lambda gs, c: (grp(gs), 0, c)),
        ]
        assert len(specs) == SSD_N_STREAMED
        return specs

    streamed = (zxbc, zxbc, zxbc, zxbc, dtg3, dtt)
    scratch = [pltpu.VMEM((n, gw), F32),
               pltpu.VMEM((SUBLANES + l, gw), F32),
               pltpu.VMEM((SUBLANES + l, n), F32),
               pltpu.VMEM((SUBLANES + l, n), F32),
               pltpu.VMEM((l, gw), F32)]
    assert len(scratch) == SSD_N_SCRATCH
    return pl.pallas_call(
        functools.partial(_ssd_kernel, n_heads=r, head_dim=p, groups=groups),
        grid=(g // groups, s // l),
        in_specs=([spec for e in range(groups) for spec in group_specs(e)]
                  + [pl.BlockSpec(memory_space=pltpu.VMEM)] * SSD_N_PARAMS),
        out_specs=pl.BlockSpec((l, groups * gw), lambda gs, c: (c, gs)),
        out_shape=jax.ShapeDtypeStruct((s, di), BF16),
        scratch_shapes=scratch * groups,
        compiler_params=_params("parallel", "arbitrary"),
        name="ssd_core",
    )(*(streamed * groups), *params)


def ssd_mixer(h, norm_w_in, w_in, conv_w, conv_b, dt_bias, a_log, d_skip, norm_w, w_out):
    d = h.shape[1]
    n_heads = dt_bias.shape[0]
    di = n_heads * SSD_HEAD_DIM
    n_main = 2 * di + 2 * SSD_N_GROUPS * SSD_D_STATE
    u = rmsnorm(h, norm_w_in)
    zxbc = matmul(u, w_in, k=d, n_out=n_main, tm=2048, tn=512, name="ssd_in_proj")
    dt_raw = matmul(u, w_in, k=d, n_out=n_heads, tm=2048, tn=LANES, w_col0=n_main,
                    name="ssd_dt_proj")
    y = ssd_core(zxbc, dt_raw, conv_w, conv_b, dt_bias, a_log, d_skip, norm_w)
    for kb in range(di // d):
        h = matmul(y, w_out, k=d, n_out=d, tm=2048, tn=512, a_kblk=kb, w_kblk=kb,
                   residual=h, name="ssd_out_proj")
    return h


def _pool_diff_kernel(v_ref, halo_ref, o_ref, *, group_dim, chunk):
    i = pl.program_id(0)
    tm = v_ref.shape[0]
    pos = i * tm + lax.broadcasted_iota(jnp.int32, (tm, chunk), 0)
    for g, window in enumerate(POOL_WINDOWS):
        inv_count = 1.0 / jnp.minimum(pos + 1, window).astype(F32)
        for c0 in range(g * group_dim, (g + 1) * group_dim, chunk):
            v = v_ref[:, c0:c0 + chunk]
            halo = jnp.where(i > 0, halo_ref[:, c0:c0 + chunk], 0.0)
            sw = jnp.concatenate([halo, v], axis=0)
            shift = 1
            while shift < window:
                sw = sw + pltpu.roll(sw, shift, 0)
                shift *= 2
            d = sw[POOL_HALO:] * inv_count - v
            o_ref[:, c0:c0 + chunk] = d.astype(o_ref.dtype)


def pool_diff(proj, width, group_dim, *, tm=256):
    s = proj.shape[0]
    tm = min(tm, s)
    chunk = min(512, group_dim)
    halo_blocks = tm // POOL_HALO
    return pl.pallas_call(
        functools.partial(_pool_diff_kernel, group_dim=group_dim, chunk=chunk),
        grid=(s // tm,),
        in_specs=[pl.BlockSpec((tm, width), lambda i: (i, 0)),
                  pl.BlockSpec((POOL_HALO, width),
                               lambda i: (jnp.maximum(i * halo_blocks - 1, 0), 0))],
        out_specs=pl.BlockSpec((tm, width), lambda i: (i, 0)),
        out_shape=jax.ShapeDtypeStruct((s, width), BF16),
        compiler_params=_params("parallel"),
        name="pool_diff",
    )(proj, proj)


def _pool_group_kernel(d_ref, w_ref, b_ref, sc_ref, z_ref, o_ref):
    w_b = w_ref[0].astype(BF16)
    bias = b_ref[0]
    scale = sc_ref[...]
    rows = d_ref.shape[0]
    rb = math.gcd(rows, MATMUL_ROW_BLOCK)
    for r0 in range(0, rows, rb):
        acc = jnp.dot(d_ref[r0:r0 + rb, :], w_b, preferred_element_type=F32)
        acc = (acc + bias) * scale
        o_ref[r0:r0 + rb, :] = (acc * _silu(z_ref[r0:r0 + rb, :])).astype(o_ref.dtype)


def pool_core(proj, w_grp, b_grp, scale, *, tm=4096, tn=512):
    s = proj.shape[0]
    ng, gd, _ = w_grp.shape
    assert ng == len(POOL_WINDOWS) and max(POOL_WINDOWS) <= POOL_HALO
    width = ng * gd
    d = pool_diff(proj, width, gd)
    tm = min(tm, s)
    tn = min(tn, gd)
    nj = gd // tn
    in_specs = [
        pl.BlockSpec((tm, gd), lambda g, i, j: (i, g), pipeline_mode=pl.Buffered(1)),
        pl.BlockSpec((1, gd, tn), lambda g, i, j: (g, 0, j)),
        pl.BlockSpec((1, 1, tn), lambda g, i, j: (g, 0, j)),
        pl.BlockSpec((1, tn), lambda g, i, j: (0, g * nj + j)),
        pl.BlockSpec((tm, tn), lambda g, i, j: (i, (width + g * gd) // tn + j)),
    ]
    return pl.pallas_call(
        _pool_group_kernel,
        grid=(ng, s // tm, nj),
        in_specs=in_specs,
        out_specs=pl.BlockSpec((tm, tn), lambda g, i, j: (i, g * nj + j)),
        out_shape=jax.ShapeDtypeStruct((s, width), BF16),
        compiler_params=_params("parallel", "parallel", "arbitrary"),
        name="pool_group",
    )(d, w_grp, b_grp.reshape(ng, 1, gd), scale.reshape(1, width), proj)


def pool_mixer(h, norm_w_in, w_in, w_grp, b_grp, scale, w_out):
    d = h.shape[1]
    width = scale.shape[0]
    u = rmsnorm(h, norm_w_in)
    proj = matmul(u, w_in, k=d, n_out=2 * width, tm=2048, tn=512, name="pool_in_proj")
    y = pool_core(proj, w_grp, b_grp, scale)
    for kb in range(width // d):
        h = matmul(y, w_out, k=d, n_out=d, tm=2048, tn=512, a_kblk=kb, w_kblk=kb,
                   residual=h, name="pool_out_proj")
    return h


def _moba_prep_kernel(q_ref, k_ref, v_ref, qw_ref, kw_ref, qs_ref, ka_ref, va_ref, km_ref):
    i = pl.program_id(0)

    @pl.when(i == 0)
    def _():
        km_ref[...] = jnp.zeros_like(km_ref)

    dh = MOBA_HEAD_DIM
    rows = q_ref.shape[0]
    n_heads = q_ref.shape[1] // dh
    scale = math.log2(math.e) / math.sqrt(dh)
    qw = qw_ref[...] * scale
    kw = kw_ref[...]
    lane = lax.broadcasted_iota(jnp.int32, (rows, dh), 1)
    onehot = jnp.where(lane == i, 1.0, 0.0).astype(ka_ref.dtype)
    ones = jnp.ones((rows, dh), va_ref.dtype)
    means = []
    for h in range(n_heads):
        sl = slice(h * dh, (h + 1) * dh)
        lo = slice(2 * h * dh, (2 * h + 1) * dh)
        hi = slice((2 * h + 1) * dh, (2 * h + 2) * dh)
        q = q_ref[:, sl]
        qs = q * lax.rsqrt(jnp.mean(q * q, axis=-1, keepdims=True) + RMS_EPS) * qw
        qs_ref[:, sl] = qs.astype(qs_ref.dtype)
        k = k_ref[:, sl]
        kn = k * lax.rsqrt(jnp.mean(k * k, axis=-1, keepdims=True) + RMS_EPS) * kw
        ka_ref[:, lo] = kn.astype(ka_ref.dtype)
        ka_ref[:, hi] = onehot
        va_ref[:, lo] = v_ref[:, sl].astype(va_ref.dtype)
        va_ref[:, hi] = ones
        means.append(jnp.mean(kn, axis=0, keepdims=True))
    mean_row = jnp.concatenate(means, axis=1)
    blk = lax.broadcasted_iota(jnp.int32, km_ref.shape, 0)
    km_ref[...] = jnp.where(blk == i, mean_row, km_ref[...])


def moba_prep(proj, q_norm_w, k_norm_w):
    s = proj.shape[0]
    width = proj.shape[1] // 4
    blk = MOBA_BLOCK
    nb = s // blk
    assert nb <= MOBA_HEAD_DIM
    row_spec = lambda c: pl.BlockSpec((blk, width), lambda i: (i, c))
    return pl.pallas_call(
        _moba_prep_kernel,
        grid=(nb,),
        in_specs=[row_spec(0), row_spec(1), row_spec(2),
                  pl.BlockSpec((1, MOBA_HEAD_DIM), lambda i: (0, 0)),
                  pl.BlockSpec((1, MOBA_HEAD_DIM), lambda i: (0, 0))],
        out_specs=[pl.BlockSpec((blk, width), lambda i: (i, 0)),
                   pl.BlockSpec((blk, 2 * width), lambda i: (i, 0)),
                   pl.BlockSpec((blk, 2 * width), lambda i: (i, 0)),
                   pl.BlockSpec((nb, width), lambda i: (0, 0))],
        out_shape=[jax.ShapeDtypeStruct((s, width), BF16),
                   jax.ShapeDtypeStruct((s, 2 * width), BF16),
                   jax.ShapeDtypeStruct((s, 2 * width), BF16),
                   jax.ShapeDtypeStruct((nb, width), F32)],
        compiler_params=_params("arbitrary"),
        name="moba_prep",
    )(proj, proj, proj, q_norm_w.reshape(1, -1), k_norm_w.reshape(1, -1))


def _moba_attn_kernel(qa_in_ref, qb_in_ref, k_ref, v_ref, km_ref, ga_ref, gb_ref, oa_ref, ob_ref,
                      *scratch, pairs):
    g = pl.program_id(1)
    tq = MOBA_BLOCK
    per_pair = len(scratch) // pairs
    stages = {}
    for e in range(pairs):
        a_rows = slice(e * tq, (e + 1) * tq)
        b_rows = slice((pairs - 1 - e) * tq, (pairs - e) * tq)
        stages[e] = _moba_pair(
            pairs * g + e, qa_in_ref.at[a_rows], qb_in_ref.at[b_rows], k_ref, v_ref, km_ref,
            ga_ref.at[a_rows], gb_ref.at[b_rows], oa_ref.at[a_rows], ob_ref.at[b_rows],
            *scratch[e * per_pair:(e + 1) * per_pair])
    tick = 0
    while stages:
        for e in sorted(stages):
            if tick >= e * MOBA_STAGE_LAG and next(stages[e], True):
                del stages[e]
        tick += 1


def _moba_pair(p, qa_in_ref, qb_in_ref, k_ref, v_ref, km_ref, ga_ref, gb_ref, oa_ref, ob_ref,
               s_ref, sstat_ref, sown_ref, qa_ref, mrun_ref, mstat_ref):
    blk = MOBA_BLOCK
    dh = MOBA_HEAD_DIM
    tq = qa_in_ref.shape[0]
    nb = km_ref.shape[0]
    n_dynamic = nb // 2 - 1
    nt = (((1,), (1,)), ((), ()))
    q_blocks = (p, nb - 1 - p)
    q2 = jnp.concatenate([qa_in_ref[...], qb_in_ref[...]], axis=0)

    gscore = lax.dot_general(km_ref[...].astype(BF16), q2, nt, preferred_element_type=F32)
    brow = lax.broadcasted_iota(jnp.int32, (nb, 2 * tq), 0)
    qcol = lax.broadcasted_iota(jnp.int32, (nb, 2 * tq), 1)
    browf = brow.astype(F32)
    own_blk = jnp.where(qcol < tq, q_blocks[0], q_blocks[1])
    gwork = jnp.where(brow < own_blk, gscore, -jnp.inf)
    sel = jnp.zeros((nb, 2 * tq), F32)
    for _ in range(min(MOBA_TOPK, nb)):
        top = jnp.max(gwork, axis=0, keepdims=True)
        is_top = (gwork == top) & (top > -jnp.inf)
        first = jnp.min(jnp.where(is_top, browf, float(nb)), axis=0, keepdims=True)
        pick = browf == first
        sel = jnp.where(pick, 1.0, sel)
        gwork = jnp.where(pick, -jnp.inf, gwork)
    bias_t = jnp.where(sel > 0.5, 0.0, MOBA_MASKED)
    if nb < dh:
        bias_t = jnp.concatenate([bias_t, jnp.zeros((dh - nb, 2 * tq), F32)], axis=0)
    q2a = jnp.concatenate([q2, bias_t.T.astype(BF16)], axis=1)
    yield

    def lane_max(s):
        m = s[:, :LANES]
        for c0 in range(LANES, blk, LANES):
            m = jnp.maximum(m, s[:, c0:c0 + LANES])
        return m

    r = lax.broadcasted_iota(jnp.int32, (tq, blk), 0)
    c = lax.broadcasted_iota(jnp.int32, (tq, blk), 1)
    own_start = []
    for t in range(2):
        qa_ref[t] = q2a[t * tq:(t + 1) * tq]
        start = pl.multiple_of(q_blocks[t] * blk, blk)
        own_start.append(start)
        s_own = lax.dot_general(q2[t * tq:(t + 1) * tq], k_ref[pl.ds(start, blk), pl.ds(0, dh)], nt,
                                preferred_element_type=F32)
        s_own = jnp.where(c <= r, s_own, MOBA_MASKED)
        sown_ref[t] = s_own
        mrun_ref[t] = lane_max(s_own)
    yield

    n_static = nb // 2
    sc_static = lax.dot_general(qa_ref[1], k_ref[pl.ds(0, n_static * blk), :], nt,
                                preferred_element_type=F32)
    sstat_ref[...] = sc_static
    m_a = mrun_ref[0]
    m_b = mrun_ref[1]
    for c0 in range(0, n_static * blk, LANES):
        m_b = jnp.maximum(m_b, sc_static[:, c0:c0 + LANES])
    m_b_static = jnp.broadcast_to(jnp.max(m_b, axis=1, keepdims=True), (tq, LANES))
    mstat_ref[...] = m_b_static
    yield

    def slot(u):
        is_a = u < p
        tile = jnp.where(is_a, 0, 1)
        kj = jnp.where(is_a, u, n_static + u - p)
        return is_a, tile, pl.multiple_of(kj * blk, blk)

    def probs(sc, m_lanes):
        return jnp.concatenate([jnp.exp2(sc[:, c0:c0 + LANES] - m_lanes)
                                for c0 in range(0, sc.shape[1], LANES)], axis=1).astype(BF16)

    def static_part():
        return (jnp.dot(probs(sown_ref[1], mstat_ref[...]), v_ref[pl.ds(own_start[1], blk), :],
                        preferred_element_type=F32)
                + jnp.dot(probs(sstat_ref[...], mstat_ref[...]), v_ref[pl.ds(0, n_static * blk), :],
                          preferred_element_type=F32))

    acc_b_static = None
    for u in range(n_dynamic):
        is_a, tile, start = slot(u)
        sc = lax.dot_general(qa_ref[tile], k_ref[pl.ds(start, blk), :], nt,
                             preferred_element_type=F32)
        s_ref[u] = sc
        sm = lane_max(sc)
        m_a = jnp.maximum(m_a, jnp.where(is_a, sm, MOBA_MASKED))
        m_b = jnp.maximum(m_b, jnp.where(is_a, MOBA_MASKED, sm))
        if u % MOBA_STAGE_SLOTS == MOBA_STAGE_SLOTS - 1:
            yield
            if acc_b_static is None:
                acc_b_static = static_part()
                yield
    if acc_b_static is None:
        acc_b_static = static_part()

    m_a = jnp.broadcast_to(jnp.max(m_a, axis=1, keepdims=True), (tq, LANES))
    m_b = jnp.maximum(jnp.broadcast_to(jnp.max(m_b, axis=1, keepdims=True), (tq, LANES)),
                      mstat_ref[...])
    mrun_ref[0] = m_a
    mrun_ref[1] = m_b
    yield

    rescale = jnp.exp2(mstat_ref[...] - mrun_ref[1])
    acc = [jnp.dot(probs(sown_ref[0], mrun_ref[0]), v_ref[pl.ds(own_start[0], blk), :],
                   preferred_element_type=F32),
           jnp.concatenate([acc_b_static[:, c0:c0 + LANES] * rescale
                            for c0 in range(0, 2 * dh, LANES)], axis=1)]
    yield
    for u in range(n_dynamic):
        is_a, tile, start = slot(u)
        part = jnp.dot(probs(s_ref[u], mrun_ref[tile]), v_ref[pl.ds(start, blk), :],
                       preferred_element_type=F32)
        zero = jnp.zeros_like(part)
        acc = [acc[0] + jnp.where(is_a, part, zero), acc[1] + jnp.where(is_a, zero, part)]
        if u % MOBA_STAGE_SLOTS == MOBA_STAGE_SLOTS - 1:
            yield

    for t, (g_ref, o_ref) in enumerate(((ga_ref, oa_ref), (gb_ref, ob_ref))):
        out = acc[t][:, :dh] / acc[t][:, dh:]
        o_ref[...] = (out * _silu(g_ref[...])).astype(o_ref.dtype)


def moba_attn(qs, ka, va, kmean, proj):
    s, width = qs.shape
    dh = MOBA_HEAD_DIM
    n_heads = width // dh
    blk = MOBA_BLOCK
    tq = blk
    nb = kmean.shape[0]
    assert nb % 2 == 0
    half = nb // 2
    pairs = math.gcd(half, MOBA_PAIRS)
    steps = half // pairs
    rows = pairs * tq
    gate0 = 3 * n_heads
    out_a, out_b = pl.pallas_call(
        functools.partial(_moba_attn_kernel, pairs=pairs),
        grid=(n_heads, steps),
        in_specs=[pl.BlockSpec((rows, dh), lambda h, g: (g, h)),
                  pl.BlockSpec((rows, dh), lambda h, g: (2 * steps - 1 - g, h)),
                  pl.BlockSpec((s, 2 * dh), lambda h, g: (0, h)),
                  pl.BlockSpec((s, 2 * dh), lambda h, g: (0, h)),
                  pl.BlockSpec((nb, dh), lambda h, g: (0, h)),
                  pl.BlockSpec((rows, dh), lambda h, g: (g, gate0 + h)),
                  pl.BlockSpec((rows, dh), lambda h, g: (2 * steps - 1 - g, gate0 + h))],
        out_specs=[pl.BlockSpec((rows, dh), lambda h, g: (g, h)),
                   pl.BlockSpec((rows, dh), lambda h, g: (steps - 1 - g, h))],
        out_shape=[jax.ShapeDtypeStruct((s // 2, width), BF16),
                   jax.ShapeDtypeStruct((s // 2, width), BF16)],
        scratch_shapes=[pltpu.VMEM((max(half - 1, 1), tq, blk), F32),
                        pltpu.VMEM((tq, half * blk), F32),
                        pltpu.VMEM((2, tq, blk), F32),
                        pltpu.VMEM((2, tq, 2 * dh), BF16),
                        pltpu.VMEM((2, tq, LANES), F32),
                        pltpu.VMEM((tq, LANES), F32)] * pairs,
        compiler_params=_params("parallel", "arbitrary"),
        name="moba_attn",
    )(qs, qs, ka, va, kmean, proj, proj)
    return jnp.concatenate([out_a, out_b], axis=0)


def moba_mixer(h, norm_w_in, w_in, q_norm_w, k_norm_w, w_out):
    s, d = h.shape
    assert s % MOBA_BLOCK == 0
    width = w_out.shape[0]
    u = rmsnorm(h, norm_w_in)
    proj = matmul(u, w_in, k=d, n_out=4 * width, tm=2048, tn=512, name="moba_in_proj")
    qs, ka, va, kmean = moba_prep(proj, q_norm_w, k_norm_w)
    y = moba_attn(qs, ka, va, kmean, proj)
    for kb in range(width // d):
        h = matmul(y, w_out, k=d, n_out=d, tm=2048, tn=512, a_kblk=kb, w_kblk=kb,
                   residual=h, name="moba_out_proj")
    return h


def kernel(x, norm0, ssd0_w_in, ssd0_conv_w, ssd0_conv_b, ssd0_dt_bias, ssd0_a_log, ssd0_d, ssd0_norm_w, ssd0_w_out, norm1, pool1_w_in, pool1_w_grp, pool1_b_grp, pool1_scale, pool1_w_out, norm2, moba2_w_in, moba2_q_norm, moba2_k_norm, moba2_w_out, norm3, ssd3_w_in, ssd3_conv_w, ssd3_conv_b, ssd3_dt_bias, ssd3_a_log, ssd3_d, ssd3_norm_w, ssd3_w_out):
    b, s, d = x.shape
    outs = []
    for bi in range(b):
        h = x[bi]
        h = ssd_mixer(h, norm0, ssd0_w_in, ssd0_conv_w, ssd0_conv_b, ssd0_dt_bias,
                      ssd0_a_log, ssd0_d, ssd0_norm_w, ssd0_w_out)
        h = pool_mixer(h, norm1, pool1_w_in, pool1_w_grp, pool1_b_grp, pool1_scale, pool1_w_out)
        h = moba_mixer(h, norm2, moba2_w_in, moba2_q_norm, moba2_k_norm, moba2_w_out)
        h = ssd_mixer(h, norm3, ssd3_w_in, ssd3_conv_w, ssd3_conv_b, ssd3_dt_bias,
                      ssd3_a_log, ssd3_d, ssd3_norm_w, ssd3_w_out)
        outs.append(h)
    return jnp.stack(outs, axis=0)
```

```python
import functools
import math

import jax
import jax.numpy as jnp
from jax import lax
from jax.experimental import pallas as pl
from jax.experimental.pallas import tpu as pltpu

RMS_EPS = 1e-6

SSD_HEAD_DIM = 64
SSD_N_GROUPS = 8
SSD_D_STATE = 128
SSD_CONV_WIDTH = 4
SSD_CHUNK = 128
SSD_STRIPE = 512
SSD_GROUPS_PER_STEP = 2

POOL_WINDOWS = (2, 4, 8, 16)
POOL_HALO = 16

MOBA_HEAD_DIM = 128
MOBA_BLOCK = 256
MOBA_TOPK = 3
MOBA_MASKED = -1e30
MOBA_PAIRS = 2
MOBA_STAGE_SLOTS = 4
MOBA_STAGE_LAG = 1

MATMUL_ROW_BLOCK = 256

LANES = 128
SUBLANES = 8
VMEM_LIMIT_BYTES = 56 * 1024 * 1024

BF16 = jnp.bfloat16
F32 = jnp.float32


def _params(*semantics):
    return pltpu.CompilerParams(dimension_semantics=semantics,
                                vmem_limit_bytes=VMEM_LIMIT_BYTES)


def _silu(x):
    half = 0.5 * x
    return half + half * jnp.tanh(half)


def _softplus(x):
    return jnp.maximum(x, 0.0) + jnp.log(1.0 + jnp.exp(-jnp.abs(x)))


def _split3(x):
    hi = x.astype(BF16)
    r1 = x - hi.astype(F32)
    mid = r1.astype(BF16)
    lo = (r1 - mid.astype(F32)).astype(BF16)
    return hi, mid, lo


def _rmsnorm_kernel(x_ref, w_ref, o_ref):
    x = x_ref[...]
    ms = jnp.mean(x * x, axis=-1, keepdims=True)
    o_ref[...] = (x * lax.rsqrt(ms + RMS_EPS) * w_ref[...]).astype(o_ref.dtype)


def rmsnorm(x, w, *, tm=256):
    m, d = x.shape
    tm = min(tm, m)
    return pl.pallas_call(
        _rmsnorm_kernel,
        grid=(m // tm,),
        in_specs=[pl.BlockSpec((tm, d), lambda i: (i, 0)),
                  pl.BlockSpec((1, d), lambda i: (0, 0))],
        out_specs=pl.BlockSpec((tm, d), lambda i: (i, 0)),
        out_shape=jax.ShapeDtypeStruct((m, d), BF16),
        compiler_params=_params("parallel"),
        name="rmsnorm",
    )(x, w.reshape(1, d))


def _matmul_kernel(*refs, has_residual):
    if has_residual:
        a_ref, w_ref, r_ref, o_ref = refs
    else:
        a_ref, w_ref, o_ref = refs
    w_b = w_ref[...].astype(BF16)
    if not has_residual:
        o_ref[...] = jnp.dot(a_ref[...], w_b, preferred_element_type=F32).astype(o_ref.dtype)
        return
    rows = a_ref.shape[0]
    rb = math.gcd(rows, MATMUL_ROW_BLOCK)
    for r0 in range(0, rows, rb):
        acc = jnp.dot(a_ref[r0:r0 + rb, :], w_b, preferred_element_type=F32)
        o_ref[r0:r0 + rb, :] = (acc + r_ref[r0:r0 + rb, :]).astype(o_ref.dtype)


def matmul(a, w, *, k, n_out, tm, tn, a_kblk=0, w_kblk=0, w_col0=0, residual=None,
           out_dtype=F32, name="matmul"):
    m = a.shape[0]
    tm = min(tm, m)
    tn = min(tn, n_out)
    assert m % tm == 0 and n_out % tn == 0 and w_col0 % tn == 0
    col0_blk = w_col0 // tn
    in_specs = [pl.BlockSpec((tm, k), lambda i, j: (i, a_kblk), pipeline_mode=pl.Buffered(1)),
                pl.BlockSpec((k, tn), lambda i, j: (w_kblk, j + col0_blk))]
    args = [a, w]
    if residual is not None:
        in_specs.append(pl.BlockSpec((tm, tn), lambda i, j: (i, j)))
        args.append(residual)
    return pl.pallas_call(
        functools.partial(_matmul_kernel, has_residual=residual is not None),
        grid=(m // tm, n_out // tn),
        in_specs=in_specs,
        out_specs=pl.BlockSpec((tm, tn), lambda i, j: (i, j)),
        out_shape=jax.ShapeDtypeStruct((m, n_out), out_dtype),
        compiler_params=_params("parallel", "arbitrary"),
        name=name,
    )(*args)


def _conv_silu(raw_ref, hist_ref, w_ref, b_ref, cols=slice(None)):
    l = raw_ref.shape[0]
    raw = raw_ref[:, cols]
    hist_ref[SUBLANES:, cols] = raw
    w = w_ref[:, cols]
    acc = b_ref[:, cols] + raw * w[SSD_CONV_WIDTH - 1:SSD_CONV_WIDTH, :]
    for j in range(1, SSD_CONV_WIDTH):
        acc = acc + hist_ref[pl.ds(SUBLANES - j, l), cols] * w[SSD_CONV_WIDTH - 1 - j:SSD_CONV_WIDTH - j, :]
    hist_ref[:SUBLANES, cols] = raw[l - SUBLANES:, :]
    return _silu(acc)


SSD_N_STREAMED = 6
SSD_N_PARAMS = 13
SSD_N_SCRATCH = 5


def _ssd_kernel(*refs, n_heads, head_dim, groups):
    n_in = SSD_N_STREAMED * groups
    params = refs[n_in:n_in + SSD_N_PARAMS]
    o_ref = refs[n_in + SSD_N_PARAMS]
    scratch = refs[n_in + SSD_N_PARAMS + 1:]
    gw = o_ref.shape[1] // groups
    chunk = pl.program_id(1)

    def group_params(e):
        grp = pl.program_id(0) * groups + e
        return [ref.at[grp] for ref in params[:-1]] + [params[-1]]

    @pl.when(chunk == 0)
    def _():
        for e in range(groups):
            state_ref, carx_ref, carb_ref, carc_ref, _ = scratch[e * SSD_N_SCRATCH:(e + 1) * SSD_N_SCRATCH]
            state_ref[...] = jnp.zeros_like(state_ref)
            for hist_ref in (carx_ref, carb_ref, carc_ref):
                hist_ref[:SUBLANES, :] = jnp.zeros((SUBLANES, hist_ref.shape[1]), F32)

    stages = [_ssd_group(*refs[e * SSD_N_STREAMED:(e + 1) * SSD_N_STREAMED], *group_params(e),
                         o_ref.at[:, e * gw:(e + 1) * gw],
                         *scratch[e * SSD_N_SCRATCH:(e + 1) * SSD_N_SCRATCH],
                         n_heads=n_heads, head_dim=head_dim) for e in range(groups)]
    while stages:
        stages = [stage for stage in stages if not next(stage, True)]


def _ssd_group(z_ref, x_ref, b_ref, c_ref, dt_ref, dtt_ref,
               cwx_ref, cbx_ref, cwb_ref, cbb_ref, cwc_ref, cbc_ref,
               bias_ref, biast_ref, alog_ref, alogt_ref, dexp_ref, nw_ref, e_ref,
               o_ref, state_ref, carx_ref, carb_ref, carc_ref, yz_ref, *, n_heads, head_dim):
    l = x_ref.shape[0]
    gw = x_ref.shape[1]
    bm = _conv_silu(b_ref, carb_ref, cwb_ref, cbb_ref)
    cm = _conv_silu(c_ref, carc_ref, cwc_ref, cbc_ref)

    log2e = math.log2(math.e)
    dt = _softplus(dt_ref[0] + bias_ref[0])
    a = dt * (-log2e * jnp.exp(alog_ref[0]))
    dtt = _softplus(dtt_ref[0] + biast_ref[0])
    at = dtt * (-log2e * jnp.exp(alogt_ref[0]))

    row = lax.broadcasted_iota(jnp.int32, (l, l), 0)
    col = lax.broadcasted_iota(jnp.int32, (l, l), 1)
    tril = row >= col
    tril_b = tril.astype(BF16)
    triu_b = (row <= col).astype(BF16)

    a_cs = sum(jnp.dot(tril_b, p, preferred_element_type=F32) for p in _split3(a))
    a_cst = sum(jnp.dot(p, triu_b, preferred_element_type=F32) for p in _split3(at))

    a_last = a_cs[l - 1:l, :]
    dte = jnp.exp2(a_last - a_cs)
    e3 = jnp.exp2(a_cs)

    stacked = jnp.concatenate([dt, dte, e3], axis=0)
    hi, mid, lo = _split3(stacked)
    grp = lax.broadcasted_iota(jnp.int32, stacked.shape, 1) // n_heads
    pieces = jnp.where(grp == 0, hi, jnp.where(grp == 1, mid, lo))

    cm_b = cm.astype(BF16)
    bmt_b = bm.T.astype(BF16)
    cb = lax.dot_general(cm_b, bm.astype(BF16), (((1,), (1,)), ((), ())),
                         preferred_element_type=F32)
    cb = jnp.where(tril, cb, 0.0)
    yield

    heads_per_vreg = LANES // head_dim
    lane = lax.broadcasted_iota(jnp.int32, (l, LANES), 1)
    stripe = min(SSD_STRIPE, gw)
    ssq = jnp.zeros((l, 1), F32)
    for c0 in range(0, gw, stripe):
        cols = slice(c0, c0 + stripe)
        xs = _conv_silu(x_ref, carx_ref, cwx_ref, cbx_ref, cols)
        expanded = jnp.dot(pieces, e_ref[:, cols], preferred_element_type=F32)
        dt_exp = expanded[:l]
        dte_exp = expanded[l:2 * l]
        e3_exp = expanded[2 * l:]
        xt = xs * dt_exp
        xt_b = xt.astype(BF16)
        yield

        y_cols = []
        for v in range(stripe // LANES):
            x_pair = xt_b[:, v * LANES:(v + 1) * LANES]
            acc = jnp.zeros((l, LANES), F32)
            for s in range(heads_per_vreg):
                h = (c0 // LANES + v) * heads_per_vreg + s
                seg = jnp.minimum(a_cs[:, h:h + 1] - a_cst[h:h + 1, :], 0.0)
                m_h = (cb * jnp.exp2(seg)).astype(BF16)
                in_head = (lane >= s * head_dim) & (lane < (s + 1) * head_dim)
                x_h = jnp.where(in_head, x_pair, jnp.zeros_like(x_pair))
                acc = acc + jnp.dot(m_h, x_h, preferred_element_type=F32)
            y_cols.append(acc)
        y_diag = jnp.concatenate(y_cols, axis=1) if len(y_cols) > 1 else y_cols[0]
        yield

        state = state_ref[:, cols]
        y_off = jnp.dot(cm_b, state.astype(BF16), preferred_element_type=F32) * e3_exp
        xw_b = (xt * dte_exp).astype(BF16)
        new_states = jnp.dot(bmt_b, xw_b, preferred_element_type=F32)
        state_ref[:, cols] = state * e3_exp[l - 1:l, :] + new_states

        y = y_diag + y_off + xs * dexp_ref[:, cols]
        yz = y * _silu(z_ref[:, cols])
        ssq = ssq + jnp.sum(yz * yz, axis=-1, keepdims=True)
        yz_ref[:, cols] = yz
        yield

    inv_rms = lax.rsqrt(ssq * (1.0 / gw) + RMS_EPS)
    o_ref[...] = (yz_ref[...] * inv_rms * nw_ref[...]).astype(o_ref.dtype)


def ssd_core(zxbc, dt_raw, conv_w, conv_b, dt_bias, a_log, d_skip, norm_w):
    s = zxbc.shape[0]
    g, n, p, l = SSD_N_GROUPS, SSD_D_STATE, SSD_HEAD_DIM, SSD_CHUNK
    h = dt_raw.shape[1]
    r = h // g
    gw = r * p
    di = h * p
    l = min(l, s)
    assert gw % LANES == 0 and n % LANES == 0 and di == g * gw
    nx = di // gw
    nb0 = 2 * di // n
    nc0 = nb0 + g

    dtg = dt_raw.reshape(s, g, r).transpose(1, 0, 2)
    dtg3 = jnp.concatenate([dtg, dtg, dtg], axis=-1)
    dtt = dt_raw.reshape(s, g, r).transpose(1, 2, 0)
    kw = SSD_CONV_WIDTH
    bias = dt_bias.reshape(g, 1, 1, r)
    bias3 = jnp.concatenate([bias, bias, bias], axis=-1)
    alog = a_log.reshape(g, 1, 1, r)
    alog3 = jnp.concatenate([alog, alog, alog], axis=-1)
    biast = dt_bias.reshape(g, 1, r, 1)
    alogt = a_log.reshape(g, 1, r, 1)
    dexp = jnp.repeat(d_skip, p).reshape(g, 1, gw)
    nw = norm_w.reshape(g, 1, gw)
    per_group = lambda a, width: a.reshape(a.shape[0], g, width).transpose(1, 0, 2)
    cb_row = conv_b.reshape(1, -1)
    cwx, cbx = per_group(conv_w[:, :di], gw), per_group(cb_row[:, :di], gw)
    cwb, cbb = per_group(conv_w[:, di:di + g * n], n), per_group(cb_row[:, di:di + g * n], n)
    cwc, cbc = per_group(conv_w[:, di + g * n:], n), per_group(cb_row[:, di + g * n:], n)
    head_of_col = jnp.arange(gw) // p
    e1 = (head_of_col[None, :] == jnp.arange(r)[:, None]).astype(BF16)
    e3 = jnp.concatenate([e1, e1, e1], axis=0)
    params = (cwx, cbx, cwb, cbb, cwc, cbc, bias3, biast, alog3, alogt, dexp, nw, e3)
    assert len(params) == SSD_N_PARAMS

    groups = math.gcd(g, SSD_GROUPS_PER_STEP)

    def group_specs(e):
        grp = lambda gs: gs * groups + e
        specs = [
            pl.BlockSpec((l, gw), lambda gs, c: (c, grp(gs))),
            pl.BlockSpec((l, gw), lambda gs, c: (c, nx + grp(gs))),
            pl.BlockSpec((l, n), lambda gs, c: (c, nb0 + grp(gs))),
            pl.BlockSpec((l, n), lambda gs, c: (c, nc0 + grp(gs))),
            pl.BlockSpec((1, l, 3 * r), lambda gs, c: (grp(gs), c, 0)),
            pl.BlockSpec((1, r, l), lambda gs, c: (grp(gs), 0, c)),
        ]
        assert len(specs) == SSD_N_STREAMED
        return specs

    streamed = (zxbc, zxbc, zxbc, zxbc, dtg3, dtt)
    scratch = [pltpu.VMEM((n, gw), F32),
               pltpu.VMEM((SUBLANES + l, gw), F32),
               pltpu.VMEM((SUBLANES + l, n), F32),
               pltpu.VMEM((SUBLANES + l, n), F32),
               pltpu.VMEM((l, gw), F32)]
    assert len(scratch) == SSD_N_SCRATCH
    return pl.pallas_call(
        functools.partial(_ssd_kernel, n_heads=r, head_dim=p, groups=groups),
        grid=(g // groups, s // l),
        in_specs=([spec for e in range(groups) for spec in group_specs(e)]
                  + [pl.BlockSpec(memory_space=pltpu.VMEM)] * SSD_N_PARAMS),
        out_specs=pl.BlockSpec((l, groups * gw), lambda gs, c: (c, gs)),
        out_shape=jax.ShapeDtypeStruct((s, di), BF16),
        scratch_shapes=scratch * groups,
        compiler_params=_params("parallel", "arbitrary"),
        name="ssd_core",
    )(*(streamed * groups), *params)


def ssd_mixer(h, norm_w_in, w_in, conv_w, conv_b, dt_bias, a_log, d_skip, norm_w, w_out):
    d = h.shape[1]
    n_heads = dt_bias.shape[0]
    di = n_heads * SSD_HEAD_DIM
    n_main = 2 * di + 2 * SSD_N_GROUPS * SSD_D_STATE
    u = rmsnorm(h, norm_w_in)
    zxbc = matmul(u, w_in, k=d, n_out=n_main, tm=2048, tn=512, name="ssd_in_proj")
    dt_raw = matmul(u, w_in, k=d, n_out=n_heads, tm=2048, tn=LANES, w_col0=n_main,
                    name="ssd_dt_proj")
    y = ssd_core(zxbc, dt_raw, conv_w, conv_b, dt_bias, a_log, d_skip, norm_w)
    for kb in range(di // d):
        h = matmul(y, w_out, k=d, n_out=d, tm=2048, tn=512, a_kblk=kb, w_kblk=kb,
                   residual=h, name="ssd_out_proj")
    return h


def _pool_diff_kernel(v_ref, halo_ref, o_ref, *, group_dim, chunk):
    i = pl.program_id(0)
    tm = v_ref.shape[0]
    pos = i * tm + lax.broadcasted_iota(jnp.int32, (tm, chunk), 0)
    for g, window in enumerate(POOL_WINDOWS):
        inv_count = 1.0 / jnp.minimum(pos + 1, window).astype(F32)
        for c0 in range(g * group_dim, (g + 1) * group_dim, chunk):
            v = v_ref[:, c0:c0 + chunk]
            halo = jnp.where(i > 0, halo_ref[:, c0:c0 + chunk], 0.0)
            sw = jnp.concatenate([halo, v], axis=0)
            shift = 1
            while shift < window:
                sw = sw + pltpu.roll(sw, shift, 0)
                shift *= 2
            d = sw[POOL_HALO:] * inv_count - v
            o_ref[:, c0:c0 + chunk] = d.astype(o_ref.dtype)


def pool_diff(proj, width, group_dim, *, tm=256):
    s = proj.shape[0]
    tm = min(tm, s)
    chunk = min(512, group_dim)
    halo_blocks = tm // POOL_HALO
    return pl.pallas_call(
        functools.partial(_pool_diff_kernel, group_dim=group_dim, chunk=chunk),
        grid=(s // tm,),
        in_specs=[pl.BlockSpec((tm, width), lambda i: (i, 0)),
                  pl.BlockSpec((POOL_HALO, width),
                               lambda i: (jnp.maximum(i * halo_blocks - 1, 0), 0))],
        out_specs=pl.BlockSpec((tm, width), lambda i: (i, 0)),
        out_shape=jax.ShapeDtypeStruct((s, width), BF16),
        compiler_params=_params("parallel"),
        name="pool_diff",
    )(proj, proj)


def _pool_group_kernel(d_ref, w_ref, b_ref, sc_ref, z_ref, o_ref):
    w_b = w_ref[0].astype(BF16)
    bias = b_ref[0]
    scale = sc_ref[...]
    rows = d_ref.shape[0]
    rb = math.gcd(rows, MATMUL_ROW_BLOCK)
    for r0 in range(0, rows, rb):
        acc = jnp.dot(d_ref[r0:r0 + rb, :], w_b, preferred_element_type=F32)
        acc = (acc + bias) * scale
        o_ref[r0:r0 + rb, :] = (acc * _silu(z_ref[r0:r0 + rb, :])).astype(o_ref.dtype)


def pool_core(proj, w_grp, b_grp, scale, *, tm=4096, tn=512):
    s = proj.shape[0]
    ng, gd, _ = w_grp.shape
    assert ng == len(POOL_WINDOWS) and max(POOL_WINDOWS) <= POOL_HALO
    width = ng * gd
    d = pool_diff(proj, width, gd)
    tm = min(tm, s)
    tn = min(tn, gd)
    nj = gd // tn
    in_specs = [
        pl.BlockSpec((tm, gd), lambda g, i, j: (i, g), pipeline_mode=pl.Buffered(1)),
        pl.BlockSpec((1, gd, tn), lambda g, i, j: (g, 0, j)),
        pl.BlockSpec((1, 1, tn), lambda g, i, j: (g, 0, j)),
        pl.BlockSpec((1, tn), lambda g, i, j: (0, g * nj + j)),
        pl.BlockSpec((tm, tn), lambda g, i, j: (i, (width + g * gd) // tn + j)),
    ]
    return pl.pallas_call(
        _pool_group_kernel,
        grid=(ng, s // tm, nj),
        in_specs=in_specs,
        out_specs=pl.BlockSpec((tm, tn), lambda g, i, j: (i, g * nj + j)),
        out_shape=jax.ShapeDtypeStruct((s, width), BF16),
        compiler_params=_params("parallel", "parallel", "arbitrary"),
        name="pool_group",
    )(d, w_grp, b_grp.reshape(ng, 1, gd), scale.reshape(1, width), proj)


def pool_mixer(h, norm_w_in, w_in, w_grp, b_grp, scale, w_out):
    d = h.shape[1]
    width = scale.shape[0]
    u = rmsnorm(h, norm_w_in)
    proj = matmul(u, w_in, k=d, n_out=2 * width, tm=2048, tn=512, name="pool_in_proj")
    y = pool_core(proj, w_grp, b_grp, scale)
    for kb in range(width // d):
        h = matmul(y, w_out, k=d, n_out=d, tm=2048, tn=512, a_kblk=kb, w_kblk=kb,
                   residual=h, name="pool_out_proj")
    return h


def _moba_prep_kernel(q_ref, k_ref, v_ref, qw_ref, kw_ref, qs_ref, ka_ref, va_ref, km_ref):
    i = pl.program_id(0)

    @pl.when(i == 0)
    def _():
        km_ref[...] = jnp.zeros_like(km_ref)

    dh = MOBA_HEAD_DIM
    rows = q_ref.shape[0]
    n_heads = q_ref.shape[1] // dh
    scale = math.log2(math.e) / math.sqrt(dh)
    qw = qw_ref[...] * scale
    kw = kw_ref[...]
    lane = lax.broadcasted_iota(jnp.int32, (rows, dh), 1)
    onehot = jnp.where(lane == i, 1.0, 0.0).astype(ka_ref.dtype)
    ones = jnp.ones((rows, dh), va_ref.dtype)
    means = []
    for h in range(n_heads):
        sl = slice(h * dh, (h + 1) * dh)
        lo = slice(2 * h * dh, (2 * h + 1) * dh)
        hi = slice((2 * h + 1) * dh, (2 * h + 2) * dh)
        q = q_ref[:, sl].astype(F32)
        qs = q * lax.rsqrt(jnp.mean(q * q, axis=-1, keepdims=True) + RMS_EPS) * qw
        qs_ref[:, sl] = qs.astype(qs_ref.dtype)
        k = k_ref[:, sl].astype(F32)
        kn = k * lax.rsqrt(jnp.mean(k * k, axis=-1, keepdims=True) + RMS_EPS) * kw
        ka_ref[:, lo] = kn.astype(ka_ref.dtype)
        ka_ref[:, hi] = onehot
        va_ref[:, lo] = v_ref[:, sl].astype(va_ref.dtype)
        va_ref[:, hi] = ones
        means.append(jnp.mean(kn, axis=0, keepdims=True))
    mean_row = jnp.concatenate(means, axis=1)
    blk = lax.broadcasted_iota(jnp.int32, km_ref.shape, 0)
    km_ref[...] = jnp.where(blk == i, mean_row, km_ref[...])


def moba_prep(proj, q_norm_w, k_norm_w):
    s = proj.shape[0]
    width = proj.shape[1] // 4
    blk = MOBA_BLOCK
    nb = s // blk
    assert nb <= MOBA_HEAD_DIM
    row_spec = lambda c: pl.BlockSpec((blk, width), lambda i: (i, c))
    return pl.pallas_call(
        _moba_prep_kernel,
        grid=(nb,),
        in_specs=[row_spec(0), row_spec(1), row_spec(2),
                  pl.BlockSpec((1, MOBA_HEAD_DIM), lambda i: (0, 0)),
                  pl.BlockSpec((1, MOBA_HEAD_DIM), lambda i: (0, 0))],
        out_specs=[pl.BlockSpec((blk, width), lambda i: (i, 0)),
                   pl.BlockSpec((blk, 2 * width), lambda i: (i, 0)),
                   pl.BlockSpec((blk, 2 * width), lambda i: (i, 0)),
                   pl.BlockSpec((nb, width), lambda i: (0, 0))],
        out_shape=[jax.ShapeDtypeStruct((s, width), BF16),
                   jax.ShapeDtypeStruct((s, 2 * width), BF16),
                   jax.ShapeDtypeStruct((s, 2 * width), BF16),
                   jax.ShapeDtypeStruct((nb, width), F32)],
        compiler_params=_params("arbitrary"),
        name="moba_prep",
    )(proj, proj, proj, q_norm_w.reshape(1, -1), k_norm_w.reshape(1, -1))


def _moba_attn_kernel(qa_in_ref, qb_in_ref, k_ref, v_ref, km_ref, ga_ref, gb_ref, oa_ref, ob_ref,
                      *scratch, pairs):
    g = pl.program_id(1)
    tq = MOBA_BLOCK
    per_pair = len(scratch) // pairs
    stages = {}
    for e in range(pairs):
        a_rows = slice(e * tq, (e + 1) * tq)
        b_rows = slice((pairs - 1 - e) * tq, (pairs - e) * tq)
        stages[e] = _moba_pair(
            pairs * g + e, qa_in_ref.at[a_rows], qb_in_ref.at[b_rows], k_ref, v_ref, km_ref,
            ga_ref.at[a_rows], gb_ref.at[b_rows], oa_ref.at[a_rows], ob_ref.at[b_rows],
            *scratch[e * per_pair:(e + 1) * per_pair])
    tick = 0
    while stages:
        for e in sorted(stages):
            if tick >= e * MOBA_STAGE_LAG and next(stages[e], True):
                del stages[e]
        tick += 1


def _moba_pair(p, qa_in_ref, qb_in_ref, k_ref, v_ref, km_ref, ga_ref, gb_ref, oa_ref, ob_ref,
               s_ref, sstat_ref, sown_ref, qa_ref, mrun_ref, mstat_ref):
    blk = MOBA_BLOCK
    dh = MOBA_HEAD_DIM
    tq = qa_in_ref.shape[0]
    nb = km_ref.shape[0]
    n_dynamic = nb // 2 - 1
    nt = (((1,), (1,)), ((), ()))
    q_blocks = (p, nb - 1 - p)
    q2 = jnp.concatenate([qa_in_ref[...], qb_in_ref[...]], axis=0)

    gscore = lax.dot_general(km_ref[...].astype(BF16), q2, nt, preferred_element_type=F32)
    brow = lax.broadcasted_iota(jnp.int32, (nb, 2 * tq), 0)
    qcol = lax.broadcasted_iota(jnp.int32, (nb, 2 * tq), 1)
    browf = brow.astype(F32)
    own_blk = jnp.where(qcol < tq, q_blocks[0], q_blocks[1])
    gwork = jnp.where(brow < own_blk, gscore, -jnp.inf)
    sel = jnp.zeros((nb, 2 * tq), F32)
    for _ in range(min(MOBA_TOPK, nb)):
        top = jnp.max(gwork, axis=0, keepdims=True)
        is_top = (gwork == top) & (top > -jnp.inf)
        first = jnp.min(jnp.where(is_top, browf, float(nb)), axis=0, keepdims=True)
        pick = browf == first
        sel = jnp.where(pick, 1.0, sel)
        gwork = jnp.where(pick, -jnp.inf, gwork)
    bias_t = jnp.where(sel > 0.5, 0.0, MOBA_MASKED)
    if nb < dh:
        bias_t = jnp.concatenate([bias_t, jnp.zeros((dh - nb, 2 * tq), F32)], axis=0)
    q2a = jnp.concatenate([q2, bias_t.T.astype(BF16)], axis=1)
    yield

    def lane_max(s):
        m = s[:, :LANES]
        for c0 in range(LANES, blk, LANES):
            m = jnp.maximum(m, s[:, c0:c0 + LANES])
        return m

    r = lax.broadcasted_iota(jnp.int32, (tq, blk), 0)
    c = lax.broadcasted_iota(jnp.int32, (tq, blk), 1)
    own_start = []
    for t in range(2):
        qa_ref[t] = q2a[t * tq:(t + 1) * tq]
        start = pl.multiple_of(q_blocks[t] * blk, blk)
        own_start.append(start)
        s_own = lax.dot_general(q2[t * tq:(t + 1) * tq], k_ref[pl.ds(start, blk), pl.ds(0, dh)], nt,
                                preferred_element_type=F32)
        s_own = jnp.where(c <= r, s_own, MOBA_MASKED)
        sown_ref[t] = s_own
        mrun_ref[t] = lane_max(s_own)
    yield

    n_static = nb // 2
    sc_static = lax.dot_general(qa_ref[1], k_ref[pl.ds(0, n_static * blk), :], nt,
                                preferred_element_type=F32)
    sstat_ref[...] = sc_static
    m_a = mrun_ref[0]
    m_b = mrun_ref[1]
    for c0 in range(0, n_static * blk, LANES):
        m_b = jnp.maximum(m_b, sc_static[:, c0:c0 + LANES])
    m_b_static = jnp.broadcast_to(jnp.max(m_b, axis=1, keepdims=True), (tq, LANES))
    mstat_ref[...] = m_b_static
    yield

    def slot(u):
        is_a = u < p
        tile = jnp.where(is_a, 0, 1)
        kj = jnp.where(is_a, u, n_static + u - p)
        return is_a, tile, pl.multiple_of(kj * blk, blk)

    def probs(sc, m_lanes):
        return jnp.concatenate([jnp.exp2(sc[:, c0:c0 + LANES] - m_lanes)
                                for c0 in range(0, sc.shape[1], LANES)], axis=1).astype(BF16)

    def static_part():
        return (jnp.dot(probs(sown_ref[1], mstat_ref[...]), v_ref[pl.ds(own_start[1], blk), :],
                        preferred_element_type=F32)
                + jnp.dot(probs(sstat_ref[...], mstat_ref[...]), v_ref[pl.ds(0, n_static * blk), :],
                          preferred_element_type=F32))

    acc_b_static = None
    for u in range(n_dynamic):
        is_a, tile, start = slot(u)
        sc = lax.dot_general(qa_ref[tile], k_ref[pl.ds(start, blk), :], nt,
                             preferred_element_type=F32)
        s_ref[u] = sc
        sm = lane_max(sc)
        m_a = jnp.maximum(m_a, jnp.where(is_a, sm, MOBA_MASKED))
        m_b = jnp.maximum(m_b, jnp.where(is_a, MOBA_MASKED, sm))
        if u % MOBA_STAGE_SLOTS == MOBA_STAGE_SLOTS - 1:
            yield
            if acc_b_static is None:
                acc_b_static = static_part()
                yield
    if acc_b_static is None:
        acc_b_static = static_part()

    m_a = jnp.broadcast_to(jnp.max(m_a, axis=1, keepdims=True), (tq, LANES))
    m_b = jnp.maximum(jnp.broadcast_to(jnp.max(m_b, axis=1, keepdims=True), (tq, LANES)),
                      mstat_ref[...])
    mrun_ref[0] = m_a
    mrun_ref[1] = m_b
    yield

    rescale = jnp.exp2(mstat_ref[...] - mrun_ref[1])
    acc = [jnp.dot(probs(sown_ref[0], mrun_ref[0]), v_ref[pl.ds(own_start[0], blk), :],
                   preferred_element_type=F32),
           jnp.concatenate([acc_b_static[:, c0:c0 + LANES] * rescale
                            for c0 in range(0, 2 * dh, LANES)], axis=1)]
    yield
    for u in range(n_dynamic):
        is_a, tile, start = slot(u)
        part = jnp.dot(probs(s_ref[u], mrun_ref[tile]), v_ref[pl.ds(start, blk), :],
                       preferred_element_type=F32)
        zero = jnp.zeros_like(part)
        acc = [acc[0] + jnp.where(is_a, part, zero), acc[1] + jnp.where(is_a, zero, part)]
        if u % MOBA_STAGE_SLOTS == MOBA_STAGE_SLOTS - 1:
            yield

    for t, (g_ref, o_ref) in enumerate(((ga_ref, oa_ref), (gb_ref, ob_ref))):
        out = acc[t][:, :dh] / acc[t][:, dh:]
        o_ref[...] = (out * _silu(g_ref[...].astype(F32))).astype(o_ref.dtype)


def moba_attn(qs, ka, va, kmean, proj):
    s, width = qs.shape
    dh = MOBA_HEAD_DIM
    n_heads = width // dh
    blk = MOBA_BLOCK
    tq = blk
    nb = kmean.shape[0]
    assert nb % 2 == 0
    half = nb // 2
    pairs = math.gcd(half, MOBA_PAIRS)
    steps = half // pairs
    rows = pairs * tq
    gate0 = 3 * n_heads
    out_a, out_b = pl.pallas_call(
        functools.partial(_moba_attn_kernel, pairs=pairs),
        grid=(n_heads, steps),
        in_specs=[pl.BlockSpec((rows, dh), lambda h, g: (g, h)),
                  pl.BlockSpec((rows, dh), lambda h, g: (2 * steps - 1 - g, h)),
                  pl.BlockSpec((s, 2 * dh), lambda h, g: (0, h)),
                  pl.BlockSpec((s, 2 * dh), lambda h, g: (0, h)),
                  pl.BlockSpec((nb, dh), lambda h, g: (0, h)),
                  pl.BlockSpec((rows, dh), lambda h, g: (g, gate0 + h)),
                  pl.BlockSpec((rows, dh), lambda h, g: (2 * steps - 1 - g, gate0 + h))],
        out_specs=[pl.BlockSpec((rows, dh), lambda h, g: (g, h)),
                   pl.BlockSpec((rows, dh), lambda h, g: (steps - 1 - g, h))],
        out_shape=[jax.ShapeDtypeStruct((s // 2, width), BF16),
                   jax.ShapeDtypeStruct((s // 2, width), BF16)],
        scratch_shapes=[pltpu.VMEM((max(half - 1, 1), tq, blk), F32),
                        pltpu.VMEM((tq, half * blk), F32),
                        pltpu.VMEM((2, tq, blk), F32),
                        pltpu.VMEM((2, tq, 2 * dh), BF16),
                        pltpu.VMEM((2, tq, LANES), F32),
                        pltpu.VMEM((tq, LANES), F32)] * pairs,
        compiler_params=_params("parallel", "arbitrary"),
        name="moba_attn",
    )(qs, qs, ka, va, kmean, proj, proj)
    return jnp.concatenate([out_a, out_b], axis=0)


def moba_mixer(h, norm_w_in, w_in, q_norm_w, k_norm_w, w_out):
    s, d = h.shape
    assert s % MOBA_BLOCK == 0
    width = w_out.shape[0]
    u = rmsnorm(h, norm_w_in)
    proj = matmul(u, w_in, k=d, n_out=4 * width, tm=2048, tn=512, out_dtype=BF16,
                  name="moba_in_proj")
    qs, ka, va, kmean = moba_prep(proj, q_norm_w, k_norm_w)
    y = moba_attn(qs, ka, va, kmean, proj)
    for kb in range(width // d):
        h = matmul(y, w_out, k=d, n_out=d, tm=2048, tn=512, a_kblk=kb, w_kblk=kb,
                   residual=h, name="moba_out_proj")
    return h


def kernel(x, norm0, ssd0_w_in, ssd0_conv_w, ssd0_conv_b, ssd0_dt_bias, ssd0_a_log, ssd0_d, ssd0_norm_w, ssd0_w_out, norm1, pool1_w_in, pool1_w_grp, pool1_b_grp, pool1_scale, pool1_w_out, norm2, moba2_w_in, moba2_q_norm, moba2_k_norm, moba2_w_out, norm3, ssd3_w_in, ssd3_conv_w, ssd3_conv_b, ssd3_dt_bias, ssd3_a_log, ssd3_d, ssd3_norm_w, ssd3_w_out):
    b, s, d = x.shape
    outs = []
    for bi in range(b):
        h = x[bi]
        h = ssd_mixer(h, norm0, ssd0_w_in, ssd0_conv_w, ssd0_conv_b, ssd0_dt_bias,
                      ssd0_a_log, ssd0_d, ssd0_norm_w, ssd0_w_out)
        h = pool_mixer(h, norm1, pool1_w_in, pool1_w_grp, pool1_b_grp, pool1_scale, pool1_w_out)
        h = moba_mixer(h, norm2, moba2_w_in, moba2_q_norm, moba2_k_norm, moba2_w_out)
        h = ssd_mixer(h, norm3, ssd3_w_in, ssd3_conv_w, ssd3_conv_b, ssd3_dt_bias,
                      ssd3_a_log, ssd3_d, ssd3_norm_w, ssd3_w_out)
        outs.append(h)
    return jnp.stack(outs, axis=0)
```

```python
import functools
import math

import jax
import jax.numpy as jnp
from jax import lax
from jax.experimental import pallas as pl
from jax.experimental.pallas import tpu as pltpu

RMS_EPS = 1e-6

SSD_HEAD_DIM = 64
SSD_N_GROUPS = 8
SSD_D_STATE = 128
SSD_CONV_WIDTH = 4
SSD_CHUNK = 128
SSD_STRIPE = 512
SSD_GROUPS_PER_STEP = 2

POOL_WINDOWS = (2, 4, 8, 16)
POOL_HALO = 16

MOBA_HEAD_DIM = 128
MOBA_BLOCK = 256
MOBA_TOPK = 3
MOBA_MASKED = -1e30
MOBA_PAIRS = 2
MOBA_STAGE_SLOTS = 4
MOBA_STAGE_LAG = 1

PROJ_ROW_TILE = 2048
PROJ_COL_TILE = 512
MATMUL_ROW_BLOCK = 256
ROWWISE_TILE = 512

LANES = 128
SUBLANES = 8
VMEM_LIMIT_BYTES = 56 * 1024 * 1024

BF16 = jnp.bfloat16
F32 = jnp.float32


def _params(*semantics):
    return pltpu.CompilerParams(dimension_semantics=semantics,
                                vmem_limit_bytes=VMEM_LIMIT_BYTES)


def _silu(x):
    half = 0.5 * x
    return half + half * jnp.tanh(half)


def _softplus(x):
    return jnp.maximum(x, 0.0) + jnp.log(1.0 + jnp.exp(-jnp.abs(x)))


def _split3(x):
    hi = x.astype(BF16)
    r1 = x - hi.astype(F32)
    mid = r1.astype(BF16)
    lo = (r1 - mid.astype(F32)).astype(BF16)
    return hi, mid, lo


def _rmsnorm_kernel(x_ref, w_ref, o_ref):
    x = x_ref[...]
    ms = jnp.mean(x * x, axis=-1, keepdims=True)
    o_ref[...] = (x * lax.rsqrt(ms + RMS_EPS) * w_ref[...]).astype(o_ref.dtype)


def rmsnorm(x, w, *, tm=ROWWISE_TILE):
    m, d = x.shape
    tm = min(tm, m)
    return pl.pallas_call(
        _rmsnorm_kernel,
        grid=(m // tm,),
        in_specs=[pl.BlockSpec((tm, d), lambda i: (i, 0)),
                  pl.BlockSpec((1, d), lambda i: (0, 0))],
        out_specs=pl.BlockSpec((tm, d), lambda i: (i, 0)),
        out_shape=jax.ShapeDtypeStruct((m, d), BF16),
        compiler_params=_params("parallel"),
        name="rmsnorm",
    )(x, w.reshape(1, d))


def _matmul_kernel(*refs, has_residual):
    if has_residual:
        a_ref, w_ref, r_ref, o_ref = refs
    else:
        a_ref, w_ref, o_ref = refs
    w_b = w_ref[...].astype(BF16)
    if not has_residual:
        o_ref[...] = jnp.dot(a_ref[...], w_b, preferred_element_type=F32).astype(o_ref.dtype)
        return
    rows = a_ref.shape[0]
    rb = math.gcd(rows, MATMUL_ROW_BLOCK)
    for r0 in range(0, rows, rb):
        acc = jnp.dot(a_ref[r0:r0 + rb, :], w_b, preferred_element_type=F32)
        o_ref[r0:r0 + rb, :] = (acc + r_ref[r0:r0 + rb, :]).astype(o_ref.dtype)


def matmul(a, w, *, k, n_out, tm=PROJ_ROW_TILE, tn=PROJ_COL_TILE, a_kblk=0, w_kblk=0, w_col0=0,
           residual=None, out_dtype=F32, name="matmul"):
    m = a.shape[0]
    tm = min(tm, m)
    tn = min(tn, n_out)
    assert m % tm == 0 and n_out % tn == 0 and w_col0 % tn == 0
    col0_blk = w_col0 // tn
    in_specs = [pl.BlockSpec((tm, k), lambda i, j: (i, a_kblk), pipeline_mode=pl.Buffered(1)),
                pl.BlockSpec((k, tn), lambda i, j: (w_kblk, j + col0_blk))]
    args = [a, w]
    if residual is not None:
        in_specs.append(pl.BlockSpec((tm, tn), lambda i, j: (i, j)))
        args.append(residual)
    return pl.pallas_call(
        functools.partial(_matmul_kernel, has_residual=residual is not None),
        grid=(m // tm, n_out // tn),
        in_specs=in_specs,
        out_specs=pl.BlockSpec((tm, tn), lambda i, j: (i, j)),
        out_shape=jax.ShapeDtypeStruct((m, n_out), out_dtype),
        compiler_params=_params("parallel", "arbitrary"),
        name=name,
    )(*args)


def _conv_silu(raw_ref, hist_ref, w_ref, b_ref, cols=slice(None)):
    l = raw_ref.shape[0]
    raw = raw_ref[:, cols]
    hist_ref[SUBLANES:, cols] = raw
    w = w_ref[:, cols]
    acc = b_ref[:, cols] + raw * w[SSD_CONV_WIDTH - 1:SSD_CONV_WIDTH, :]
    for j in range(1, SSD_CONV_WIDTH):
        acc = acc + hist_ref[pl.ds(SUBLANES - j, l), cols] * w[SSD_CONV_WIDTH - 1 - j:SSD_CONV_WIDTH - j, :]
    hist_ref[:SUBLANES, cols] = raw[l - SUBLANES:, :]
    return _silu(acc)


SSD_N_STREAMED = 6
SSD_N_PARAMS = 13
SSD_N_SCRATCH = 5


def _ssd_kernel(*refs, n_heads, head_dim, groups):
    n_in = SSD_N_STREAMED * groups
    params = refs[n_in:n_in + SSD_N_PARAMS]
    o_ref = refs[n_in + SSD_N_PARAMS]
    scratch = refs[n_in + SSD_N_PARAMS + 1:]
    gw = o_ref.shape[1] // groups
    chunk = pl.program_id(1)

    def group_params(e):
        grp = pl.program_id(0) * groups + e
        return [ref.at[grp] for ref in params[:-1]] + [params[-1]]

    @pl.when(chunk == 0)
    def _():
        for e in range(groups):
            state_ref, carx_ref, carb_ref, carc_ref, _ = scratch[e * SSD_N_SCRATCH:(e + 1) * SSD_N_SCRATCH]
            state_ref[...] = jnp.zeros_like(state_ref)
            for hist_ref in (carx_ref, carb_ref, carc_ref):
                hist_ref[:SUBLANES, :] = jnp.zeros((SUBLANES, hist_ref.shape[1]), F32)

    stages = [_ssd_group(*refs[e * SSD_N_STREAMED:(e + 1) * SSD_N_STREAMED], *group_params(e),
                         o_ref.at[:, e * gw:(e + 1) * gw],
                         *scratch[e * SSD_N_SCRATCH:(e + 1) * SSD_N_SCRATCH],
                         n_heads=n_heads, head_dim=head_dim) for e in range(groups)]
    while stages:
        stages = [stage for stage in stages if not next(stage, True)]


def _ssd_group(z_ref, x_ref, b_ref, c_ref, dt_ref, dtt_ref,
               cwx_ref, cbx_ref, cwb_ref, cbb_ref, cwc_ref, cbc_ref,
               bias_ref, biast_ref, alog_ref, alogt_ref, dexp_ref, nw_ref, e_ref,
               o_ref, state_ref, carx_ref, carb_ref, carc_ref, yz_ref, *, n_heads, head_dim):
    l = x_ref.shape[0]
    gw = x_ref.shape[1]
    bm = _conv_silu(b_ref, carb_ref, cwb_ref, cbb_ref)
    cm = _conv_silu(c_ref, carc_ref, cwc_ref, cbc_ref)

    log2e = math.log2(math.e)
    dt = _softplus(dt_ref[0] + bias_ref[0])
    a = dt * (-log2e * jnp.exp(alog_ref[0]))
    dtt = _softplus(dtt_ref[0] + biast_ref[0])
    at = dtt * (-log2e * jnp.exp(alogt_ref[0]))

    row = lax.broadcasted_iota(jnp.int32, (l, l), 0)
    col = lax.broadcasted_iota(jnp.int32, (l, l), 1)
    tril = row >= col
    tril_b = tril.astype(BF16)
    triu_b = (row <= col).astype(BF16)

    a_cs = sum(jnp.dot(tril_b, p, preferred_element_type=F32) for p in _split3(a))
    a_cst = sum(jnp.dot(p, triu_b, preferred_element_type=F32) for p in _split3(at))

    a_last = a_cs[l - 1:l, :]
    dte = jnp.exp2(a_last - a_cs)
    e3 = jnp.exp2(a_cs)

    stacked = jnp.concatenate([dt, dte, e3], axis=0)
    hi, mid, lo = _split3(stacked)
    grp = lax.broadcasted_iota(jnp.int32, stacked.shape, 1) // n_heads
    pieces = jnp.where(grp == 0, hi, jnp.where(grp == 1, mid, lo))

    cm_b = cm.astype(BF16)
    bmt_b = bm.T.astype(BF16)
    cb = lax.dot_general(cm_b, bm.astype(BF16), (((1,), (1,)), ((), ())),
                         preferred_element_type=F32)
    cb = jnp.where(tril, cb, 0.0)
    yield

    heads_per_vreg = LANES // head_dim
    lane = lax.broadcasted_iota(jnp.int32, (l, LANES), 1)
    stripe = min(SSD_STRIPE, gw)
    ssq = jnp.zeros((l, 1), F32)
    for c0 in range(0, gw, stripe):
        cols = slice(c0, c0 + stripe)
        xs = _conv_silu(x_ref, carx_ref, cwx_ref, cbx_ref, cols)
        expanded = jnp.dot(pieces, e_ref[:, cols], preferred_element_type=F32)
        dt_exp = expanded[:l]
        dte_exp = expanded[l:2 * l]
        e3_exp = expanded[2 * l:]
        xt = xs * dt_exp
        xt_b = xt.astype(BF16)
        yield

        y_cols = []
        for v in range(stripe // LANES):
            x_pair = xt_b[:, v * LANES:(v + 1) * LANES]
            acc = jnp.zeros((l, LANES), F32)
            for s in range(heads_per_vreg):
                h = (c0 // LANES + v) * heads_per_vreg + s
                seg = jnp.minimum(a_cs[:, h:h + 1] - a_cst[h:h + 1, :], 0.0)
                m_h = (cb * jnp.exp2(seg)).astype(BF16)
                in_head = (lane >= s * head_dim) & (lane < (s + 1) * head_dim)
                x_h = jnp.where(in_head, x_pair, jnp.zeros_like(x_pair))
                acc = acc + jnp.dot(m_h, x_h, preferred_element_type=F32)
            y_cols.append(acc)
        y_diag = jnp.concatenate(y_cols, axis=1) if len(y_cols) > 1 else y_cols[0]
        yield

        state = state_ref[:, cols]
        y_off = jnp.dot(cm_b, state.astype(BF16), preferred_element_type=F32) * e3_exp
        xw_b = (xt * dte_exp).astype(BF16)
        new_states = jnp.dot(bmt_b, xw_b, preferred_element_type=F32)
        state_ref[:, cols] = state * e3_exp[l - 1:l, :] + new_states

        y = y_diag + y_off + xs * dexp_ref[:, cols]
        yz = y * _silu(z_ref[:, cols])
        ssq = ssq + jnp.sum(yz * yz, axis=-1, keepdims=True)
        yz_ref[:, cols] = yz
        yield

    inv_rms = lax.rsqrt(ssq * (1.0 / gw) + RMS_EPS)
    o_ref[...] = (yz_ref[...] * inv_rms * nw_ref[...]).astype(o_ref.dtype)


def ssd_core(zxbc, dt_raw, conv_w, conv_b, dt_bias, a_log, d_skip, norm_w):
    s = zxbc.shape[0]
    g, n, p, l = SSD_N_GROUPS, SSD_D_STATE, SSD_HEAD_DIM, SSD_CHUNK
    h = dt_raw.shape[1]
    r = h // g
    gw = r * p
    di = h * p
    l = min(l, s)
    assert gw % LANES == 0 and n % LANES == 0 and di == g * gw
    nx = di // gw
    nb0 = 2 * di // n
    nc0 = nb0 + g

    dtg = dt_raw.reshape(s, g, r).transpose(1, 0, 2)
    dtg3 = jnp.concatenate([dtg, dtg, dtg], axis=-1)
    dtt = dt_raw.reshape(s, g, r).transpose(1, 2, 0)
    kw = SSD_CONV_WIDTH
    bias = dt_bias.reshape(g, 1, 1, r)
    bias3 = jnp.concatenate([bias, bias, bias], axis=-1)
    alog = a_log.reshape(g, 1, 1, r)
    alog3 = jnp.concatenate([alog, alog, alog], axis=-1)
    biast = dt_bias.reshape(g, 1, r, 1)
    alogt = a_log.reshape(g, 1, r, 1)
    dexp = jnp.repeat(d_skip, p).reshape(g, 1, gw)
    nw = norm_w.reshape(g, 1, gw)
    per_group = lambda a, width: a.reshape(a.shape[0], g, width).transpose(1, 0, 2)
    cb_row = conv_b.reshape(1, -1)
    cwx, cbx = per_group(conv_w[:, :di], gw), per_group(cb_row[:, :di], gw)
    cwb, cbb = per_group(conv_w[:, di:di + g * n], n), per_group(cb_row[:, di:di + g * n], n)
    cwc, cbc = per_group(conv_w[:, di + g * n:], n), per_group(cb_row[:, di + g * n:], n)
    head_of_col = jnp.arange(gw) // p
    e1 = (head_of_col[None, :] == jnp.arange(r)[:, None]).astype(BF16)
    e3 = jnp.concatenate([e1, e1, e1], axis=0)
    params = (cwx, cbx, cwb, cbb, cwc, cbc, bias3, biast, alog3, alogt, dexp, nw, e3)
    assert len(params) == SSD_N_PARAMS

    groups = math.gcd(g, SSD_GROUPS_PER_STEP)

    def group_specs(e):
        grp = lambda gs: gs * groups + e
        specs = [
            pl.BlockSpec((l, gw), lambda gs, c: (c, grp(gs))),
            pl.BlockSpec((l, gw), lambda gs, c: (c, nx + grp(gs))),
            pl.BlockSpec((l, n), lambda gs, c: (c, nb0 + grp(gs))),
            pl.BlockSpec((l, n), lambda gs, c: (c, nc0 + grp(gs))),
            pl.BlockSpec((1, l, 3 * r), lambda gs, c: (grp(gs), c, 0)),
            pl.BlockSpec((1, r, l), lambda gs, c: (grp(gs), 0, c)),
        ]
        assert len(specs) == SSD_N_STREAMED
        return specs

    streamed = (zxbc, zxbc, zxbc, zxbc, dtg3, dtt)
    scratch = [pltpu.VMEM((n, gw), F32),
               pltpu.VMEM((SUBLANES + l, gw), F32),
               pltpu.VMEM((SUBLANES + l, n), F32),
               pltpu.VMEM((SUBLANES + l, n), F32),
               pltpu.VMEM((l, gw), F32)]
    assert len(scratch) == SSD_N_SCRATCH
    return pl.pallas_call(
        functools.partial(_ssd_kernel, n_heads=r, head_dim=p, groups=groups),
        grid=(g // groups, s // l),
        in_specs=([spec for e in range(groups) for spec in group_specs(e)]
                  + [pl.BlockSpec(memory_space=pltpu.VMEM)] * SSD_N_PARAMS),
        out_specs=pl.BlockSpec((l, groups * gw), lambda gs, c: (c, gs)),
        out_shape=jax.ShapeDtypeStruct((s, di), BF16),
        scratch_shapes=scratch * groups,
        compiler_params=_params("parallel", "arbitrary"),
        name="ssd_core",
    )(*(streamed * groups), *params)


def ssd_mixer(h, norm_w_in, w_in, conv_w, conv_b, dt_bias, a_log, d_skip, norm_w, w_out):
    d = h.shape[1]
    n_heads = dt_bias.shape[0]
    di = n_heads * SSD_HEAD_DIM
    n_main = 2 * di + 2 * SSD_N_GROUPS * SSD_D_STATE
    u = rmsnorm(h, norm_w_in)
    zxbc = matmul(u, w_in, k=d, n_out=n_main, name="ssd_in_proj")
    dt_raw = matmul(u, w_in, k=d, n_out=n_heads, tn=LANES, w_col0=n_main,
                    name="ssd_dt_proj")
    y = ssd_core(zxbc, dt_raw, conv_w, conv_b, dt_bias, a_log, d_skip, norm_w)
    for kb in range(di // d):
        h = matmul(y, w_out, k=d, n_out=d, a_kblk=kb, w_kblk=kb,
                   residual=h, name="ssd_out_proj")
    return h


def _pool_diff_kernel(v_ref, halo_ref, o_ref, *, group_dim, chunk):
    i = pl.program_id(0)
    tm = v_ref.shape[0]
    pos = i * tm + lax.broadcasted_iota(jnp.int32, (tm, chunk), 0)
    for g, window in enumerate(POOL_WINDOWS):
        inv_count = 1.0 / jnp.minimum(pos + 1, window).astype(F32)
        for c0 in range(g * group_dim, (g + 1) * group_dim, chunk):
            v = v_ref[:, c0:c0 + chunk]
            halo = jnp.where(i > 0, halo_ref[:, c0:c0 + chunk], 0.0)
            sw = jnp.concatenate([halo, v], axis=0)
            shift = 1
            while shift < window:
                sw = sw + pltpu.roll(sw, shift, 0)
                shift *= 2
            d = sw[POOL_HALO:] * inv_count - v
            o_ref[:, c0:c0 + chunk] = d.astype(o_ref.dtype)


def pool_diff(proj, width, group_dim, *, tm=256):
    s = proj.shape[0]
    tm = min(tm, s)
    chunk = min(512, group_dim)
    halo_blocks = tm // POOL_HALO
    return pl.pallas_call(
        functools.partial(_pool_diff_kernel, group_dim=group_dim, chunk=chunk),
        grid=(s // tm,),
        in_specs=[pl.BlockSpec((tm, width), lambda i: (i, 0)),
                  pl.BlockSpec((POOL_HALO, width),
                               lambda i: (jnp.maximum(i * halo_blocks - 1, 0), 0))],
        out_specs=pl.BlockSpec((tm, width), lambda i: (i, 0)),
        out_shape=jax.ShapeDtypeStruct((s, width), BF16),
        compiler_params=_params("parallel"),
        name="pool_diff",
    )(proj, proj)


def _pool_group_kernel(d_ref, w_ref, b_ref, sc_ref, z_ref, o_ref):
    w_b = w_ref[0].astype(BF16)
    bias = b_ref[0]
    scale = sc_ref[...]
    rows = d_ref.shape[0]
    rb = math.gcd(rows, MATMUL_ROW_BLOCK)
    for r0 in range(0, rows, rb):
        acc = jnp.dot(d_ref[r0:r0 + rb, :], w_b, preferred_element_type=F32)
        acc = (acc + bias) * scale
        o_ref[r0:r0 + rb, :] = (acc * _silu(z_ref[r0:r0 + rb, :])).astype(o_ref.dtype)


def pool_core(proj, w_grp, b_grp, scale, *, tm=4096, tn=512):
    s = proj.shape[0]
    ng, gd, _ = w_grp.shape
    assert ng == len(POOL_WINDOWS) and max(POOL_WINDOWS) <= POOL_HALO
    width = ng * gd
    d = pool_diff(proj, width, gd)
    tm = min(tm, s)
    tn = min(tn, gd)
    nj = gd // tn
    in_specs = [
        pl.BlockSpec((tm, gd), lambda g, i, j: (i, g), pipeline_mode=pl.Buffered(1)),
        pl.BlockSpec((1, gd, tn), lambda g, i, j: (g, 0, j)),
        pl.BlockSpec((1, 1, tn), lambda g, i, j: (g, 0, j)),
        pl.BlockSpec((1, tn), lambda g, i, j: (0, g * nj + j)),
        pl.BlockSpec((tm, tn), lambda g, i, j: (i, (width + g * gd) // tn + j)),
    ]
    return pl.pallas_call(
        _pool_group_kernel,
        grid=(ng, s // tm, nj),
        in_specs=in_specs,
        out_specs=pl.BlockSpec((tm, tn), lambda g, i, j: (i, g * nj + j)),
        out_shape=jax.ShapeDtypeStruct((s, width), BF16),
        compiler_params=_params("parallel", "parallel", "arbitrary"),
        name="pool_group",
    )(d, w_grp, b_grp.reshape(ng, 1, gd), scale.reshape(1, width), proj)


def pool_mixer(h, norm_w_in, w_in, w_grp, b_grp, scale, w_out):
    d = h.shape[1]
    width = scale.shape[0]
    u = rmsnorm(h, norm_w_in)
    proj = matmul(u, w_in, k=d, n_out=2 * width, name="pool_in_proj")
    y = pool_core(proj, w_grp, b_grp, scale)
    for kb in range(width // d):
        h = matmul(y, w_out, k=d, n_out=d, a_kblk=kb, w_kblk=kb,
                   residual=h, name="pool_out_proj")
    return h


def _moba_prep_kernel(q_ref, k_ref, v_ref, qw_ref, kw_ref, qs_ref, ka_ref, va_ref, km_ref):
    i = pl.program_id(0)

    @pl.when(i == 0)
    def _():
        km_ref[...] = jnp.zeros_like(km_ref)

    dh = MOBA_HEAD_DIM
    rows = q_ref.shape[0]
    n_heads = q_ref.shape[1] // dh
    scale = math.log2(math.e) / math.sqrt(dh)
    qw = qw_ref[...] * scale
    kw = kw_ref[...]
    lane = lax.broadcasted_iota(jnp.int32, (rows, dh), 1)
    onehot = jnp.where(lane == i, 1.0, 0.0).astype(ka_ref.dtype)
    ones = jnp.ones((rows, dh), va_ref.dtype)
    means = []
    for h in range(n_heads):
        sl = slice(h * dh, (h + 1) * dh)
        lo = slice(2 * h * dh, (2 * h + 1) * dh)
        hi = slice((2 * h + 1) * dh, (2 * h + 2) * dh)
        q = q_ref[:, sl].astype(F32)
        qs = q * lax.rsqrt(jnp.mean(q * q, axis=-1, keepdims=True) + RMS_EPS) * qw
        qs_ref[:, sl] = qs.astype(qs_ref.dtype)
        k = k_ref[:, sl].astype(F32)
        kn = k * lax.rsqrt(jnp.mean(k * k, axis=-1, keepdims=True) + RMS_EPS) * kw
        ka_ref[:, lo] = kn.astype(ka_ref.dtype)
        ka_ref[:, hi] = onehot
        va_ref[:, lo] = v_ref[:, sl].astype(va_ref.dtype)
        va_ref[:, hi] = ones
        means.append(jnp.mean(kn, axis=0, keepdims=True))
    mean_row = jnp.concatenate(means, axis=1)
    blk = lax.broadcasted_iota(jnp.int32, km_ref.shape, 0)
    km_ref[...] = jnp.where(blk == i, mean_row, km_ref[...])


def moba_prep(proj, q_norm_w, k_norm_w):
    s = proj.shape[0]
    width = proj.shape[1] // 4
    blk = MOBA_BLOCK
    nb = s // blk
    assert nb <= MOBA_HEAD_DIM
    row_spec = lambda c: pl.BlockSpec((blk, width), lambda i: (i, c))
    return pl.pallas_call(
        _moba_prep_kernel,
        grid=(nb,),
        in_specs=[row_spec(0), row_spec(1), row_spec(2),
                  pl.BlockSpec((1, MOBA_HEAD_DIM), lambda i: (0, 0)),
                  pl.BlockSpec((1, MOBA_HEAD_DIM), lambda i: (0, 0))],
        out_specs=[pl.BlockSpec((blk, width), lambda i: (i, 0)),
                   pl.BlockSpec((blk, 2 * width), lambda i: (i, 0)),
                   pl.BlockSpec((blk, 2 * width), lambda i: (i, 0)),
                   pl.BlockSpec((nb, width), lambda i: (0, 0))],
        out_shape=[jax.ShapeDtypeStruct((s, width), BF16),
                   jax.ShapeDtypeStruct((s, 2 * width), BF16),
                   jax.ShapeDtypeStruct((s, 2 * width), BF16),
                   jax.ShapeDtypeStruct((nb, width), F32)],
        compiler_params=_params("arbitrary"),
        name="moba_prep",
    )(proj, proj, proj, q_norm_w.reshape(1, -1), k_norm_w.reshape(1, -1))


def _moba_attn_kernel(qa_in_ref, qb_in_ref, k_ref, v_ref, km_ref, ga_ref, gb_ref, oa_ref, ob_ref,
                      *scratch, pairs):
    g = pl.program_id(1)
    tq = MOBA_BLOCK
    per_pair = len(scratch) // pairs
    stages = {}
    for e in range(pairs):
        a_rows = slice(e * tq, (e + 1) * tq)
        b_rows = slice((pairs - 1 - e) * tq, (pairs - e) * tq)
        stages[e] = _moba_pair(
            pairs * g + e, qa_in_ref.at[a_rows], qb_in_ref.at[b_rows], k_ref, v_ref, km_ref,
            ga_ref.at[a_rows], gb_ref.at[b_rows], oa_ref.at[a_rows], ob_ref.at[b_rows],
            *scratch[e * per_pair:(e + 1) * per_pair])
    tick = 0
    while stages:
        for e in sorted(stages):
            if tick >= e * MOBA_STAGE_LAG and next(stages[e], True):
                del stages[e]
        tick += 1


def _moba_pair(p, qa_in_ref, qb_in_ref, k_ref, v_ref, km_ref, ga_ref, gb_ref, oa_ref, ob_ref,
               s_ref, sstat_ref, sown_ref, qa_ref, mrun_ref, mstat_ref):
    blk = MOBA_BLOCK
    dh = MOBA_HEAD_DIM
    tq = qa_in_ref.shape[0]
    nb = km_ref.shape[0]
    n_dynamic = nb // 2 - 1
    nt = (((1,), (1,)), ((), ()))
    q_blocks = (p, nb - 1 - p)
    q2 = jnp.concatenate([qa_in_ref[...], qb_in_ref[...]], axis=0)

    gscore = lax.dot_general(km_ref[...].astype(BF16), q2, nt, preferred_element_type=F32)
    brow = lax.broadcasted_iota(jnp.int32, (nb, 2 * tq), 0)
    qcol = lax.broadcasted_iota(jnp.int32, (nb, 2 * tq), 1)
    browf = brow.astype(F32)
    own_blk = jnp.where(qcol < tq, q_blocks[0], q_blocks[1])
    gwork = jnp.where(brow < own_blk, gscore, -jnp.inf)
    sel = jnp.zeros((nb, 2 * tq), F32)
    for _ in range(min(MOBA_TOPK, nb)):
        top = jnp.max(gwork, axis=0, keepdims=True)
        is_top = (gwork == top) & (top > -jnp.inf)
        first = jnp.min(jnp.where(is_top, browf, float(nb)), axis=0, keepdims=True)
        pick = browf == first
        sel = jnp.where(pick, 1.0, sel)
        gwork = jnp.where(pick, -jnp.inf, gwork)
    bias_t = jnp.where(sel > 0.5, 0.0, MOBA_MASKED)
    if nb < dh:
        bias_t = jnp.concatenate([bias_t, jnp.zeros((dh - nb, 2 * tq), F32)], axis=0)
    q2a = jnp.concatenate([q2, bias_t.T.astype(BF16)], axis=1)
    yield

    def lane_max(s):
        m = s[:, :LANES]
        for c0 in range(LANES, blk, LANES):
            m = jnp.maximum(m, s[:, c0:c0 + LANES])
        return m

    r = lax.broadcasted_iota(jnp.int32, (tq, blk), 0)
    c = lax.broadcasted_iota(jnp.int32, (tq, blk), 1)
    own_start = []
    for t in range(2):
        qa_ref[t] = q2a[t * tq:(t + 1) * tq]
        start = pl.multiple_of(q_blocks[t] * blk, blk)
        own_start.append(start)
        s_own = lax.dot_general(q2[t * tq:(t + 1) * tq], k_ref[pl.ds(start, blk), pl.ds(0, dh)], nt,
                                preferred_element_type=F32)
        s_own = jnp.where(c <= r, s_own, MOBA_MASKED)
        sown_ref[t] = s_own
        mrun_ref[t] = lane_max(s_own)
    yield

    n_static = nb // 2
    sc_static = lax.dot_general(qa_ref[1], k_ref[pl.ds(0, n_static * blk), :], nt,
                                preferred_element_type=F32)
    sstat_ref[...] = sc_static
    m_a = mrun_ref[0]
    m_b = mrun_ref[1]
    for c0 in range(0, n_static * blk, LANES):
        m_b = jnp.maximum(m_b, sc_static[:, c0:c0 + LANES])
    m_b_static = jnp.broadcast_to(jnp.max(m_b, axis=1, keepdims=True), (tq, LANES))
    mstat_ref[...] = m_b_static
    yield

    def slot(u):
        is_a = u < p
        tile = jnp.where(is_a, 0, 1)
        kj = jnp.where(is_a, u, n_static + u - p)
        return is_a, tile, pl.multiple_of(kj * blk, blk)

    def probs(sc, m_lanes):
        return jnp.concatenate([jnp.exp2(sc[:, c0:c0 + LANES] - m_lanes)
                                for c0 in range(0, sc.shape[1], LANES)], axis=1).astype(BF16)

    def static_part():
        return (jnp.dot(probs(sown_ref[1], mstat_ref[...]), v_ref[pl.ds(own_start[1], blk), :],
                        preferred_element_type=F32)
                + jnp.dot(probs(sstat_ref[...], mstat_ref[...]), v_ref[pl.ds(0, n_static * blk), :],
                          preferred_element_type=F32))

    acc_b_static = None
    for u in range(n_dynamic):
        is_a, tile, start = slot(u)
        sc = lax.dot_general(qa_ref[tile], k_ref[pl.ds(start, blk), :], nt,
                             preferred_element_type=F32)
        s_ref[u] = sc
        sm = lane_max(sc)
        m_a = jnp.maximum(m_a, jnp.where(is_a, sm, MOBA_MASKED))
        m_b = jnp.maximum(m_b, jnp.where(is_a, MOBA_MASKED, sm))
        if u % MOBA_STAGE_SLOTS == MOBA_STAGE_SLOTS - 1:
            yield
            if acc_b_static is None:
                acc_b_static = static_part()
                yield
    if acc_b_static is None:
        acc_b_static = static_part()

    m_a = jnp.broadcast_to(jnp.max(m_a, axis=1, keepdims=True), (tq, LANES))
    m_b = jnp.maximum(jnp.broadcast_to(jnp.max(m_b, axis=1, keepdims=True), (tq, LANES)),
                      mstat_ref[...])
    mrun_ref[0] = m_a
    mrun_ref[1] = m_b
    yield

    rescale = jnp.exp2(mstat_ref[...] - mrun_ref[1])
    acc = [jnp.dot(probs(sown_ref[0], mrun_ref[0]), v_ref[pl.ds(own_start[0], blk), :],
                   preferred_element_type=F32),
           jnp.concatenate([acc_b_static[:, c0:c0 + LANES] * rescale
                            for c0 in range(0, 2 * dh, LANES)], axis=1)]
    yield
    for u in range(n_dynamic):
        is_a, tile, start = slot(u)
        part = jnp.dot(probs(s_ref[u], mrun_ref[tile]), v_ref[pl.ds(start, blk), :],
                       preferred_element_type=F32)
        zero = jnp.zeros_like(part)
        acc = [acc[0] + jnp.where(is_a, part, zero), acc[1] + jnp.where(is_a, zero, part)]
        if u % MOBA_STAGE_SLOTS == MOBA_STAGE_SLOTS - 1:
            yield

    for t, (g_ref, o_ref) in enumerate(((ga_ref, oa_ref), (gb_ref, ob_ref))):
        out = acc[t][:, :dh] / acc[t][:, dh:]
        o_ref[...] = (out * _silu(g_ref[...].astype(F32))).astype(o_ref.dtype)


def moba_attn(qs, ka, va, kmean, proj):
    s, width = qs.shape
    dh = MOBA_HEAD_DIM
    n_heads = width // dh
    blk = MOBA_BLOCK
    tq = blk
    nb = kmean.shape[0]
    assert nb % 2 == 0
    half = nb // 2
    pairs = math.gcd(half, MOBA_PAIRS)
    steps = half // pairs
    rows = pairs * tq
    gate0 = 3 * n_heads
    out_a, out_b = pl.pallas_call(
        functools.partial(_moba_attn_kernel, pairs=pairs),
        grid=(n_heads, steps),
        in_specs=[pl.BlockSpec((rows, dh), lambda h, g: (g, h)),
                  pl.BlockSpec((rows, dh), lambda h, g: (2 * steps - 1 - g, h)),
                  pl.BlockSpec((s, 2 * dh), lambda h, g: (0, h)),
                  pl.BlockSpec((s, 2 * dh), lambda h, g: (0, h)),
                  pl.BlockSpec((nb, dh), lambda h, g: (0, h)),
                  pl.BlockSpec((rows, dh), lambda h, g: (g, gate0 + h)),
                  pl.BlockSpec((rows, dh), lambda h, g: (2 * steps - 1 - g, gate0 + h))],
        out_specs=[pl.BlockSpec((rows, dh), lambda h, g: (g, h)),
                   pl.BlockSpec((rows, dh), lambda h, g: (steps - 1 - g, h))],
        out_shape=[jax.ShapeDtypeStruct((s // 2, width), BF16),
                   jax.ShapeDtypeStruct((s // 2, width), BF16)],
        scratch_shapes=[pltpu.VMEM((max(half - 1, 1), tq, blk), F32),
                        pltpu.VMEM((tq, half * blk), F32),
                        pltpu.VMEM((2, tq, blk), F32),
                        pltpu.VMEM((2, tq, 2 * dh), BF16),
                        pltpu.VMEM((2, tq, LANES), F32),
                        pltpu.VMEM((tq, LANES), F32)] * pairs,
        compiler_params=_params("parallel", "arbitrary"),
        name="moba_attn",
    )(qs, qs, ka, va, kmean, proj, proj)
    return jnp.concatenate([out_a, out_b], axis=0)


def moba_mixer(h, norm_w_in, w_in, q_norm_w, k_norm_w, w_out):
    s, d = h.shape
    assert s % MOBA_BLOCK == 0
    width = w_out.shape[0]
    u = rmsnorm(h, norm_w_in)
    proj = matmul(u, w_in, k=d, n_out=4 * width, out_dtype=BF16,
                  name="moba_in_proj")
    qs, ka, va, kmean = moba_prep(proj, q_norm_w, k_norm_w)
    y = moba_attn(qs, ka, va, kmean, proj)
    for kb in range(width // d):
        h = matmul(y, w_out, k=d, n_out=d, a_kblk=kb, w_kblk=kb,
                   residual=h, name="moba_out_proj")
    return h


def kernel(x, norm0, ssd0_w_in, ssd0_conv_w, ssd0_conv_b, ssd0_dt_bias, ssd0_a_log, ssd0_d, ssd0_norm_w, ssd0_w_out, norm1, pool1_w_in, pool1_w_grp, pool1_b_grp, pool1_scale, pool1_w_out, norm2, moba2_w_in, moba2_q_norm, moba2_k_norm, moba2_w_out, norm3, ssd3_w_in, ssd3_conv_w, ssd3_conv_b, ssd3_dt_bias, ssd3_a_log, ssd3_d, ssd3_norm_w, ssd3_w_out):
    b, s, d = x.shape
    outs = []
    for bi in range(b):
        h = x[bi]
        h = ssd_mixer(h, norm0, ssd0_w_in, ssd0_conv_w, ssd0_conv_b, ssd0_dt_bias,
                      ssd0_a_log, ssd0_d, ssd0_norm_w, ssd0_w_out)
        h = pool_mixer(h, norm1, pool1_w_in, pool1_w_grp, pool1_b_grp, pool1_scale, pool1_w_out)
        h = moba_mixer(h, norm2, moba2_w_in, moba2_q_norm, moba2_k_norm, moba2_w_out)
        h = ssd_mixer(h, norm3, ssd3_w_in, ssd3_conv_w, ssd3_conv_b, ssd3_dt_bias,
                      ssd3_a_log, ssd3_d, ssd3_norm_w, ssd3_w_out)
        outs.append(h)
    return jnp.stack(outs, axis=0)
```

```python
import functools
import math

import jax
import jax.numpy as jnp
from jax import lax
from jax.experimental import pallas as pl
from jax.experimental.pallas import tpu as pltpu

RMS_EPS = 1e-6

SSD_HEAD_DIM = 64
SSD_N_GROUPS = 8
SSD_D_STATE = 128
SSD_CONV_WIDTH = 4
SSD_CHUNK = 128
SSD_STRIPE = 512
SSD_GROUPS_PER_STEP = 2

POOL_WINDOWS = (2, 4, 8, 16)
POOL_HALO = 16

MOBA_HEAD_DIM = 128
MOBA_BLOCK = 256
MOBA_TOPK = 3
MOBA_MASKED = -1e30
MOBA_PAIRS = 2
MOBA_STAGE_SLOTS = 4
MOBA_STAGE_LAG = 1

PROJ_ROW_TILE = 2048
PROJ_COL_TILE = 512
MATMUL_ROW_BLOCK = 256
ROWWISE_TILE = 512

LANES = 128
SUBLANES = 8
VMEM_LIMIT_BYTES = 56 * 1024 * 1024

BF16 = jnp.bfloat16
F32 = jnp.float32


def _params(*semantics):
    return pltpu.CompilerParams(dimension_semantics=semantics,
                                vmem_limit_bytes=VMEM_LIMIT_BYTES)


def _silu(x):
    half = 0.5 * x
    return half + half * jnp.tanh(half)


def _softplus(x):
    return jnp.maximum(x, 0.0) + jnp.log(1.0 + jnp.exp(-jnp.abs(x)))


def _split3(x):
    hi = x.astype(BF16)
    r1 = x - hi.astype(F32)
    mid = r1.astype(BF16)
    lo = (r1 - mid.astype(F32)).astype(BF16)
    return hi, mid, lo


def _rmsnorm_kernel(x_ref, w_ref, o_ref):
    x = x_ref[...]
    ms = jnp.mean(x * x, axis=-1, keepdims=True)
    o_ref[...] = (x * lax.rsqrt(ms + RMS_EPS) * w_ref[...]).astype(o_ref.dtype)


def rmsnorm(x, w, *, tm=ROWWISE_TILE):
    m, d = x.shape
    tm = min(tm, m)
    return pl.pallas_call(
        _rmsnorm_kernel,
        grid=(m // tm,),
        in_specs=[pl.BlockSpec((tm, d), lambda i: (i, 0)),
                  pl.BlockSpec((1, d), lambda i: (0, 0))],
        out_specs=pl.BlockSpec((tm, d), lambda i: (i, 0)),
        out_shape=jax.ShapeDtypeStruct((m, d), BF16),
        compiler_params=_params("parallel"),
        name="rmsnorm",
    )(x, w.reshape(1, d))


def _matmul_kernel(*refs, has_residual):
    if has_residual:
        a_ref, w_ref, r_ref, o_ref = refs
    else:
        a_ref, w_ref, o_ref = refs
    w_b = w_ref[...].astype(BF16)
    if not has_residual:
        o_ref[...] = jnp.dot(a_ref[...], w_b, preferred_element_type=F32).astype(o_ref.dtype)
        return
    rows = a_ref.shape[0]
    rb = math.gcd(rows, MATMUL_ROW_BLOCK)
    for r0 in range(0, rows, rb):
        acc = jnp.dot(a_ref[r0:r0 + rb, :], w_b, preferred_element_type=F32)
        o_ref[r0:r0 + rb, :] = (acc + r_ref[r0:r0 + rb, :]).astype(o_ref.dtype)


def matmul(a, w, *, k, n_out, tm=PROJ_ROW_TILE, tn=PROJ_COL_TILE, a_kblk=0, w_kblk=0, w_col0=0,
           residual=None, out_dtype=F32, name="matmul"):
    m = a.shape[0]
    tm = min(tm, m)
    tn = min(tn, n_out)
    assert m % tm == 0 and n_out % tn == 0 and w_col0 % tn == 0
    col0_blk = w_col0 // tn
    in_specs = [pl.BlockSpec((tm, k), lambda i, j: (i, a_kblk), pipeline_mode=pl.Buffered(1)),
                pl.BlockSpec((k, tn), lambda i, j: (w_kblk, j + col0_blk))]
    args = [a, w]
    if residual is not None:
        in_specs.append(pl.BlockSpec((tm, tn), lambda i, j: (i, j)))
        args.append(residual)
    return pl.pallas_call(
        functools.partial(_matmul_kernel, has_residual=residual is not None),
        grid=(m // tm, n_out // tn),
        in_specs=in_specs,
        out_specs=pl.BlockSpec((tm, tn), lambda i, j: (i, j)),
        out_shape=jax.ShapeDtypeStruct((m, n_out), out_dtype),
        compiler_params=_params("parallel", "arbitrary"),
        name=name,
    )(*args)


def _conv_silu(raw_ref, hist_ref, w_ref, b_ref, cols=slice(None)):
    l = raw_ref.shape[0]
    raw = raw_ref[:, cols]
    hist_ref[SUBLANES:, cols] = raw
    w = w_ref[:, cols]
    acc = b_ref[:, cols] + raw * w[SSD_CONV_WIDTH - 1:SSD_CONV_WIDTH, :]
    for j in range(1, SSD_CONV_WIDTH):
        acc = acc + hist_ref[pl.ds(SUBLANES - j, l), cols] * w[SSD_CONV_WIDTH - 1 - j:SSD_CONV_WIDTH - j, :]
    hist_ref[:SUBLANES, cols] = raw[l - SUBLANES:, :]
    return _silu(acc)


SSD_N_STREAMED = 6
SSD_N_PARAMS = 13
SSD_N_SCRATCH = 5


def _ssd_kernel(*refs, n_heads, head_dim, groups):
    n_in = SSD_N_STREAMED * groups
    params = refs[n_in:n_in + SSD_N_PARAMS]
    o_ref = refs[n_in + SSD_N_PARAMS]
    scratch = refs[n_in + SSD_N_PARAMS + 1:]
    gw = o_ref.shape[1] // groups
    chunk = pl.program_id(1)

    def group_params(e):
        grp = pl.program_id(0) * groups + e
        return [ref.at[grp] for ref in params[:-1]] + [params[-1]]

    @pl.when(chunk == 0)
    def _():
        for e in range(groups):
            state_ref, carx_ref, carb_ref, carc_ref, _ = scratch[e * SSD_N_SCRATCH:(e + 1) * SSD_N_SCRATCH]
            state_ref[...] = jnp.zeros_like(state_ref)
            for hist_ref in (carx_ref, carb_ref, carc_ref):
                hist_ref[:SUBLANES, :] = jnp.zeros((SUBLANES, hist_ref.shape[1]), F32)

    stages = [_ssd_group(*refs[e * SSD_N_STREAMED:(e + 1) * SSD_N_STREAMED], *group_params(e),
                         o_ref.at[:, e * gw:(e + 1) * gw],
                         *scratch[e * SSD_N_SCRATCH:(e + 1) * SSD_N_SCRATCH],
                         n_heads=n_heads, head_dim=head_dim) for e in range(groups)]
    while stages:
        stages = [stage for stage in stages if not next(stage, True)]


def _ssd_group(z_ref, x_ref, b_ref, c_ref, dt_ref, dtt_ref,
               cwx_ref, cbx_ref, cwb_ref, cbb_ref, cwc_ref, cbc_ref,
               bias_ref, biast_ref, alog_ref, alogt_ref, dexp_ref, nw_ref, e_ref,
               o_ref, state_ref, carx_ref, carb_ref, carc_ref, yz_ref, *, n_heads, head_dim):
    l = x_ref.shape[0]
    gw = x_ref.shape[1]
    bm = _conv_silu(b_ref, carb_ref, cwb_ref, cbb_ref)
    cm = _conv_silu(c_ref, carc_ref, cwc_ref, cbc_ref)

    log2e = math.log2(math.e)
    dt = _softplus(dt_ref[0] + bias_ref[0])
    a = dt * (-log2e * jnp.exp(alog_ref[0]))
    dtt = _softplus(dtt_ref[0] + biast_ref[0])
    at = dtt * (-log2e * jnp.exp(alogt_ref[0]))

    row = lax.broadcasted_iota(jnp.int32, (l, l), 0)
    col = lax.broadcasted_iota(jnp.int32, (l, l), 1)
    tril = row >= col
    tril_b = tril.astype(BF16)
    triu_b = (row <= col).astype(BF16)

    a_cs = sum(jnp.dot(tril_b, p, preferred_element_type=F32) for p in _split3(a))
    a_cst = sum(jnp.dot(p, triu_b, preferred_element_type=F32) for p in _split3(at))

    a_last = a_cs[l - 1:l, :]
    dte = jnp.exp2(a_last - a_cs)
    e3 = jnp.exp2(a_cs)

    stacked = jnp.concatenate([dt, dte, e3], axis=0)
    hi, mid, lo = _split3(stacked)
    grp = lax.broadcasted_iota(jnp.int32, stacked.shape, 1) // n_heads
    pieces = jnp.where(grp == 0, hi, jnp.where(grp == 1, mid, lo))

    cm_b = cm.astype(BF16)
    bmt_b = bm.T.astype(BF16)
    cb = lax.dot_general(cm_b, bm.astype(BF16), (((1,), (1,)), ((), ())),
                         preferred_element_type=F32)
    cb = jnp.where(tril, cb, 0.0)
    yield

    heads_per_vreg = LANES // head_dim
    lane = lax.broadcasted_iota(jnp.int32, (l, LANES), 1)
    stripe = min(SSD_STRIPE, gw)
    ssq = jnp.zeros((l, 1), F32)
    for c0 in range(0, gw, stripe):
        cols = slice(c0, c0 + stripe)
        xs = _conv_silu(x_ref, carx_ref, cwx_ref, cbx_ref, cols)
        expanded = jnp.dot(pieces, e_ref[:, cols], preferred_element_type=F32)
        dt_exp = expanded[:l]
        dte_exp = expanded[l:2 * l]
        e3_exp = expanded[2 * l:]
        xt = xs * dt_exp
        xt_b = xt.astype(BF16)
        yield

        y_cols = []
        for v in range(stripe // LANES):
            x_pair = xt_b[:, v * LANES:(v + 1) * LANES]
            acc = jnp.zeros((l, LANES), F32)
            for s in range(heads_per_vreg):
                h = (c0 // LANES + v) * heads_per_vreg + s
                seg = jnp.minimum(a_cs[:, h:h + 1] - a_cst[h:h + 1, :], 0.0)
                m_h = (cb * jnp.exp2(seg)).astype(BF16)
                in_head = (lane >= s * head_dim) & (lane < (s + 1) * head_dim)
                x_h = jnp.where(in_head, x_pair, jnp.zeros_like(x_pair))
                acc = acc + jnp.dot(m_h, x_h, preferred_element_type=F32)
            y_cols.append(acc)
        y_diag = jnp.concatenate(y_cols, axis=1) if len(y_cols) > 1 else y_cols[0]
        yield

        state = state_ref[:, cols]
        y_off = jnp.dot(cm_b, state.astype(BF16), preferred_element_type=F32) * e3_exp
        xw_b = (xt * dte_exp).astype(BF16)
        new_states = jnp.dot(bmt_b, xw_b, preferred_element_type=F32)
        state_ref[:, cols] = state * e3_exp[l - 1:l, :] + new_states

        y = y_diag + y_off + xs * dexp_ref[:, cols]
        yz = y * _silu(z_ref[:, cols])
        ssq = ssq + jnp.sum(yz * yz, axis=-1, keepdims=True)
        yz_ref[:, cols] = yz
        yield

    inv_rms = lax.rsqrt(ssq * (1.0 / gw) + RMS_EPS)
    o_ref[...] = (yz_ref[...] * inv_rms * nw_ref[...]).astype(o_ref.dtype)


def ssd_core(zxbc, dt_raw, conv_w, conv_b, dt_bias, a_log, d_skip, norm_w):
    s = zxbc.shape[0]
    g, n, p, l = SSD_N_GROUPS, SSD_D_STATE, SSD_HEAD_DIM, SSD_CHUNK
    h = dt_raw.shape[1]
    r = h // g
    gw = r * p
    di = h * p
    l = min(l, s)
    assert gw % LANES == 0 and n % LANES == 0 and di == g * gw
    nx = di // gw
    nb0 = 2 * di // n
    nc0 = nb0 + g

    dtg = dt_raw.reshape(s, g, r).transpose(1, 0, 2)
    dtg3 = jnp.concatenate([dtg, dtg, dtg], axis=-1)
    dtt = dt_raw.reshape(s, g, r).transpose(1, 2, 0)
    kw = SSD_CONV_WIDTH
    bias = dt_bias.reshape(g, 1, 1, r)
    bias3 = jnp.concatenate([bias, bias, bias], axis=-1)
    alog = a_log.reshape(g, 1, 1, r)
    alog3 = jnp.concatenate([alog, alog, alog], axis=-1)
    biast = dt_bias.reshape(g, 1, r, 1)
    alogt = a_log.reshape(g, 1, r, 1)
    dexp = jnp.repeat(d_skip, p).reshape(g, 1, gw)
    nw = norm_w.reshape(g, 1, gw)
    per_group = lambda a, width: a.reshape(a.shape[0], g, width).transpose(1, 0, 2)
    cb_row = conv_b.reshape(1, -1)
    cwx, cbx = per_group(conv_w[:, :di], gw), per_group(cb_row[:, :di], gw)
    cwb, cbb = per_group(conv_w[:, di:di + g * n], n), per_group(cb_row[:, di:di + g * n], n)
    cwc, cbc = per_group(conv_w[:, di + g * n:], n), per_group(cb_row[:, di + g * n:], n)
    head_of_col = jnp.arange(gw) // p
    e1 = (head_of_col[None, :] == jnp.arange(r)[:, None]).astype(BF16)
    e3 = jnp.concatenate([e1, e1, e1], axis=0)
    params = (cwx, cbx, cwb, cbb, cwc, cbc, bias3, biast, alog3, alogt, dexp, nw, e3)
    assert len(params) == SSD_N_PARAMS

    groups = math.gcd(g, SSD_GROUPS_PER_STEP)

    def group_specs(e):
        grp = lambda gs: gs * groups + e
        specs = [
            pl.BlockSpec((l, gw), lambda gs, c: (c, grp(gs))),
            pl.BlockSpec((l, gw), lambda gs, c: (c, nx + grp(gs))),
            pl.BlockSpec((l, n), lambda gs, c: (c, nb0 + grp(gs))),
            pl.BlockSpec((l, n), lambda gs, c: (c, nc0 + grp(gs))),
            pl.BlockSpec((1, l, 3 * r), lambda gs, c: (grp(gs), c, 0)),
            pl.BlockSpec((1, r, l), lambda gs, c: (grp(gs), 0, c)),
        ]
        assert len(specs) == SSD_N_STREAMED
        return specs

    streamed = (zxbc, zxbc, zxbc, zxbc, dtg3, dtt)
    scratch = [pltpu.VMEM((n, gw), F32),
               pltpu.VMEM((SUBLANES + l, gw), F32),
               pltpu.VMEM((SUBLANES + l, n), F32),
               pltpu.VMEM((SUBLANES + l, n), F32),
               pltpu.VMEM((l, gw), F32)]
    assert len(scratch) == SSD_N_SCRATCH
    return pl.pallas_call(
        functools.partial(_ssd_kernel, n_heads=r, head_dim=p, groups=groups),
        grid=(g // groups, s // l),
        in_specs=([spec for e in range(groups) for spec in group_specs(e)]
                  + [pl.BlockSpec(memory_space=pltpu.VMEM)] * SSD_N_PARAMS),
        out_specs=pl.BlockSpec((l, groups * gw), lambda gs, c: (c, gs)),
        out_shape=jax.ShapeDtypeStruct((s, di), BF16),
        scratch_shapes=scratch * groups,
        compiler_params=_params("parallel", "arbitrary"),
        name="ssd_core",
    )(*(streamed * groups), *params)


def ssd_mixer(h, norm_w_in, w_in, conv_w, conv_b, dt_bias, a_log, d_skip, norm_w, w_out):
    d = h.shape[1]
    n_heads = dt_bias.shape[0]
    di = n_heads * SSD_HEAD_DIM
    n_main = 2 * di + 2 * SSD_N_GROUPS * SSD_D_STATE
    u = rmsnorm(h, norm_w_in)
    zxbc = matmul(u, w_in, k=d, n_out=n_main, name="ssd_in_proj")
    dt_raw = matmul(u, w_in, k=d, n_out=n_heads, tn=LANES, w_col0=n_main,
                    name="ssd_dt_proj")
    y = ssd_core(zxbc, dt_raw, conv_w, conv_b, dt_bias, a_log, d_skip, norm_w)
    for kb in range(di // d):
        h = matmul(y, w_out, k=d, n_out=d, a_kblk=kb, w_kblk=kb,
                   residual=h, name="ssd_out_proj")
    return h


def _pool_diff_kernel(v_ref, halo_ref, o_ref, *, group_dim, chunk):
    i = pl.program_id(0)
    tm = v_ref.shape[0]
    pos = i * tm + lax.broadcasted_iota(jnp.int32, (tm, chunk), 0)
    for g, window in enumerate(POOL_WINDOWS):
        inv_count = 1.0 / jnp.minimum(pos + 1, window).astype(F32)
        for c0 in range(g * group_dim, (g + 1) * group_dim, chunk):
            v = v_ref[:, c0:c0 + chunk]
            halo = jnp.where(i > 0, halo_ref[:, c0:c0 + chunk], 0.0)
            sw = jnp.concatenate([halo, v], axis=0)
            shift = 1
            while shift < window:
                sw = sw + pltpu.roll(sw, shift, 0)
                shift *= 2
            d = sw[POOL_HALO:] * inv_count - v
            o_ref[:, c0:c0 + chunk] = d.astype(o_ref.dtype)


def pool_diff(proj, width, group_dim, *, tm=256):
    s = proj.shape[0]
    tm = min(tm, s)
    chunk = min(512, group_dim)
    halo_blocks = tm // POOL_HALO
    return pl.pallas_call(
        functools.partial(_pool_diff_kernel, group_dim=group_dim, chunk=chunk),
        grid=(s // tm,),
        in_specs=[pl.BlockSpec((tm, width), lambda i: (i, 0)),
                  pl.BlockSpec((POOL_HALO, width),
                               lambda i: (jnp.maximum(i * halo_blocks - 1, 0), 0))],
        out_specs=pl.BlockSpec((tm, width), lambda i: (i, 0)),
        out_shape=jax.ShapeDtypeStruct((s, width), BF16),
        compiler_params=_params("parallel"),
        name="pool_diff",
    )(proj, proj)


def _pool_group_kernel(d_ref, w_ref, b_ref, sc_ref, z_ref, o_ref):
    w_b = w_ref[0].astype(BF16)
    bias = b_ref[0]
    scale = sc_ref[...]
    rows = d_ref.shape[0]
    rb = math.gcd(rows, MATMUL_ROW_BLOCK)
    for r0 in range(0, rows, rb):
        acc = jnp.dot(d_ref[r0:r0 + rb, :], w_b, preferred_element_type=F32)
        acc = (acc + bias) * scale
        o_ref[r0:r0 + rb, :] = (acc * _silu(z_ref[r0:r0 + rb, :].astype(F32))).astype(o_ref.dtype)


def pool_core(v, z, w_grp, b_grp, scale, *, tm=4096, tn=512):
    s = v.shape[0]
    ng, gd, _ = w_grp.shape
    assert ng == len(POOL_WINDOWS) and max(POOL_WINDOWS) <= POOL_HALO
    width = ng * gd
    d = pool_diff(v, width, gd)
    tm = min(tm, s)
    tn = min(tn, gd)
    nj = gd // tn
    in_specs = [
        pl.BlockSpec((tm, gd), lambda g, i, j: (i, g), pipeline_mode=pl.Buffered(1)),
        pl.BlockSpec((1, gd, tn), lambda g, i, j: (g, 0, j)),
        pl.BlockSpec((1, 1, tn), lambda g, i, j: (g, 0, j)),
        pl.BlockSpec((1, tn), lambda g, i, j: (0, g * nj + j)),
        pl.BlockSpec((tm, tn), lambda g, i, j: (i, g * nj + j)),
    ]
    return pl.pallas_call(
        _pool_group_kernel,
        grid=(ng, s // tm, nj),
        in_specs=in_specs,
        out_specs=pl.BlockSpec((tm, tn), lambda g, i, j: (i, g * nj + j)),
        out_shape=jax.ShapeDtypeStruct((s, width), BF16),
        compiler_params=_params("parallel", "parallel", "arbitrary"),
        name="pool_group",
    )(d, w_grp, b_grp.reshape(ng, 1, gd), scale.reshape(1, width), z)


def pool_mixer(h, norm_w_in, w_in, w_grp, b_grp, scale, w_out):
    d = h.shape[1]
    width = scale.shape[0]
    u = rmsnorm(h, norm_w_in)
    v = matmul(u, w_in, k=d, n_out=width, name="pool_v_proj")
    z = matmul(u, w_in, k=d, n_out=width, w_col0=width, out_dtype=BF16, name="pool_z_proj")
    y = pool_core(v, z, w_grp, b_grp, scale)
    for kb in range(width // d):
        h = matmul(y, w_out, k=d, n_out=d, a_kblk=kb, w_kblk=kb,
                   residual=h, name="pool_out_proj")
    return h


def _moba_prep_kernel(q_ref, k_ref, v_ref, qw_ref, kw_ref, qs_ref, ka_ref, va_ref, km_ref):
    i = pl.program_id(0)

    @pl.when(i == 0)
    def _():
        km_ref[...] = jnp.zeros_like(km_ref)

    dh = MOBA_HEAD_DIM
    rows = q_ref.shape[0]
    n_heads = q_ref.shape[1] // dh
    scale = math.log2(math.e) / math.sqrt(dh)
    qw = qw_ref[...] * scale
    kw = kw_ref[...]
    lane = lax.broadcasted_iota(jnp.int32, (rows, dh), 1)
    onehot = jnp.where(lane == i, 1.0, 0.0).astype(ka_ref.dtype)
    ones = jnp.ones((rows, dh), va_ref.dtype)
    means = []
    for h in range(n_heads):
        sl = slice(h * dh, (h + 1) * dh)
        lo = slice(2 * h * dh, (2 * h + 1) * dh)
        hi = slice((2 * h + 1) * dh, (2 * h + 2) * dh)
        q = q_ref[:, sl].astype(F32)
        qs = q * lax.rsqrt(jnp.mean(q * q, axis=-1, keepdims=True) + RMS_EPS) * qw
        qs_ref[:, sl] = qs.astype(qs_ref.dtype)
        k = k_ref[:, sl].astype(F32)
        kn = k * lax.rsqrt(jnp.mean(k * k, axis=-1, keepdims=True) + RMS_EPS) * kw
        ka_ref[:, lo] = kn.astype(ka_ref.dtype)
        ka_ref[:, hi] = onehot
        va_ref[:, lo] = v_ref[:, sl].astype(va_ref.dtype)
        va_ref[:, hi] = ones
        means.append(jnp.mean(kn, axis=0, keepdims=True))
    mean_row = jnp.concatenate(means, axis=1)
    blk = lax.broadcasted_iota(jnp.int32, km_ref.shape, 0)
    km_ref[...] = jnp.where(blk == i, mean_row, km_ref[...])


def moba_prep(proj, q_norm_w, k_norm_w):
    s = proj.shape[0]
    width = proj.shape[1] // 4
    blk = MOBA_BLOCK
    nb = s // blk
    assert nb <= MOBA_HEAD_DIM
    row_spec = lambda c: pl.BlockSpec((blk, width), lambda i: (i, c))
    return pl.pallas_call(
        _moba_prep_kernel,
        grid=(nb,),
        in_specs=[row_spec(0), row_spec(1), row_spec(2),
                  pl.BlockSpec((1, MOBA_HEAD_DIM), lambda i: (0, 0)),
                  pl.BlockSpec((1, MOBA_HEAD_DIM), lambda i: (0, 0))],
        out_specs=[pl.BlockSpec((blk, width), lambda i: (i, 0)),
                   pl.BlockSpec((blk, 2 * width), lambda i: (i, 0)),
                   pl.BlockSpec((blk, 2 * width), lambda i: (i, 0)),
                   pl.BlockSpec((nb, width), lambda i: (0, 0))],
        out_shape=[jax.ShapeDtypeStruct((s, width), BF16),
                   jax.ShapeDtypeStruct((s, 2 * width), BF16),
                   jax.ShapeDtypeStruct((s, 2 * width), BF16),
                   jax.ShapeDtypeStruct((nb, width), F32)],
        compiler_params=_params("arbitrary"),
        name="moba_prep",
    )(proj, proj, proj, q_norm_w.reshape(1, -1), k_norm_w.reshape(1, -1))


def _moba_attn_kernel(qa_in_ref, qb_in_ref, k_ref, v_ref, km_ref, ga_ref, gb_ref, oa_ref, ob_ref,
                      *scratch, pairs):
    g = pl.program_id(1)
    tq = MOBA_BLOCK
    per_pair = len(scratch) // pairs
    stages = {}
    for e in range(pairs):
        a_rows = slice(e * tq, (e + 1) * tq)
        b_rows = slice((pairs - 1 - e) * tq, (pairs - e) * tq)
        stages[e] = _moba_pair(
            pairs * g + e, qa_in_ref.at[a_rows], qb_in_ref.at[b_rows], k_ref, v_ref, km_ref,
            ga_ref.at[a_rows], gb_ref.at[b_rows], oa_ref.at[a_rows], ob_ref.at[b_rows],
            *scratch[e * per_pair:(e + 1) * per_pair])
    tick = 0
    while stages:
        for e in sorted(stages):
            if tick >= e * MOBA_STAGE_LAG and next(stages[e], True):
                del stages[e]
        tick += 1


def _moba_pair(p, qa_in_ref, qb_in_ref, k_ref, v_ref, km_ref, ga_ref, gb_ref, oa_ref, ob_ref,
               s_ref, sstat_ref, sown_ref, qa_ref, mrun_ref, mstat_ref):
    blk = MOBA_BLOCK
    dh = MOBA_HEAD_DIM
    tq = qa_in_ref.shape[0]
    nb = km_ref.shape[0]
    n_dynamic = nb // 2 - 1
    nt = (((1,), (1,)), ((), ()))
    q_blocks = (p, nb - 1 - p)
    q2 = jnp.concatenate([qa_in_ref[...], qb_in_ref[...]], axis=0)

    gscore = lax.dot_general(km_ref[...].astype(BF16), q2, nt, preferred_element_type=F32)
    brow = lax.broadcasted_iota(jnp.int32, (nb, 2 * tq), 0)
    qcol = lax.broadcasted_iota(jnp.int32, (nb, 2 * tq), 1)
    browf = brow.astype(F32)
    own_blk = jnp.where(qcol < tq, q_blocks[0], q_blocks[1])
    gwork = jnp.where(brow < own_blk, gscore, -jnp.inf)
    sel = jnp.zeros((nb, 2 * tq), F32)
    for _ in range(min(MOBA_TOPK, nb)):
        top = jnp.max(gwork, axis=0, keepdims=True)
        is_top = (gwork == top) & (top > -jnp.inf)
        first = jnp.min(jnp.where(is_top, browf, float(nb)), axis=0, keepdims=True)
        pick = browf == first
        sel = jnp.where(pick, 1.0, sel)
        gwork = jnp.where(pick, -jnp.inf, gwork)
    bias_t = jnp.where(sel > 0.5, 0.0, MOBA_MASKED)
    if nb < dh:
        bias_t = jnp.concatenate([bias_t, jnp.zeros((dh - nb, 2 * tq), F32)], axis=0)
    q2a = jnp.concatenate([q2, bias_t.T.astype(BF16)], axis=1)
    yield

    def lane_max(s):
        m = s[:, :LANES]
        for c0 in range(LANES, blk, LANES):
            m = jnp.maximum(m, s[:, c0:c0 + LANES])
        return m

    r = lax.broadcasted_iota(jnp.int32, (tq, blk), 0)
    c = lax.broadcasted_iota(jnp.int32, (tq, blk), 1)
    own_start = []
    for t in range(2):
        qa_ref[t] = q2a[t * tq:(t + 1) * tq]
        start = pl.multiple_of(q_blocks[t] * blk, blk)
        own_start.append(start)
        s_own = lax.dot_general(q2[t * tq:(t + 1) * tq], k_ref[pl.ds(start, blk), pl.ds(0, dh)], nt,
                                preferred_element_type=F32)
        s_own = jnp.where(c <= r, s_own, MOBA_MASKED)
        sown_ref[t] = s_own
        mrun_ref[t] = lane_max(s_own)
    yield

    n_static = nb // 2
    sc_static = lax.dot_general(qa_ref[1], k_ref[pl.ds(0, n_static * blk), :], nt,
                                preferred_element_type=F32)
    sstat_ref[...] = sc_static
    m_a = mrun_ref[0]
    m_b = mrun_ref[1]
    for c0 in range(0, n_static * blk, LANES):
        m_b = jnp.maximum(m_b, sc_static[:, c0:c0 + LANES])
    m_b_static = jnp.broadcast_to(jnp.max(m_b, axis=1, keepdims=True), (tq, LANES))
    mstat_ref[...] = m_b_static
    yield

    def slot(u):
        is_a = u < p
        tile = jnp.where(is_a, 0, 1)
        kj = jnp.where(is_a, u, n_static + u - p)
        return is_a, tile, pl.multiple_of(kj * blk, blk)

    def probs(sc, m_lanes):
        return jnp.concatenate([jnp.exp2(sc[:, c0:c0 + LANES] - m_lanes)
                                for c0 in range(0, sc.shape[1], LANES)], axis=1).astype(BF16)

    def static_part():
        return (jnp.dot(probs(sown_ref[1], mstat_ref[...]), v_ref[pl.ds(own_start[1], blk), :],
                        preferred_element_type=F32)
                + jnp.dot(probs(sstat_ref[...], mstat_ref[...]), v_ref[pl.ds(0, n_static * blk), :],
                          preferred_element_type=F32))

    acc_b_static = None
    for u in range(n_dynamic):
        is_a, tile, start = slot(u)
        sc = lax.dot_general(qa_ref[tile], k_ref[pl.ds(start, blk), :], nt,
                             preferred_element_type=F32)
        s_ref[u] = sc
        sm = lane_max(sc)
        m_a = jnp.maximum(m_a, jnp.where(is_a, sm, MOBA_MASKED))
        m_b = jnp.maximum(m_b, jnp.where(is_a, MOBA_MASKED, sm))
        if u % MOBA_STAGE_SLOTS == MOBA_STAGE_SLOTS - 1:
            yield
            if acc_b_static is None:
                acc_b_static = static_part()
                yield
    if acc_b_static is None:
        acc_b_static = static_part()

    m_a = jnp.broadcast_to(jnp.max(m_a, axis=1, keepdims=True), (tq, LANES))
    m_b = jnp.maximum(jnp.broadcast_to(jnp.max(m_b, axis=1, keepdims=True), (tq, LANES)),
                      mstat_ref[...])
    mrun_ref[0] = m_a
    mrun_ref[1] = m_b
    yield

    rescale = jnp.exp2(mstat_ref[...] - mrun_ref[1])
    acc = [jnp.dot(probs(sown_ref[0], mrun_ref[0]), v_ref[pl.ds(own_start[0], blk), :],
                   preferred_element_type=F32),
           jnp.concatenate([acc_b_static[:, c0:c0 + LANES] * rescale
                            for c0 in range(0, 2 * dh, LANES)], axis=1)]
    yield
    for u in range(n_dynamic):
        is_a, tile, start = slot(u)
        part = jnp.dot(probs(s_ref[u], mrun_ref[tile]), v_ref[pl.ds(start, blk), :],
                       preferred_element_type=F32)
        zero = jnp.zeros_like(part)
        acc = [acc[0] + jnp.where(is_a, part, zero), acc[1] + jnp.where(is_a, zero, part)]
        if u % MOBA_STAGE_SLOTS == MOBA_STAGE_SLOTS - 1:
            yield

    for t, (g_ref, o_ref) in enumerate(((ga_ref, oa_ref), (gb_ref, ob_ref))):
        out = acc[t][:, :dh] / acc[t][:, dh:]
        o_ref[...] = (out * _silu(g_ref[...].astype(F32))).astype(o_ref.dtype)


def moba_attn(qs, ka, va, kmean, proj):
    s, width = qs.shape
    dh = MOBA_HEAD_DIM
    n_heads = width // dh
    blk = MOBA_BLOCK
    tq = blk
    nb = kmean.shape[0]
    assert nb % 2 == 0
    half = nb // 2
    pairs = math.gcd(half, MOBA_PAIRS)
    steps = half // pairs
    rows = pairs * tq
    gate0 = 3 * n_heads
    out_a, out_b = pl.pallas_call(
        functools.partial(_moba_attn_kernel, pairs=pairs),
        grid=(n_heads, steps),
        in_specs=[pl.BlockSpec((rows, dh), lambda h, g: (g, h)),
                  pl.BlockSpec((rows, dh), lambda h, g: (2 * steps - 1 - g, h)),
                  pl.BlockSpec((s, 2 * dh), lambda h, g: (0, h)),
                  pl.BlockSpec((s, 2 * dh), lambda h, g: (0, h)),
                  pl.BlockSpec((nb, dh), lambda h, g: (0, h)),
                  pl.BlockSpec((rows, dh), lambda h, g: (g, gate0 + h)),
                  pl.BlockSpec((rows, dh), lambda h, g: (2 * steps - 1 - g, gate0 + h))],
        out_specs=[pl.BlockSpec((rows, dh), lambda h, g: (g, h)),
                   pl.BlockSpec((rows, dh), lambda h, g: (steps - 1 - g, h))],
        out_shape=[jax.ShapeDtypeStruct((s // 2, width), BF16),
                   jax.ShapeDtypeStruct((s // 2, width), BF16)],
        scratch_shapes=[pltpu.VMEM((max(half - 1, 1), tq, blk), F32),
                        pltpu.VMEM((tq, half * blk), F32),
                        pltpu.VMEM((2, tq, blk), F32),
                        pltpu.VMEM((2, tq, 2 * dh), BF16),
                        pltpu.VMEM((2, tq, LANES), F32),
                        pltpu.VMEM((tq, LANES), F32)] * pairs,
        compiler_params=_params("parallel", "arbitrary"),
        name="moba_attn",
    )(qs, qs, ka, va, kmean, proj, proj)
    return jnp.concatenate([out_a, out_b], axis=0)


def moba_mixer(h, norm_w_in, w_in, q_norm_w, k_norm_w, w_out):
    s, d = h.shape
    assert s % MOBA_BLOCK == 0
    width = w_out.shape[0]
    u = rmsnorm(h, norm_w_in)
    proj = matmul(u, w_in, k=d, n_out=4 * width, out_dtype=BF16,
                  name="moba_in_proj")
    qs, ka, va, kmean = moba_prep(proj, q_norm_w, k_norm_w)
    y = moba_attn(qs, ka, va, kmean, proj)
    for kb in range(width // d):
        h = matmul(y, w_out, k=d, n_out=d, a_kblk=kb, w_kblk=kb,
                   residual=h, name="moba_out_proj")
    return h


def kernel(x, norm0, ssd0_w_in, ssd0_conv_w, ssd0_conv_b, ssd0_dt_bias, ssd0_a_log, ssd0_d, ssd0_norm_w, ssd0_w_out, norm1, pool1_w_in, pool1_w_grp, pool1_b_grp, pool1_scale, pool1_w_out, norm2, moba2_w_in, moba2_q_norm, moba2_k_norm, moba2_w_out, norm3, ssd3_w_in, ssd3_conv_w, ssd3_conv_b, ssd3_dt_bias, ssd3_a_log, ssd3_d, ssd3_norm_w, ssd3_w_out):
    b, s, d = x.shape
    outs = []
    for bi in range(b):
        h = x[bi]
        h = ssd_mixer(h, norm0, ssd0_w_in, ssd0_conv_w, ssd0_conv_b, ssd0_dt_bias,
                      ssd0_a_log, ssd0_d, ssd0_norm_w, ssd0_w_out)
        h = pool_mixer(h, norm1, pool1_w_in, pool1_w_grp, pool1_b_grp, pool1_scale, pool1_w_out)
        h = moba_mixer(h, norm2, moba2_w_in, moba2_q_norm, moba2_k_norm, moba2_w_out)
        h = ssd_mixer(h, norm3, ssd3_w_in, ssd3_conv_w, ssd3_conv_b, ssd3_dt_bias,
                      ssd3_a_log, ssd3_d, ssd3_norm_w, ssd3_w_out)
        outs.append(h)
    return jnp.stack(outs, axis=0)
```
